```python
import jax, jax.numpy as jnp
from jax import lax
import numpy as np

D_MODEL = 4096
BATCH = 1
SEQ = 8192
DEPTH = 2

CTX_LEN = 256
GRID_W = 64
N_BRANCH = 4
BRANCH_W = D_MODEL // N_BRANCH
A_HEAD = 64
A_HEADS = BRANCH_W // A_HEAD
A_DECAY_LORA = 64
A_ICLR_LORA = 64
A_GATE_LORA = 160
A_GN_EPS = 64e-5
A_PROJ = 3 * BRANCH_W + A_DECAY_LORA + A_ICLR_LORA + A_GATE_LORA
A_SPLITS = [BRANCH_W, 2 * BRANCH_W, 3 * BRANCH_W, 3 * BRANCH_W + A_DECAY_LORA,
            3 * BRANCH_W + A_DECAY_LORA + A_ICLR_LORA]
B_CHUNK = 128
B_GROUP_CH = 128
B_GROUPS = BRANCH_W // B_GROUP_CH
B_PROJ = 2 * BRANCH_W
C_KERNEL = 31
C_PROJ = 2 * BRANCH_W
D_GROUPS = 4
D_GROUP_CH = BRANCH_W // D_GROUPS
D_PROJ = BRANCH_W
P_IN = A_PROJ + B_PROJ + C_PROJ + D_PROJ
IN_SPLITS = [A_PROJ, A_PROJ + B_PROJ, A_PROJ + B_PROJ + C_PROJ]
FFN_HIDDEN = -(-(8 * D_MODEL) // (3 * 256)) * 256
RMS_EPS = 1e-6
LN_EPS = 1e-5

kernel_name = "hybrid_rwkv7_gmlp_conformer_fnet_block"


def _rmsnorm(x, g):
    xf = x.astype(jnp.float32)
    y = xf * lax.rsqrt(jnp.mean(xf * xf, axis=-1, keepdims=True) + RMS_EPS)
    return y.astype(x.dtype) * g


def _layernorm(x, g, b):
    xf = x.astype(jnp.float32)
    mu = jnp.mean(xf, axis=-1, keepdims=True)
    var = jnp.mean(jnp.square(xf - mu), axis=-1, keepdims=True)
    return ((xf - mu) * lax.rsqrt(var + LN_EPS)).astype(x.dtype) * g + b


def _modulate(xn, shift, scale):
    return xn * (1.0 + scale) + shift


def _neighbour_mean(x):
    z = jnp.zeros_like(x[..., :1, :])
    prev = jnp.concatenate([z, x[..., :-1, :]], axis=-2)
    nxt = jnp.concatenate([x[..., 1:, :], z], axis=-2)
    return 0.5 * (prev + nxt)


def _token_shift(x, mu, rows):
    if rows is None:
        nb = _neighbour_mean(x)
    else:
        bsz, t, ch = x.shape
        nb = _neighbour_mean(x.reshape(bsz, rows, GRID_W, ch)).reshape(bsz, t, ch)
    return x + mu * (nb - x)


def _wkv7_scan(r, decay, k, v, a, b, s0, reverse):
    def step(state, inp):
        r_t, w_t, k_t, v_t, a_t, b_t = inp
        sa = jnp.einsum("bhvk,bhk->bhv", state, a_t)
        state = (state * w_t[:, :, None, :] + sa[..., None] * b_t[:, :, None, :]
                 + v_t[..., None] * k_t[:, :, None, :])
        return state, jnp.einsum("bhvk,bhk->bhv", state, r_t)
    xs = tuple(jnp.swapaxes(z, 0, 1) for z in (r, decay, k, v, a, b))
    s_final, ys = lax.scan(step, s0, xs, reverse=reverse)
    return jnp.swapaxes(ys, 0, 1), s_final


def _zero_states(bsz):
    return jnp.zeros((2, bsz, A_HEADS, A_HEAD, A_HEAD), jnp.float32)


def _rwkv7_core(xa, s0, lp):
    bsz, t, _ = xa.shape
    heads = lambda z: z.reshape(bsz, t, A_HEADS, A_HEAD)
    r, k, v, wc, ac, gc = jnp.split(xa, A_SPLITS, axis=-1)
    kk = heads(k * lp["a_k_k"])
    kk = kk * lax.rsqrt(jnp.maximum(jnp.sum(kk * kk, axis=-1, keepdims=True), 1e-12))
    rh, vh = heads(r), heads(v)
    ys, kds, finals = [], [], []
    for d in range(2):
        w = -jax.nn.softplus(-(lp["a_w0"][d] + jnp.tanh(wc) @ lp["a_w_up"][d])) - 0.5
        a = jax.nn.sigmoid(lp["a_a0"][d] + ac @ lp["a_a_up"][d])
        kd = heads(k * (1.0 + (a - 1.0) * lp["a_k_a"]))
        y, s_fin = _wkv7_scan(rh, heads(jnp.exp(-jnp.exp(w))), kd, vh, -kk, kk * heads(a),
                              s0[d], reverse=(d == 1))
        ys.append(y)
        kds.append(kd)
        finals.append(s_fin)
    return rh, vh, gc, ys[0] + ys[1], kds, jnp.stack(finals)


def _rwkv7_readout(rh, vh, gc, y_sum, kds, lp, out_dtype):
    bsz, t = rh.shape[:2]
    mu = jnp.mean(y_sum, axis=-1, keepdims=True)
    var = jnp.mean(jnp.square(y_sum - mu), axis=-1, keepdims=True)
    yn = ((y_sum - mu) * lax.rsqrt(var + A_GN_EPS)).reshape(bsz, t, BRANCH_W) * lp["a_ln"][0] + lp["a_ln"][1]
    bonus_f = jnp.sum(rh * kds[0] * lp["a_r_k"], axis=-1, keepdims=True) * vh
    bonus_b = jnp.sum(rh * kds[1] * lp["a_r_k"], axis=-1, keepdims=True) * vh
    g = jax.nn.sigmoid(gc) @ lp["a_g_up"]
    out = (yn + (bonus_f + bonus_b).reshape(bsz, t, BRANCH_W)) * g
    return out.astype(out_dtype) @ lp["a_w_out"]


def _chunk_mlp_mixer(pb, lp):
    bsz, t, _ = pb.shape
    u, v = jnp.split(jax.nn.gelu(pb), 2, axis=-1)
    v = _layernorm(v, lp["b_ln"][0], lp["b_ln"][1])
    vc = v.reshape(bsz, t // B_CHUNK, B_CHUNK, B_GROUPS, B_GROUP_CH)
    s = jnp.einsum("gpq,bnqgc->bnpgc", lp["b_ws"], vc) + jnp.swapaxes(lp["b_bs"], 0, 1)[:, :, None]
    return (u * s.reshape(bsz, t, BRANCH_W)) @ lp["b_w_out"]


def _conv_mixer(pc, lp):
    a, gt = jnp.split(pc, 2, axis=-1)
    z = a * jax.nn.sigmoid(gt)
    rhs = lp["c_dw"][:, None, :].astype(z.dtype)
    z = lax.conv_general_dilated(z, rhs, window_strides=(1,),
                                 padding=[(C_KERNEL // 2, C_KERNEL // 2)],
                                 dimension_numbers=("NWC", "WIO", "NWC"),
                                 feature_group_count=BRANCH_W) + lp["c_dw_b"]
    z = jax.nn.silu(_layernorm(z, lp["c_ln"][0], lp["c_ln"][1]))
    return z @ lp["c_w_out"]


def _fourier_mixer(pd, lp):
    bsz, t, _ = pd.shape
    z = pd.astype(jnp.float32).reshape(bsz, t, D_GROUPS, D_GROUP_CH)
    f = jnp.fft.fft2(z, axes=(1, 3), norm="ortho").real
    return f.reshape(bsz, t, BRANCH_W).astype(pd.dtype) @ lp["d_w_out"]


def _token_mix(h, s0, rows, lp):
    pa, pb, pc, pd = jnp.split(h @ lp["w_in"], IN_SPLITS, axis=-1)
    xa = _token_shift(pa, lp["a_mu"], rows).astype(jnp.float32)
    rh, vh, gc, y_sum, kds, finals = _rwkv7_core(xa, s0, lp)
    ya = _rwkv7_readout(rh, vh, gc, y_sum, kds, lp, h.dtype)
    yb = _chunk_mlp_mixer(pb, lp)
    yc = _conv_mixer(pc, lp)
    yd = _fourier_mixer(pd, lp)
    gates = jax.nn.sigmoid(h @ lp["gate_w"] + lp["gate_b"])
    ga, gb, gcv, gd = jnp.split(gates, N_BRANCH, axis=-1)
    merged = ga * ya + gb * yb + gcv * yc + gd * yd
    return merged @ lp["w_o"], finals


def _context_scan_states(h, lp):
    pa = h @ lp["w_in"][:, :A_PROJ]
    xa = _token_shift(pa, lp["a_mu"], None).astype(jnp.float32)
    return _rwkv7_core(xa, _zero_states(h.shape[0]), lp)[5]


def _ffn_sublayer(x, shift, scale, gate, g_pre, g_post, w1, w3, w2):
    h = _modulate(_rmsnorm(x, g_pre), shift, scale)
    y = (jax.nn.silu(h @ w1) * (h @ w3)) @ w2
    return x + gate * _rmsnorm(y, g_post)


def setup_inputs(seed: int = 0) -> dict:
    key = jax.random.key(seed)
    ks = iter(jax.random.split(key, 48))
    L, D, W = DEPTH, D_MODEL, BRANCH_W

    def nrm(shape, scale):
        return jax.random.normal(next(ks), shape, jnp.float32) * scale

    def gain_bias(shape):
        return jnp.stack([1.0 + nrm(shape, 0.05), nrm(shape, 0.02)], axis=1)

    return {
        "x": nrm((BATCH, SEQ, D), 1.0),
        "c": nrm((BATCH, D), 1.0),
        "ctx": nrm((BATCH, CTX_LEN, D), 1.0),
        "c_ctx": nrm((D,), 1.0),
        "mod_w": nrm((L, D, 6 * D), D ** -0.5),
        "mod_b": nrm((L, 6 * D), 0.02),
        "norm_g": 1.0 + nrm((L, 4, D), 0.05),
        "w_in": nrm((L, D, P_IN), D ** -0.5),
        "a_mu": jax.random.uniform(next(ks), (L, A_PROJ), jnp.float32),
        "a_w0": jax.random.uniform(next(ks), (L, 2, W), jnp.float32, -6.0, -1.0),
        "a_w_up": nrm((L, 2, A_DECAY_LORA, W), 0.5 * A_DECAY_LORA ** -0.5),
        "a_a0": nrm((L, 2, W), 0.5),
        "a_a_up": nrm((L, 2, A_ICLR_LORA, W), 0.5 * A_ICLR_LORA ** -0.5),
        "a_g_up": nrm((L, A_GATE_LORA, W), A_GATE_LORA ** -0.5),
        "a_k_k": 0.85 + nrm((L, W), 0.05),
        "a_k_a": 1.0 + nrm((L, W), 0.05),
        "a_r_k": nrm((L, A_HEADS, A_HEAD), 0.1),
        "a_ln": gain_bias((L, W)),
        "a_w_out": nrm((L, W, D), W ** -0.5),
        "b_ln": gain_bias((L, W)),
        "b_ws": nrm((L, B_GROUPS, B_CHUNK, B_CHUNK), B_CHUNK ** -0.5),
        "b_bs": 1.0 + nrm((L, B_GROUPS, B_CHUNK), 0.05),
        "b_w_out": nrm((L, W, D), W ** -0.5),
        "c_dw": nrm((L, C_KERNEL, W), C_KERNEL ** -0.5),
        "c_dw_b": nrm((L, W), 0.02),
        "c_ln": gain_bias((L, W)),
        "c_w_out": nrm((L, W, D), W ** -0.5),
        "d_w_out": nrm((L, W, D), W ** -0.5),
        "gate_w": nrm((L, D, N_BRANCH * D), D ** -0.5),
        "gate_b": nrm((L, N_BRANCH * D), 0.02),
        "w_o": nrm((L, D, D), D ** -0.5),
        "ffn_w1": nrm((L, D, FFN_HIDDEN), D ** -0.5),
        "ffn_w3": nrm((L, D, FFN_HIDDEN), D ** -0.5),
        "ffn_w2": nrm((L, FFN_HIDDEN, D), FFN_HIDDEN ** -0.5),
    }


def reference(x, c, ctx, c_ctx, mod_w, mod_b, norm_g, w_in, a_mu, a_w0, a_w_up, a_a0, a_a_up,
              a_g_up, a_k_k, a_k_a, a_r_k, a_ln, a_w_out, b_ln, b_ws, b_bs, b_w_out, c_dw, c_dw_b,
              c_ln, c_w_out, d_w_out, gate_w, gate_b, w_o, ffn_w1, ffn_w3, ffn_w2):
    rows = x.shape[1] // GRID_W
    x_lat, x_ctx = x, ctx
    for l in range(DEPTH):
        last = l == DEPTH - 1
        lp = dict(w_in=w_in[l], a_mu=a_mu[l], a_w0=a_w0[l], a_w_up=a_w_up[l], a_a0=a_a0[l],
                  a_a_up=a_a_up[l], a_g_up=a_g_up[l], a_k_k=a_k_k[l], a_k_a=a_k_a[l],
                  a_r_k=a_r_k[l], a_ln=a_ln[l], a_w_out=a_w_out[l], b_ln=b_ln[l], b_ws=b_ws[l],
                  b_bs=b_bs[l], b_w_out=b_w_out[l], c_dw=c_dw[l], c_dw_b=c_dw_b[l], c_ln=c_ln[l],
                  c_w_out=c_w_out[l], d_w_out=d_w_out[l], gate_w=gate_w[l], gate_b=gate_b[l],
                  w_o=w_o[l])
        ng = norm_g[l]
        mod_lat = jnp.split((jax.nn.silu(c) @ mod_w[l] + mod_b[l])[:, None, :], 6, axis=-1)
        n_ctx_mod = 2 if last else 6
        mod_ctx = jnp.split(jax.nn.silu(c_ctx) @ mod_w[l][:, :n_ctx_mod * D_MODEL]
                            + mod_b[l][:n_ctx_mod * D_MODEL], n_ctx_mod, axis=-1)

        h_ctx = _modulate(_rmsnorm(x_ctx, ng[0]), mod_ctx[0], mod_ctx[1])
        if last:
            ctx_states = _context_scan_states(h_ctx, lp)
        else:
            mix_ctx, ctx_states = _token_mix(h_ctx, _zero_states(x_ctx.shape[0]), None, lp)

        h_lat = _modulate(_rmsnorm(x_lat, ng[0]), mod_lat[0], mod_lat[1])
        mix_lat, _ = _token_mix(h_lat, ctx_states, rows, lp)
        x_lat = x_lat + mod_lat[2] * _rmsnorm(mix_lat, ng[1])
        x_lat = _ffn_sublayer(x_lat, mod_lat[3], mod_lat[4], mod_lat[5], ng[2], ng[3],
                              ffn_w1[l], ffn_w3[l], ffn_w2[l])

        if not last:
            x_ctx = x_ctx + mod_ctx[2] * _rmsnorm(mix_ctx, ng[1])
            x_ctx = _ffn_sublayer(x_ctx, mod_ctx[3], mod_ctx[4], mod_ctx[5], ng[2], ng[3],
                                  ffn_w1[l], ffn_w3[l], ffn_w2[l])
    return x_lat
```

```python
import functools
import math

import numpy as np
import jax
import jax.numpy as jnp
from jax import lax
from jax.experimental import pallas as pl
from jax.experimental.pallas import tpu as pltpu

F32, BF16 = jnp.float32, jnp.bfloat16
HIGHEST = lax.Precision.HIGHEST

D_MODEL = 4096
DEPTH = 2
GRID_W = 64
BRANCH_W = 1024
HEAD = 64
HEADS = BRANCH_W // HEAD
PAIRS = HEADS // 2
LORA_W, LORA_A, LORA_G = 64, 64, 160
A_PROJ = 3 * BRANCH_W + LORA_W + LORA_A + LORA_G
A_GN_EPS = 64e-5
B_CHUNK = 128
B_GROUPS = 8
C_KERNEL = 31
D_GROUPS = 4
D_GROUP_CH = BRANCH_W // D_GROUPS
FFN_HIDDEN = 11008
RMS_EPS = 1e-6
LN_EPS = 1e-5

LANES = 128
VMEM_BUDGET = 60 * 1024 * 1024

A_WC, A_AC, A_GC = 3 * BRANCH_W, 3 * BRANCH_W + LANES, 3 * BRANCH_W + 2 * LANES
A_PAD = A_GC + 2 * LANES
P_PAD = A_PAD + 5 * BRANCH_W

CHUNK = 64
CONV_HALO = 16


def _cparams(n_axes, vmem_bytes):
    limit = int(min(max(vmem_bytes + (8 << 20), 32 << 20), VMEM_BUDGET))
    return pltpu.CompilerParams(dimension_semantics=("arbitrary",) * n_axes, vmem_limit_bytes=limit)


def _dot(a, b):
    return jnp.dot(a.astype(BF16), b.astype(BF16), preferred_element_type=F32)


def _dot_nt(a, b):
    return lax.dot_general(a.astype(BF16), b.astype(BF16), (((1,), (1,)), ((), ())), preferred_element_type=F32)


def _dot_tn(a, b):
    return lax.dot_general(a.astype(BF16), b.astype(BF16), (((0,), (0,)), ((), ())), preferred_element_type=F32)


def _split2(x):
    hi = x.astype(BF16)
    lo = (x - hi.astype(F32)).astype(BF16)
    return hi, lo


def _split3(x):
    hi = x.astype(BF16)
    r1 = x - hi.astype(F32)
    mid = r1.astype(BF16)
    lo = (r1 - mid.astype(F32)).astype(BF16)
    return hi, mid, lo


def _dot_exact_rhs(x, e):
    hi, mid, lo = _split3(x)
    f = lambda p: jnp.dot(p, e, preferred_element_type=F32)
    return f(hi) + f(mid) + f(lo)


def _dot_exact_lhs(e, x):
    hi, mid, lo = _split3(x)
    f = lambda p: jnp.dot(e, p, preferred_element_type=F32)
    return f(hi) + f(mid) + f(lo)


def _dot3(a, b):
    ah, al = _split2(a)
    bh, bl = _split2(b)
    f = lambda p, q: jnp.dot(p, q, preferred_element_type=F32)
    return f(ah, bh) + f(ah, bl) + f(al, bh)


def _sigmoid(x):
    return 1.0 / (1.0 + jnp.exp(-x))


def _silu(x):
    return x * _sigmoid(x)


def _softplus(x):
    return jnp.maximum(x, 0.0) + jnp.log(1.0 + jnp.exp(-jnp.abs(x)))


def _gelu_tanh(x):
    return 0.5 * x * (1.0 + jnp.tanh(math.sqrt(2.0 / math.pi) * (x + 0.044715 * (x * x * x))))


def _rmsnorm(x, g):
    return (x * lax.rsqrt(jnp.mean(x * x, axis=-1, keepdims=True) + RMS_EPS)) * g


def _layernorm(x, g, b):
    mu = jnp.mean(x, axis=-1, keepdims=True)
    xc = x - mu
    var = jnp.mean(xc * xc, axis=-1, keepdims=True)
    return (xc * lax.rsqrt(var + LN_EPS)) * g + b


def _mod_body(c_ref, w_ref, b_ref, o_ref):
    s = _silu(c_ref[...])
    o_ref[0] = jnp.dot(s, w_ref[0], preferred_element_type=F32, precision=HIGHEST) + b_ref[0]


def modulation(c_rows, mod_w, mod_b):
    depth, d, n = mod_w.shape
    tn = 1024
    return pl.pallas_call(
        _mod_body,
        grid=(depth, n // tn),
        in_specs=[pl.BlockSpec((8, d), lambda l, j: (0, 0)),
                  pl.BlockSpec((1, d, tn), lambda l, j: (l, 0, j)),
                  pl.BlockSpec((1, 1, tn), lambda l, j: (l, 0, j))],
        out_specs=pl.BlockSpec((1, 8, tn), lambda l, j: (l, 0, j)),
        out_shape=jax.ShapeDtypeStruct((depth, 8, n), F32),
        compiler_params=_cparams(2, 2 * d * tn * 4),
        name="modulation",
    )(c_rows, mod_w, mod_b)


def _normmod_body(x_ref, g_ref, sh_ref, sc_ref, h_ref):
    h = _rmsnorm(x_ref[...], g_ref[...]) * (1.0 + sc_ref[...]) + sh_ref[...]
    h_ref[...] = h.astype(h_ref.dtype)


def normmod(x, g, shift, scale):
    t, d = x.shape
    tm = 256
    row = pl.BlockSpec((1, d), lambda i: (0, 0))
    return pl.pallas_call(
        _normmod_body,
        grid=(t // tm,),
        in_specs=[pl.BlockSpec((tm, d), lambda i: (i, 0)), row, row, row],
        out_specs=pl.BlockSpec((tm, d), lambda i: (i, 0)),
        out_shape=jax.ShapeDtypeStruct((t, d), BF16),
        compiler_params=_cparams(1, 2 * tm * d * 6),
        name="normmod",
    )(x, g, shift, scale)


def _resnorm_body(x_ref, y_ref, gate_ref, gpost_ref, gpre_ref, sh_ref, sc_ref, xo_ref, h_ref):
    xn = x_ref[...] + gate_ref[...] * _rmsnorm(y_ref[...], gpost_ref[...])
    xo_ref[...] = xn
    h = _rmsnorm(xn, gpre_ref[...]) * (1.0 + sc_ref[...]) + sh_ref[...]
    h_ref[...] = h.astype(h_ref.dtype)


def _res_body(x_ref, y_ref, gate_ref, gpost_ref, xo_ref):
    xo_ref[...] = x_ref[...] + gate_ref[...] * _rmsnorm(y_ref[...], gpost_ref[...])


def resnorm(x, y, gate, g_post, nxt=None):
    t, d = x.shape
    tm = 256
    row = pl.BlockSpec((1, d), lambda i: (0, 0))
    tile = pl.BlockSpec((tm, d), lambda i: (i, 0))
    if nxt is None:
        return pl.pallas_call(
            _res_body, grid=(t // tm,), in_specs=[tile, tile, row, row], out_specs=tile,
            out_shape=jax.ShapeDtypeStruct((t, d), F32),
            compiler_params=_cparams(1, 2 * tm * d * 12), name="residual",
        )(x, y, gate, g_post)
    return pl.pallas_call(
        _resnorm_body, grid=(t // tm,), in_specs=[tile, tile, row, row, row, row, row],
        out_specs=[tile, tile],
        out_shape=[jax.ShapeDtypeStruct((t, d), F32), jax.ShapeDtypeStruct((t, d), BF16)],
        compiler_params=_cparams(1, 2 * tm * d * 14), name="residual_norm",
    )(x, y, gate, g_post, *nxt)


def _mm_body(x_ref, w_ref, o_ref):
    o_ref[...] = jnp.dot(x_ref[...], w_ref[...], preferred_element_type=F32).astype(o_ref.dtype)


def matmul(x, w, n, col0, tn, out_dtype, name):
    m, k = x.shape
    tm = min(m, 512 if k <= 4096 else 256)
    j0 = col0 // tn
    assert col0 % tn == 0 and n % tn == 0 and m % tm == 0
    vm = 2 * (tm * k * 2 + k * tn * 2 + tm * tn * 4)
    return pl.pallas_call(
        _mm_body,
        grid=(n // tn, m // tm),
        in_specs=[pl.BlockSpec((tm, k), lambda j, i: (i, 0)),
                  pl.BlockSpec((k, tn), lambda j, i: (0, j + j0))],
        out_specs=pl.BlockSpec((tm, tn), lambda j, i: (i, j)),
        out_shape=jax.ShapeDtypeStruct((m, n), out_dtype),
        compiler_params=_cparams(2, vm),
        name=name,
    )(x, w)


def _ffn_up_body(x_ref, w1_ref, w3_ref, o_ref):
    x = x_ref[...]
    a = jnp.dot(x, w1_ref[...], preferred_element_type=F32)
    b = jnp.dot(x, w3_ref[...], preferred_element_type=F32)
    o_ref[...] = (_silu(a) * b).astype(o_ref.dtype)


def ffn_up(h, w1, w3):
    m, k = h.shape
    n = w1.shape[1]
    tm, tn = min(m, 512), 256
    vm = 2 * (tm * k * 2 + 2 * k * tn * 2 + tm * tn * 2) + 3 * tm * tn * 4
    return pl.pallas_call(
        _ffn_up_body,
        grid=(n // tn, m // tm),
        in_specs=[pl.BlockSpec((tm, k), lambda j, i: (i, 0)),
                  pl.BlockSpec((k, tn), lambda j, i: (0, j)),
                  pl.BlockSpec((k, tn), lambda j, i: (0, j))],
        out_specs=pl.BlockSpec((tm, tn), lambda j, i: (i, j)),
        out_shape=jax.ShapeDtypeStruct((m, n), BF16),
        compiler_params=_cparams(2, vm),
        name="ffn_up",
    )(h, w1, w3)


def _merge_body(h_ref, za_ref, zb_ref, zc_ref, zd_ref,
                ga_ref, gb_ref, gc_ref, gd_ref, ba_ref, bb_ref, bc_ref, bd_ref,
                wa_ref, wb_ref, wc_ref, wd_ref, o_ref):
    h = h_ref[...]
    acc = None
    for z_ref, g_ref, b_ref, w_ref in ((za_ref, ga_ref, ba_ref, wa_ref), (zb_ref, gb_ref, bb_ref, wb_ref),
                                       (zc_ref, gc_ref, bc_ref, wc_ref), (zd_ref, gd_ref, bd_ref, wd_ref)):
        gate = _sigmoid(jnp.dot(h, g_ref[...], preferred_element_type=F32) + b_ref[...])
        y = jnp.dot(z_ref[...], w_ref[...], preferred_element_type=F32)
        acc = gate * y if acc is None else acc + gate * y
    o_ref[...] = acc.astype(o_ref.dtype)


def merge(h, zs, gate_w, gate_b, w_outs):
    m, d = h.shape
    bw = zs[0].shape[1]
    tm, tn = min(m, 512), 256
    nj = d // tn
    hspec = pl.BlockSpec((tm, d), lambda j, i: (i, 0))
    zspec = pl.BlockSpec((tm, bw), lambda j, i: (i, 0))
    gspecs = [pl.BlockSpec((d, tn), functools.partial(lambda j, i, br: (0, br * nj + j), br=br)) for br in range(4)]
    bspecs = [pl.BlockSpec((1, tn), functools.partial(lambda j, i, br: (0, br * nj + j), br=br)) for br in range(4)]
    wspec = pl.BlockSpec((bw, tn), lambda j, i: (0, j))
    vm = 2 * (tm * d * 2 + 4 * tm * bw * 2 + 4 * d * tn * 2 + 4 * bw * tn * 2 + tm * tn * 2) + 4 * tm * tn * 4
    return pl.pallas_call(
        _merge_body,
        grid=(nj, m // tm),
        in_specs=[hspec] + [zspec] * 4 + gspecs + bspecs + [wspec] * 4,
        out_specs=pl.BlockSpec((tm, tn), lambda j, i: (i, j)),
        out_shape=jax.ShapeDtypeStruct((m, d), BF16),
        compiler_params=_cparams(2, vm),
        name="merge",
    )(h, *zs, gate_w, gate_w, gate_w, gate_w, gate_b, gate_b, gate_b, gate_b, *w_outs)


def _token_shift(x, mu, row_len):
    tm = x.shape[0]
    pos = lax.broadcasted_iota(jnp.int32, x.shape, 0) & (row_len - 1)
    prev = jnp.where(pos == 0, 0.0, pltpu.roll(x, 1, 0))
    nxt = jnp.where(pos == row_len - 1, 0.0, pltpu.roll(x, tm - 1, 0))
    return x + mu * (0.5 * (prev + nxt) - x)


def _aprep_body(r_ref, k_ref, v_ref, wc_ref, ac_ref, gc_ref,
                mur_ref, muk_ref, muv_ref, muw_ref, mua_ref, mug_ref,
                w0_ref, wup_ref, a0_ref, aup_ref, gup_ref, kk_ref, ka_ref, rk_ref, e_ref,
                ro_ref, vo_ref, kko_ref, lw_ref, kd_ref, b_ref, bonus_ref, g_ref, *, row_len):
    e = e_ref[...]
    xr = _token_shift(r_ref[...], mur_ref[...], row_len)
    xk = _token_shift(k_ref[...], muk_ref[...], row_len)
    xv = _token_shift(v_ref[...], muv_ref[...], row_len)
    xw = _token_shift(wc_ref[...], muw_ref[...], row_len)
    xa = _token_shift(ac_ref[...], mua_ref[...], row_len)
    xg = _token_shift(gc_ref[...], mug_ref[...], row_len)
    kk = xk * kk_ref[...]
    kk = kk * lax.rsqrt(jnp.maximum(_dot_exact_rhs(kk * kk, e), 1e-12))
    tw = jnp.tanh(xw)
    bonus = None
    for d in range(2):
        w = -_softplus(-(w0_ref[d] + jnp.dot(tw, wup_ref[d], preferred_element_type=F32, precision=HIGHEST))) - 0.5
        a = _sigmoid(a0_ref[d] + jnp.dot(xa, aup_ref[d], preferred_element_type=F32, precision=HIGHEST))
        kd = xk * (1.0 + (a - 1.0) * ka_ref[...])
        lw_ref[d] = -jnp.exp(w)
        kd_ref[d] = kd
        b_ref[d] = kk * a
        bn = _dot_exact_rhs(xr * kd * rk_ref[...], e) * xv
        bonus = bn if bonus is None else bonus + bn
    ro_ref[...] = xr
    vo_ref[...] = xv
    kko_ref[...] = kk
    bonus_ref[...] = bonus
    g_ref[...] = jnp.dot(_sigmoid(xg), gup_ref[...], preferred_element_type=F32, precision=HIGHEST)


def a_prep(pa, ap, row_len):
    t = pa.shape[0]
    tm = 256
    nb = BRANCH_W // LANES
    col = lambda c: pl.BlockSpec((tm, LANES), functools.partial(lambda i, p, c: (i, c + p), c=c))
    fix = lambda c, w=LANES: pl.BlockSpec((tm, w), functools.partial(lambda i, p, c: (i, c), c=c))
    mcol = lambda c: pl.BlockSpec((1, LANES), functools.partial(lambda i, p, c: (0, c + p), c=c))
    mfix = lambda c, w=LANES: pl.BlockSpec((1, w), functools.partial(lambda i, p, c: (0, c), c=c))
    prow = pl.BlockSpec((1, LANES), lambda i, p: (0, p))
    p2 = pl.BlockSpec((2, 1, LANES), lambda i, p: (0, 0, p))
    up2 = pl.BlockSpec((2, LANES, LANES), lambda i, p: (0, 0, p))
    out1 = pl.BlockSpec((tm, LANES), lambda i, p: (i, p))
    out2 = pl.BlockSpec((2, tm, LANES), lambda i, p: (0, i, p))
    s1 = jax.ShapeDtypeStruct((t, BRANCH_W), F32)
    s2 = jax.ShapeDtypeStruct((2, t, BRANCH_W), F32)
    return pl.pallas_call(
        functools.partial(_aprep_body, row_len=row_len),
        grid=(t // tm, nb),
        in_specs=[col(0), col(nb), col(2 * nb), fix(A_WC // LANES), fix(A_AC // LANES), fix(A_GC // (2 * LANES), 2 * LANES),
                  mcol(0), mcol(nb), mcol(2 * nb), mfix(A_WC // LANES), mfix(A_AC // LANES), mfix(A_GC // (2 * LANES), 2 * LANES),
                  p2, up2, p2, up2, pl.BlockSpec((2 * LANES, LANES), lambda i, p: (0, p)), prow, prow, prow,
                  pl.BlockSpec((LANES, LANES), lambda i, p: (0, 0))],
        out_specs=[out1, out1, out1, out2, out2, out2, out1, out1],
        out_shape=[s1, s1, s1, s2, s2, s2, s1, s1],
        compiler_params=_cparams(2, 40 * tm * LANES * 4),
        name="rwkv_prep",
    )(pa, pa, pa, pa, pa, pa, ap["mu"], ap["mu"], ap["mu"], ap["mu"], ap["mu"], ap["mu"],
      ap["w0"], ap["w_up"], ap["a0"], ap["a_up"], ap["g_up"], ap["k_k"], ap["k_a"], ap["r_k"], ap["e2"])


def _pair_rows(x):
    lane = lax.broadcasted_iota(jnp.int32, x.shape, 1)
    return jnp.concatenate([jnp.where(lane < HEAD, x, 0.0), jnp.where(lane >= HEAD, x, 0.0)], axis=0)


MSK_STRICT, MSK_INCL, MSK_EYE, MSK_DIAG8, MSK_OFF8, MSK_OFF16, MSK_OFF32 = range(7)


def _chunk_mask_tables():
    n = 2 * CHUNK
    r = np.arange(n)[:, None]
    c = np.arange(n)[None, :]
    same = (r // CHUNK) == (c // CHUNK)
    out = np.zeros((2, 7, n, n), np.float32)
    for d in range(2):
        before = (c > r) if d else (c < r)
        out[d, MSK_STRICT] = same & before
        out[d, MSK_INCL] = same & (before | (r == c))
        out[d, MSK_EYE] = r == c
        out[d, MSK_DIAG8] = ((r // 8) == (c // 8)) & before
        for idx, s in ((MSK_OFF8, 8), (MSK_OFF16, 16), (MSK_OFF32, 32)):
            blk = (r // (2 * s)) == (c // (2 * s))
            rh, ch = (r // s) % 2, (c // s) % 2
            out[d, idx] = blk & ((rh == 0) & (ch == 1) if d else (rh == 1) & (ch == 0))
    return jnp.asarray(out)


def _unit_tri_inverse(nm, msk):
    n8 = nm * msk[MSK_DIAG8]
    t = msk[MSK_EYE] + n8
    n2 = _dot3(n8, n8)
    t = t + _dot3(n2, t)
    n4 = _dot3(n2, n2)
    t = t + _dot3(n4, t)
    for idx in (MSK_OFF8, MSK_OFF16, MSK_OFF32):
        t = t + _dot3(_dot3(t, nm * msk[idx]), t)
    return t


def _chunk_maps(r, lw, kd, v, kk, b, tri, msk, rev):
    c = r.shape[0]
    cum = _dot_exact_lhs(tri, lw)
    tot = cum[0:1] if rev else cum[c - 1:c]
    g_inv = jnp.exp(-cum)
    g_tail = jnp.exp(tot - cum)
    atp = _pair_rows(-kk * jnp.exp(cum - lw))
    rtp = _pair_rows(r * jnp.exp(cum))
    btp = _pair_rows(b * g_inv)
    ktp = _pair_rows(kd * g_inv)
    vp = _pair_rows(v)
    n = 2 * c
    big = _dot_nt(jnp.concatenate([atp, rtp], axis=0), jnp.concatenate([btp, ktp], axis=0))
    strict = msk[MSK_STRICT] > 0.0
    incl = msk[MSK_INCL] > 0.0
    a_ab = jnp.where(strict, big[:n, :n], 0.0)
    a_ak = jnp.where(strict, big[:n, n:], 0.0)
    a_rb = jnp.where(incl, big[n:, :n], 0.0)
    a_rk = jnp.where(incl, big[n:, n:], 0.0)
    tinv = _unit_tri_inverse(a_ab, msk)
    w_u = _dot(tinv, jnp.concatenate([atp, _dot(a_ak, vp)], axis=1))
    ry = _dot(a_rb, w_u)
    rhat_p = rtp + ry[:, :n]
    y0_p = ry[:, n:] + _dot(a_rk, vp)
    mn = _dot_tn(_pair_rows(b * g_tail), w_u)
    m = mn[:, :n] + msk[MSK_EYE] * jnp.exp(tot)
    nn = mn[:, n:] + _dot_tn(_pair_rows(kd * g_tail), vp)
    return rhat_p[:c] + rhat_p[c:], y0_p[:c] + y0_p[c:], m, nn


def _scan_pre_body(r_ref, v_ref, kk_ref, lw_ref, kd_ref, b_ref, tri_ref, msk_ref, rhat_ref, y0_ref, m_ref, n_ref, *, nchunk):
    for d in range(2):
        tri = tri_ref[d]
        msk = msk_ref.at[d]
        for ci in range(nchunk):
            rows = slice(ci * CHUNK, (ci + 1) * CHUNK)
            rhat, y0, m, nn = _chunk_maps(r_ref[rows, :], lw_ref[d, rows, :], kd_ref[d, rows, :], v_ref[rows, :],
                                          kk_ref[rows, :], b_ref[d, rows, :], tri, msk, rev=(d == 1))
            rhat_ref[d, rows, :] = rhat
            y0_ref[d, rows, :] = y0
            m_ref[d, ci, 0] = m
            n_ref[d, ci, 0] = nn


def scan_pre(r, v, kk, lw, kd, b, tri, msk):
    t = r.shape[0]
    nchunk = 2
    tm = nchunk * CHUNK
    nc = t // CHUNK
    in1 = pl.BlockSpec((tm, LANES), lambda i, p: (i, p))
    in2 = pl.BlockSpec((2, tm, LANES), lambda i, p: (0, i, p))
    mspec = pl.BlockSpec((2, nchunk, 1, LANES, LANES), lambda i, p: (0, i, p, 0, 0))
    s2 = jax.ShapeDtypeStruct((2, t, BRANCH_W), F32)
    sm = jax.ShapeDtypeStruct((2, nc, PAIRS, LANES, LANES), F32)
    return pl.pallas_call(
        functools.partial(_scan_pre_body, nchunk=nchunk),
        grid=(t // tm, PAIRS),
        in_specs=[in1, in1, in1, in2, in2, in2, pl.BlockSpec((2, CHUNK, CHUNK), lambda i, p: (0, 0, 0)),
                  pl.BlockSpec(msk.shape, lambda i, p: (0, 0, 0, 0))],
        out_specs=[in2, in2, mspec, mspec],
        out_shape=[s2, s2, sm, sm],
        compiler_params=_cparams(2, 16 << 20),
        name="rwkv_chunk_maps",
    )(r, v, kk, lw, kd, b, tri, msk)


def _scan_seq_body(s0_ref, mf_ref, mb_ref, nf_ref, nb_ref, rf_ref, rb_ref, yf0_ref, yb0_ref,
                   yf_ref, yb_ref, sfin_ref, s_ref):
    c = pl.program_id(0)

    @pl.when(c == 0)
    def _():
        s_ref[...] = s0_ref[...]

    for d, (m_ref, n_ref, rh_ref, y0_ref, y_ref) in enumerate(((mf_ref, nf_ref, rf_ref, yf0_ref, yf_ref),
                                                                 (mb_ref, nb_ref, rb_ref, yb0_ref, yb_ref))):
        for p in range(PAIRS):
            ln = slice(p * LANES, (p + 1) * LANES)
            s = s_ref[d, p]
            y_ref[:, ln] = _dot(rh_ref[0, :, ln], s) + y0_ref[0, :, ln]
            s_ref[d, p] = _dot3(m_ref[0, 0, p], s) + n_ref[0, 0, p]

    @pl.when(c == pl.num_programs(0) - 1)
    def _():
        sfin_ref[...] = s_ref[...]


def scan_seq(s0, rhat, y0, m, n):
    t = rhat.shape[1]
    nc = t // CHUNK
    fwd5 = lambda c: (0, c, 0, 0, 0)
    bwd5 = lambda c: (1, nc - 1 - c, 0, 0, 0)
    mblk = (1, 1, PAIRS, LANES, LANES)
    rblk = (1, CHUNK, BRANCH_W)
    sblk = pl.BlockSpec((2, PAIRS, LANES, LANES), lambda c: (0, 0, 0, 0))
    ys = jax.ShapeDtypeStruct((t, BRANCH_W), F32)
    return pl.pallas_call(
        _scan_seq_body,
        grid=(nc,),
        in_specs=[sblk,
                  pl.BlockSpec(mblk, fwd5), pl.BlockSpec(mblk, bwd5),
                  pl.BlockSpec(mblk, fwd5), pl.BlockSpec(mblk, bwd5),
                  pl.BlockSpec(rblk, lambda c: (0, c, 0)), pl.BlockSpec(rblk, lambda c: (1, nc - 1 - c, 0)),
                  pl.BlockSpec(rblk, lambda c: (0, c, 0)), pl.BlockSpec(rblk, lambda c: (1, nc - 1 - c, 0))],
        out_specs=[pl.BlockSpec((CHUNK, BRANCH_W), lambda c: (c, 0)),
                   pl.BlockSpec((CHUNK, BRANCH_W), lambda c: (nc - 1 - c, 0)),
                   sblk],
        out_shape=[ys, ys, jax.ShapeDtypeStruct((2, PAIRS, LANES, LANES), F32)],
        scratch_shapes=[pltpu.VMEM((2, PAIRS, LANES, LANES), F32)],
        compiler_params=_cparams(1, 16 << 20),
        name="rwkv_chain",
    )(s0, m, m, n, n, rhat, rhat, y0, y0)


def _areadout_body(yf_ref, yb_ref, bonus_ref, g_ref, lng_ref, lnb_ref, e_ref, o_ref):
    e = e_ref[...]
    y = yf_ref[...] + yb_ref[...]
    mu = _dot_exact_rhs(y, e) * (1.0 / HEAD)
    yc = y - mu
    var = _dot_exact_rhs(yc * yc, e) * (1.0 / HEAD)
    yn = (yc * lax.rsqrt(var + A_GN_EPS)) * lng_ref[...] + lnb_ref[...]
    o_ref[...] = ((yn + bonus_ref[...]) * g_ref[...]).astype(o_ref.dtype)


def a_readout(yf, yb, bonus, g, ap):
    t = yf.shape[0]
    tm = 256
    blk = pl.BlockSpec((tm, LANES), lambda i, p: (i, p))
    prow = pl.BlockSpec((1, LANES), lambda i, p: (0, p))
    return pl.pallas_call(
        _areadout_body,
        grid=(t // tm, BRANCH_W // LANES),
        in_specs=[blk, blk, blk, blk, prow, prow, pl.BlockSpec((LANES, LANES), lambda i, p: (0, 0))],
        out_specs=blk,
        out_shape=jax.ShapeDtypeStruct((t, BRANCH_W), BF16),
        compiler_params=_cparams(2, 16 * tm * LANES * 4),
        name="rwkv_readout",
    )(yf, yb, bonus, g, ap["ln_g"], ap["ln_b"], ap["e2"])


def _bmix_body(u_ref, v_ref, lng_ref, lnb_ref, ws_ref, bs_ref, o_ref):
    u = _gelu_tanh(u_ref[...])
    v = _layernorm(_gelu_tanh(v_ref[...]), lng_ref[...], lnb_ref[...]).astype(BF16)
    for ci in range(u.shape[0] // B_CHUNK):
        rows = slice(ci * B_CHUNK, (ci + 1) * B_CHUNK)
        for g in range(B_GROUPS):
            ln = slice(g * LANES, (g + 1) * LANES)
            s = jnp.dot(ws_ref[g], v[rows, ln], preferred_element_type=F32) + bs_ref[:, ln]
            o_ref[rows, ln] = (u[rows, ln] * s).astype(o_ref.dtype)


def b_mix(pb, bp):
    t = pb.shape[0]
    tm = 256
    row = pl.BlockSpec((1, BRANCH_W), lambda i: (0, 0))
    return pl.pallas_call(
        _bmix_body,
        grid=(t // tm,),
        in_specs=[pl.BlockSpec((tm, BRANCH_W), lambda i: (i, 0)), pl.BlockSpec((tm, BRANCH_W), lambda i: (i, 1)),
                  row, row, pl.BlockSpec((B_GROUPS, B_CHUNK, B_CHUNK), lambda i: (0, 0, 0)),
                  pl.BlockSpec((B_CHUNK, BRANCH_W), lambda i: (0, 0))],
        out_specs=pl.BlockSpec((tm, BRANCH_W), lambda i: (i, 0)),
        out_shape=jax.ShapeDtypeStruct((t, BRANCH_W), BF16),
        compiler_params=_cparams(1, 12 * tm * BRANCH_W * 4),
        name="gmlp_mix",
    )(pb, pb, bp["ln_g"], bp["ln_b"], bp["ws"], bp["bs"])


def _conv_body(ac_ref, gc_ref, ap_ref, gp_ref, an_ref, gn_ref, dw_ref, dwb_ref, lng_ref, lnb_ref,
               o_ref, zs_ref, cs_ref, *, tm):
    i = pl.program_id(0)
    last = pl.num_programs(0) - 1
    glu = lambda a, g: a * _sigmoid(g)
    zs_ref[0:CONV_HALO, :] = jnp.where(i == 0, 0.0, glu(ap_ref[...], gp_ref[...]))
    zs_ref[CONV_HALO:CONV_HALO + tm, :] = glu(ac_ref[...], gc_ref[...])
    zs_ref[CONV_HALO + tm:, :] = jnp.where(i == last, 0.0, glu(an_ref[...], gn_ref[...]))
    rb, lb = 32, 256
    half = C_KERNEL // 2
    for r0 in range(0, tm, rb):
        for l0 in range(0, BRANCH_W, lb):
            acc = jnp.zeros((rb, lb), F32) + dwb_ref[:, l0:l0 + lb]
            for j in range(C_KERNEL):
                off = CONV_HALO + r0 + j - half
                acc = acc + zs_ref[off:off + rb, l0:l0 + lb] * dw_ref[j:j + 1, l0:l0 + lb]
            cs_ref[r0:r0 + rb, l0:l0 + lb] = acc
    o_ref[...] = _silu(_layernorm(cs_ref[...], lng_ref[...], lnb_ref[...])).astype(o_ref.dtype)


def conv_mix(pc, cp):
    t = pc.shape[0]
    tm = 256
    hb = tm // CONV_HALO
    nh = t // CONV_HALO
    cur = lambda c: pl.BlockSpec((tm, BRANCH_W), functools.partial(lambda i, c: (i, c), c=c))
    prev = lambda c: pl.BlockSpec((CONV_HALO, BRANCH_W), functools.partial(lambda i, c: (jnp.maximum(i * hb - 1, 0), c), c=c))
    nxt = lambda c: pl.BlockSpec((CONV_HALO, BRANCH_W), functools.partial(lambda i, c: (jnp.minimum((i + 1) * hb, nh - 1), c), c=c))
    row = pl.BlockSpec((1, BRANCH_W), lambda i: (0, 0))
    return pl.pallas_call(
        functools.partial(_conv_body, tm=tm),
        grid=(t // tm,),
        in_specs=[cur(0), cur(1), prev(0), prev(1), nxt(0), nxt(1),
                  pl.BlockSpec((C_KERNEL + 1, BRANCH_W), lambda i: (0, 0)), row, row, row],
        out_specs=pl.BlockSpec((tm, BRANCH_W), lambda i: (i, 0)),
        out_shape=jax.ShapeDtypeStruct((t, BRANCH_W), BF16),
        scratch_shapes=[pltpu.VMEM((tm + 2 * CONV_HALO, BRANCH_W), F32), pltpu.VMEM((tm, BRANCH_W), F32)],
        compiler_params=_cparams(1, 16 * tm * BRANCH_W * 4),
        name="conv_mix",
    )(pc, pc, pc, pc, pc, pc, cp["dw"], cp["dw_b"], cp["ln_g"], cp["ln_b"])


FFT_N1, FFT_N2 = 64, 128


def _dft_tables(t):
    two_pi = 2.0 * np.pi
    cidx = np.arange(D_GROUP_CH)
    ph = two_pi * np.outer(cidx, cidx) / D_GROUP_CH
    chan = np.concatenate([np.cos(ph), np.sin(ph)], axis=0)
    if t <= 256:
        n = np.arange(t)
        th = two_pi * np.outer(n, n) / t
        m2 = np.concatenate([np.cos(th), -np.sin(th)], axis=0)
        return None, m2.astype(np.float32), chan.astype(np.float32)
    n1, n2 = FFT_N1, FFT_N2
    assert t == n1 * n2
    k1 = np.arange(n1)
    tok = (n2 * np.arange(n1))[None, None, :] + np.arange(n2)[:, None, None]
    th = two_pi * (k1[None, :, None] * tok) / t
    g1 = np.concatenate([np.cos(th), -np.sin(th)], axis=1)
    q = np.arange(n2)
    th2 = two_pi * np.outer(q, q) / n2
    c2, s2 = np.cos(th2), np.sin(th2)
    m2 = np.block([[c2, s2], [-s2, c2]])
    return g1.astype(np.float32), m2.astype(np.float32), chan.astype(np.float32)


def _fft1_body(x_ref, g_ref, o_ref):
    o_ref[...] = _dot(g_ref[0], x_ref[...])


def _fft2_body(z_ref, m2_ref, ch_ref, o_ref, *, n_out, scale, stacked):
    z = jnp.concatenate([z_ref[0], z_ref[1]], axis=0) if stacked else z_ref[...]
    x = _dot(m2_ref[...], z)
    xr, xi = x[:n_out], x[n_out:]
    for g in range(D_GROUPS):
        ln = slice(g * D_GROUP_CH, (g + 1) * D_GROUP_CH)
        f = _dot(xr[:, ln], ch_ref[0:D_GROUP_CH]) + _dot(xi[:, ln], ch_ref[D_GROUP_CH:])
        o_ref[:, ln] = (f * scale).astype(o_ref.dtype)


def fourier_mix(pd):
    t = pd.shape[0]
    g1, m2, chan = (None if a is None else jnp.asarray(a) for a in _dft_tables(t))
    scale = 1.0 / math.sqrt(t * D_GROUP_CH)
    chspec = pl.BlockSpec((2 * D_GROUP_CH, D_GROUP_CH), lambda i: (0, 0))
    if g1 is None:
        return pl.pallas_call(
            functools.partial(_fft2_body, n_out=t, scale=scale, stacked=False),
            grid=(1,),
            in_specs=[pl.BlockSpec((t, BRANCH_W), lambda i: (0, 0)), pl.BlockSpec((2 * t, t), lambda i: (0, 0)), chspec],
            out_specs=pl.BlockSpec((t, BRANCH_W), lambda i: (0, 0)),
            out_shape=jax.ShapeDtypeStruct((t, BRANCH_W), BF16),
            compiler_params=_cparams(1, 16 << 20),
            name="fourier_small",
        )(pd, m2, chan)
    n1, n2 = FFT_N1, FFT_N2
    z = pl.pallas_call(
        _fft1_body,
        grid=(n2,),
        in_specs=[pl.BlockSpec((n1, BRANCH_W), lambda q: (0, q)), pl.BlockSpec((1, 2 * n1, n1), lambda q: (q, 0, 0))],
        out_specs=pl.BlockSpec((2 * n1, BRANCH_W), lambda q: (0, q)),
        out_shape=jax.ShapeDtypeStruct((2 * n1, n2 * BRANCH_W), F32),
        compiler_params=_cparams(1, 8 << 20),
        name="fourier_stage1",
    )(pd.reshape(n1, n2 * BRANCH_W), g1)
    f = pl.pallas_call(
        functools.partial(_fft2_body, n_out=n2, scale=scale, stacked=True),
        grid=(n1,),
        in_specs=[pl.BlockSpec((2, None, n2, BRANCH_W), lambda k: (0, k, 0, 0)),
                  pl.BlockSpec((2 * n2, 2 * n2), lambda k: (0, 0)), chspec],
        out_specs=pl.BlockSpec((n2, BRANCH_W), lambda k: (0, k)),
        out_shape=jax.ShapeDtypeStruct((n2, n1 * BRANCH_W), BF16),
        compiler_params=_cparams(1, 16 << 20),
        name="fourier_stage2",
    )(z.reshape(2, n1, n2, BRANCH_W), m2, chan)
    return f.reshape(t, BRANCH_W)


def _pad_cols(x, width):
    return jnp.pad(x, [(0, 0)] * (x.ndim - 1) + [(0, width - x.shape[-1])])


def _align_a(x):
    w3 = 3 * BRANCH_W
    return jnp.concatenate([x[..., :w3],
                            _pad_cols(x[..., w3:w3 + LORA_W], LANES),
                            _pad_cols(x[..., w3 + LORA_W:w3 + LORA_W + LORA_A], LANES),
                            _pad_cols(x[..., w3 + LORA_W + LORA_A:A_PROJ], 2 * LANES)], axis=-1)


def _pad_rows(x, rows):
    return jnp.pad(x, [(0, 0)] * (x.ndim - 2) + [(0, rows - x.shape[-2]), (0, 0)])


def _layer_params(l, w_in, a_mu, a_w0, a_w_up, a_a0, a_a_up, a_g_up, a_k_k, a_k_a, a_r_k, a_ln, a_w_out,
                  b_ln, b_ws, b_bs, b_w_out, c_dw, c_dw_b, c_ln, c_w_out, d_w_out, gate_w, gate_b, w_o,
                  ffn_w1, ffn_w3, ffn_w2, a_only):
    row = lambda v: v.reshape(1, -1)
    hid = np.arange(LANES) // HEAD
    e2 = jnp.asarray((hid[:, None] == hid[None, :]).astype(np.float32)).astype(BF16)
    win = w_in[l]
    if a_only:
        w_in_al = _align_a(win[:, :A_PROJ]).astype(BF16)
    else:
        w_in_al = jnp.concatenate([_align_a(win[:, :A_PROJ]), win[:, A_PROJ:]], axis=-1).astype(BF16)
    ap = dict(mu=_align_a(row(a_mu[l])), w0=a_w0[l][:, None, :], w_up=_pad_rows(a_w_up[l], LANES),
              a0=a_a0[l][:, None, :], a_up=_pad_rows(a_a_up[l], LANES), g_up=_pad_rows(a_g_up[l], 2 * LANES),
              k_k=row(a_k_k[l]), k_a=row(a_k_a[l]), r_k=row(a_r_k[l]), ln_g=row(a_ln[l][0]), ln_b=row(a_ln[l][1]), e2=e2)
    lp = dict(w_in=w_in_al, ap=ap)
    if a_only:
        return lp
    bs_exp = jnp.repeat(jnp.swapaxes(b_bs[l], 0, 1), LANES, axis=1)
    lp.update(
        bp=dict(ln_g=row(b_ln[l][0]), ln_b=row(b_ln[l][1]), ws=b_ws[l].astype(BF16), bs=bs_exp),
        cp=dict(dw=_pad_rows(c_dw[l], C_KERNEL + 1), dw_b=row(c_dw_b[l]), ln_g=row(c_ln[l][0]), ln_b=row(c_ln[l][1])),
        w_outs=[w[l].astype(BF16) for w in (a_w_out, b_w_out, c_w_out, d_w_out)],
        gate_w=gate_w[l].astype(BF16), gate_b=row(gate_b[l]), w_o=w_o[l].astype(BF16),
        w1=ffn_w1[l].astype(BF16), w3=ffn_w3[l].astype(BF16), w2=ffn_w2[l].astype(BF16))
    return lp


def _tri_tables():
    i = np.arange(CHUNK)
    lower = (i[None, :] <= i[:, None]).astype(np.float32)
    return jnp.asarray(np.stack([lower, lower.T])).astype(BF16)


def _rwkv_scan(pa, s0, ap, row_len):
    r, v, kk, lw, kd, b, bonus, g = a_prep(pa, ap, row_len)
    rhat, y0, m, n = scan_pre(r, v, kk, lw, kd, b, _tri_tables(), _chunk_mask_tables())
    yf, yb, s_fin = scan_seq(s0, rhat, y0, m, n)
    return yf, yb, bonus, g, s_fin


def _token_mix(h, s0, row_len, lp):
    tn = 512
    pa = matmul(h, lp["w_in"], A_PAD, 0, tn, F32, "in_proj_a")
    yf, yb, bonus, g, s_fin = _rwkv_scan(pa, s0, lp["ap"], row_len)
    za = a_readout(yf, yb, bonus, g, lp["ap"])
    pb = matmul(h, lp["w_in"], 2 * BRANCH_W, A_PAD, tn, F32, "in_proj_b")
    zb = b_mix(pb, lp["bp"])
    pc = matmul(h, lp["w_in"], 2 * BRANCH_W, A_PAD + 2 * BRANCH_W, tn, F32, "in_proj_c")
    zc = conv_mix(pc, lp["cp"])
    pd = matmul(h, lp["w_in"], BRANCH_W, A_PAD + 4 * BRANCH_W, tn, F32, "in_proj_d")
    zd = fourier_mix(pd)
    merged = merge(h, [za, zb, zc, zd], lp["gate_w"], lp["gate_b"], lp["w_outs"])
    mix = matmul(merged, lp["w_o"], D_MODEL, 0, tn, F32, "out_proj")
    return mix, s_fin


def _ffn(h2, lp):
    u = ffn_up(h2, lp["w1"], lp["w3"])
    return matmul(u, lp["w2"], D_MODEL, 0, 512, F32, "ffn_down")


def kernel(x, c, ctx, c_ctx, mod_w, mod_b, norm_g, w_in, a_mu, a_w0, a_w_up, a_a0, a_a_up, a_g_up, a_k_k, a_k_a,
           a_r_k, a_ln, a_w_out, b_ln, b_ws, b_bs, b_w_out, c_dw, c_dw_b, c_ln, c_w_out, d_w_out, gate_w, gate_b,
           w_o, ffn_w1, ffn_w3, ffn_w2):
    depth = mod_w.shape[0]
    d = D_MODEL
    c_rows = jnp.concatenate([c.reshape(1, d), c_ctx.reshape(1, d), jnp.zeros((6, d), F32)], axis=0)
    mods = modulation(c_rows, mod_w, mod_b.reshape(depth, 1, 6 * d))
    x_lat, x_ctx = x[0], ctx[0]
    zero_state = jnp.zeros((2, PAIRS, LANES, LANES), F32)
    weights = (w_in, a_mu, a_w0, a_w_up, a_a0, a_a_up, a_g_up, a_k_k, a_k_a, a_r_k, a_ln, a_w_out, b_ln, b_ws, b_bs,
               b_w_out, c_dw, c_dw_b, c_ln, c_w_out, d_w_out, gate_w, gate_b, w_o, ffn_w1, ffn_w3, ffn_w2)
    h_lat = h_ctx = None
    for l in range(depth):
        last = l == depth - 1
        lp = _layer_params(l, *weights, a_only=False)
        ng = [norm_g[l, i].reshape(1, d) for i in range(4)]
        ml = [mods[l, 0:1, i * d:(i + 1) * d] for i in range(6)]
        mc = [mods[l, 1:2, i * d:(i + 1) * d] for i in range(6)]
        if l == 0:
            h_lat = normmod(x_lat, ng[0], ml[0], ml[1])
            h_ctx = normmod(x_ctx, ng[0], mc[0], mc[1])

        if last:
            pa = matmul(h_ctx, lp["w_in"], A_PAD, 0, 512, F32, "in_proj_a")
            ctx_states = _rwkv_scan(pa, zero_state, lp["ap"], x_ctx.shape[0])[4]
        else:
            mix_ctx, ctx_states = _token_mix(h_ctx, zero_state, x_ctx.shape[0], lp)

        mix_lat, _ = _token_mix(h_lat, ctx_states, GRID_W, lp)
        x_lat, h2 = resnorm(x_lat, mix_lat, ml[2], ng[1], (ng[2], ml[3], ml[4]))
        y = _ffn(h2, lp)
        if last:
            x_lat = resnorm(x_lat, y, ml[5], ng[3])
        else:
            ngn = norm_g[l + 1, 0].reshape(1, d)
            mln = [mods[l + 1, 0:1, i * d:(i + 1) * d] for i in range(2)]
            mcn = [mods[l + 1, 1:2, i * d:(i + 1) * d] for i in range(2)]
            x_lat, h_lat = resnorm(x_lat, y, ml[5], ng[3], (ngn, mln[0], mln[1]))
            x_ctx, h2c = resnorm(x_ctx, mix_ctx, mc[2], ng[1], (ng[2], mc[3], mc[4]))
            yc = _ffn(h2c, lp)
            x_ctx, h_ctx = resnorm(x_ctx, yc, mc[5], ng[3], (ngn, mcn[0], mcn[1]))
    return x_lat[None]
```

```python
import functools
import math

import numpy as np
import jax
import jax.numpy as jnp
from jax import lax
from jax.experimental import pallas as pl
from jax.experimental.pallas import tpu as pltpu

F32, BF16 = jnp.float32, jnp.bfloat16
HIGHEST = lax.Precision.HIGHEST

D_MODEL = 4096
DEPTH = 2
GRID_W = 64
BRANCH_W = 1024
HEAD = 64
HEADS = BRANCH_W // HEAD
PAIRS = HEADS // 2
LORA_W, LORA_A, LORA_G = 64, 64, 160
A_PROJ = 3 * BRANCH_W + LORA_W + LORA_A + LORA_G
A_GN_EPS = 64e-5
B_CHUNK = 128
B_GROUPS = 8
C_KERNEL = 31
D_GROUPS = 4
D_GROUP_CH = BRANCH_W // D_GROUPS
FFN_HIDDEN = 11008
RMS_EPS = 1e-6
LN_EPS = 1e-5

LANES = 128
VMEM_BUDGET = 60 * 1024 * 1024

A_LORA = 3 * BRANCH_W
A_GC = A_LORA + LANES
A_PAD = A_GC + 3 * LANES

CHUNK = 64
CONV_HALO = 16


def _cparams(n_axes, vmem_bytes):
    limit = int(min(max(vmem_bytes + (8 << 20), 32 << 20), VMEM_BUDGET))
    return pltpu.CompilerParams(dimension_semantics=("arbitrary",) * n_axes, vmem_limit_bytes=limit)


def _dot(a, b):
    return jnp.dot(a.astype(BF16), b.astype(BF16), preferred_element_type=F32)


def _dot_nt(a, b):
    return lax.dot_general(a.astype(BF16), b.astype(BF16), (((1,), (1,)), ((), ())), preferred_element_type=F32)


def _dot_tn(a, b):
    return lax.dot_general(a.astype(BF16), b.astype(BF16), (((0,), (0,)), ((), ())), preferred_element_type=F32)


def _split3(x):
    hi = x.astype(BF16)
    r1 = x - hi.astype(F32)
    mid = r1.astype(BF16)
    lo = (r1 - mid.astype(F32)).astype(BF16)
    return hi, mid, lo


def _dot_exact_rhs(x, e):
    hi, mid, lo = _split3(x)
    f = lambda p: jnp.dot(p, e, preferred_element_type=F32)
    return f(hi) + f(mid) + f(lo)


def _dot_exact_lhs(e, x):
    hi, mid, lo = _split3(x)
    f = lambda p: jnp.dot(e, p, preferred_element_type=F32)
    return f(hi) + f(mid) + f(lo)


def _sigmoid(x):
    return 1.0 / (1.0 + jnp.exp(-x))


def _silu(x):
    return x * _sigmoid(x)


def _softplus(x):
    return jnp.maximum(x, 0.0) + jnp.log(1.0 + jnp.exp(-jnp.abs(x)))


def _gelu_tanh(x):
    return 0.5 * x * (1.0 + jnp.tanh(math.sqrt(2.0 / math.pi) * (x + 0.044715 * (x * x * x))))


def _rmsnorm(x, g):
    return (x * lax.rsqrt(jnp.mean(x * x, axis=-1, keepdims=True) + RMS_EPS)) * g


def _layernorm(x, g, b):
    mu = jnp.mean(x, axis=-1, keepdims=True)
    xc = x - mu
    var = jnp.mean(xc * xc, axis=-1, keepdims=True)
    return (xc * lax.rsqrt(var + LN_EPS)) * g + b


MOD_ROWS = 64


def _mod_body(c_ref, w_ref, b_ref, o_ref, s_ref):
    @pl.when((pl.program_id(0) == 0) & (pl.program_id(1) == 0))
    def _():
        s_ref[...] = _silu(c_ref[...])

    d, tn = w_ref.shape[1], w_ref.shape[2]

    def step(k, acc):
        r0 = pl.multiple_of(k * MOD_ROWS, MOD_ROWS)
        w = w_ref[0, pl.ds(r0, MOD_ROWS), :]
        s = s_ref[pl.ds(r0, MOD_ROWS), :]
        return acc[0] + w * s[:, 0:1], acc[1] + w * s[:, 1:2]

    z = jnp.zeros((MOD_ROWS, tn), F32)
    a0, a1 = lax.fori_loop(0, d // MOD_ROWS, step, (z, z))
    bias = b_ref[0]
    o_ref[0] = jnp.concatenate([jnp.sum(a0, axis=0, keepdims=True) + bias, jnp.sum(a1, axis=0, keepdims=True) + bias,
                                jnp.zeros((6, tn), F32)], axis=0)


def modulation(c_cols, mod_w, mod_b):
    depth, d, n = mod_w.shape
    tn = 256
    return pl.pallas_call(
        _mod_body,
        grid=(depth, n // tn),
        in_specs=[pl.BlockSpec((d, 8), lambda l, j: (0, 0)),
                  pl.BlockSpec((1, d, tn), lambda l, j: (l, 0, j)),
                  pl.BlockSpec((1, 1, tn), lambda l, j: (l, 0, j))],
        out_specs=pl.BlockSpec((1, 8, tn), lambda l, j: (l, 0, j)),
        out_shape=jax.ShapeDtypeStruct((depth, 8, n), F32),
        scratch_shapes=[pltpu.VMEM((d, 8), F32)],
        compiler_params=_cparams(2, 2 * d * tn * 4 + 3 * d * LANES * 4),
        name="modulation",
    )(c_cols, mod_w, mod_b)


def _normmod_body(x_ref, g_ref, sh_ref, sc_ref, h_ref):
    h = _rmsnorm(x_ref[...], g_ref[...]) * (1.0 + sc_ref[...]) + sh_ref[...]
    h_ref[...] = h.astype(h_ref.dtype)


def normmod(x, g, shift, scale):
    t, d = x.shape
    tm = 256
    row = pl.BlockSpec((1, d), lambda i: (0, 0))
    return pl.pallas_call(
        _normmod_body,
        grid=(t // tm,),
        in_specs=[pl.BlockSpec((tm, d), lambda i: (i, 0)), row, row, row],
        out_specs=pl.BlockSpec((tm, d), lambda i: (i, 0)),
        out_shape=jax.ShapeDtypeStruct((t, d), BF16),
        compiler_params=_cparams(1, 2 * tm * d * 6),
        name="normmod",
    )(x, g, shift, scale)


def _resnorm_body(x_ref, y_ref, gate_ref, gpost_ref, gpre_ref, sh_ref, sc_ref, xo_ref, h_ref):
    xn = x_ref[...] + gate_ref[...] * _rmsnorm(y_ref[...], gpost_ref[...])
    xo_ref[...] = xn
    h = _rmsnorm(xn, gpre_ref[...]) * (1.0 + sc_ref[...]) + sh_ref[...]
    h_ref[...] = h.astype(h_ref.dtype)


def _res_body(x_ref, y_ref, gate_ref, gpost_ref, xo_ref):
    xo_ref[...] = x_ref[...] + gate_ref[...] * _rmsnorm(y_ref[...], gpost_ref[...])


def resnorm(x, y, gate, g_post, nxt=None):
    t, d = x.shape
    tm = 256
    row = pl.BlockSpec((1, d), lambda i: (0, 0))
    tile = pl.BlockSpec((tm, d), lambda i: (i, 0))
    if nxt is None:
        return pl.pallas_call(
            _res_body, grid=(t // tm,), in_specs=[tile, tile, row, row], out_specs=tile,
            out_shape=jax.ShapeDtypeStruct((t, d), F32),
            compiler_params=_cparams(1, 2 * tm * d * 12), name="residual",
        )(x, y, gate, g_post)
    return pl.pallas_call(
        _resnorm_body, grid=(t // tm,), in_specs=[tile, tile, row, row, row, row, row],
        out_specs=[tile, tile],
        out_shape=[jax.ShapeDtypeStruct((t, d), F32), jax.ShapeDtypeStruct((t, d), BF16)],
        compiler_params=_cparams(1, 2 * tm * d * 14), name="residual_norm",
    )(x, y, gate, g_post, *nxt)


def _mm_body(x_ref, w_ref, o_ref):
    o_ref[...] = jnp.dot(x_ref[...], w_ref[...], preferred_element_type=F32).astype(o_ref.dtype)


def _mm_cast_body(x_ref, w_ref, o_ref, wb_ref):
    @pl.when(pl.program_id(1) == 0)
    def _():
        wb_ref[...] = w_ref[...].astype(BF16)

    o_ref[...] = jnp.dot(x_ref[...], wb_ref[...], preferred_element_type=F32).astype(o_ref.dtype)


def _wspec(w, layer, rows, tn, col_block):
    if w.ndim == 3:
        return pl.BlockSpec((None, rows, tn), lambda j, i: (layer, 0, col_block(j)))
    return pl.BlockSpec((rows, tn), lambda j, i: (0, col_block(j)))


def matmul(x, w, n, col0, tn, out_dtype, name, layer=None):
    m, k = x.shape
    tm = min(m, 512 if k <= 4096 else 256)
    j0 = col0 // tn
    assert col0 % tn == 0 and n % tn == 0 and m % tm == 0
    cast = w.dtype == F32
    vm = 2 * (tm * k * 2 + k * tn * w.dtype.itemsize + tm * tn * 4) + (k * tn * 2 if cast else 0)
    return pl.pallas_call(
        _mm_cast_body if cast else _mm_body,
        grid=(n // tn, m // tm),
        in_specs=[pl.BlockSpec((tm, k), lambda j, i: (i, 0)),
                  _wspec(w, layer, k, tn, lambda j: j + j0)],
        out_specs=pl.BlockSpec((tm, tn), lambda j, i: (i, j)),
        out_shape=jax.ShapeDtypeStruct((m, n), out_dtype),
        scratch_shapes=[pltpu.VMEM((k, tn), BF16)] if cast else [],
        compiler_params=_cparams(2, vm),
        name=name,
    )(x, w)


def _ffn_up_body(x_ref, w1_ref, w3_ref, o_ref, w1b_ref, w3b_ref):
    @pl.when(pl.program_id(1) == 0)
    def _():
        w1b_ref[...] = w1_ref[...].astype(BF16)
        w3b_ref[...] = w3_ref[...].astype(BF16)

    x = x_ref[...]
    a = jnp.dot(x, w1b_ref[...], preferred_element_type=F32)
    b = jnp.dot(x, w3b_ref[...], preferred_element_type=F32)
    o_ref[...] = (_silu(a) * b).astype(o_ref.dtype)


def ffn_up(h, w1, w3, layer):
    m, k = h.shape
    n = w1.shape[-1]
    tm, tn = min(m, 512), 256
    vm = 2 * (tm * k * 2 + 2 * k * tn * 4 + tm * tn * 2) + 2 * k * tn * 2 + 3 * tm * tn * 4
    return pl.pallas_call(
        _ffn_up_body,
        grid=(n // tn, m // tm),
        in_specs=[pl.BlockSpec((tm, k), lambda j, i: (i, 0)),
                  _wspec(w1, layer, k, tn, lambda j: j),
                  _wspec(w3, layer, k, tn, lambda j: j)],
        out_specs=pl.BlockSpec((tm, tn), lambda j, i: (i, j)),
        out_shape=jax.ShapeDtypeStruct((m, n), BF16),
        scratch_shapes=[pltpu.VMEM((k, tn), BF16), pltpu.VMEM((k, tn), BF16)],
        compiler_params=_cparams(2, vm),
        name="ffn_up",
    )(h, w1, w3)


def _merge_body(h_ref, za_ref, zb_ref, zc_ref, zd_ref,
                ga_ref, gb_ref, gc_ref, gd_ref, ba_ref, bb_ref, bc_ref, bd_ref,
                wa_ref, wb_ref, wc_ref, wd_ref, o_ref, gbf_ref, wbf_ref):
    branches = ((za_ref, ga_ref, ba_ref, wa_ref), (zb_ref, gb_ref, bb_ref, wb_ref),
                (zc_ref, gc_ref, bc_ref, wc_ref), (zd_ref, gd_ref, bd_ref, wd_ref))

    @pl.when(pl.program_id(1) == 0)
    def _():
        for br, (_, g_ref, _, w_ref) in enumerate(branches):
            gbf_ref[br] = g_ref[...].astype(BF16)
            wbf_ref[br] = w_ref[...].astype(BF16)

    h = h_ref[...]
    acc = None
    for br, (z_ref, _, b_ref, _) in enumerate(branches):
        gate = _sigmoid(jnp.dot(h, gbf_ref[br], preferred_element_type=F32) + b_ref[...])
        y = jnp.dot(z_ref[...], wbf_ref[br], preferred_element_type=F32)
        acc = gate * y if acc is None else acc + gate * y
    o_ref[...] = acc.astype(o_ref.dtype)


def merge(h, zs, gate_w, gate_b, w_outs, layer):
    m, d = h.shape
    bw = zs[0].shape[1]
    tm, tn = min(m, 512), 256
    nj = d // tn
    once = pl.Buffered(1)
    hspec = pl.BlockSpec((tm, d), lambda j, i: (i, 0))
    zspec = pl.BlockSpec((tm, bw), lambda j, i: (i, 0))
    gspecs = [pl.BlockSpec((None, d, tn), functools.partial(lambda j, i, br: (layer, 0, br * nj + j), br=br),
                           pipeline_mode=once) for br in range(4)]
    bspecs = [pl.BlockSpec((1, tn), functools.partial(lambda j, i, br: (0, br * nj + j), br=br)) for br in range(4)]
    wspec = pl.BlockSpec((None, bw, tn), lambda j, i: (layer, 0, j), pipeline_mode=once)
    vm = (2 * (tm * d * 2 + 4 * tm * bw * 2 + tm * tn * 2) + 4 * (d + bw) * tn * (4 + 2) + 4 * tm * tn * 4)
    return pl.pallas_call(
        _merge_body,
        grid=(nj, m // tm),
        in_specs=[hspec] + [zspec] * 4 + gspecs + bspecs + [wspec] * 4,
        out_specs=pl.BlockSpec((tm, tn), lambda j, i: (i, j)),
        out_shape=jax.ShapeDtypeStruct((m, d), BF16),
        scratch_shapes=[pltpu.VMEM((4, d, tn), BF16), pltpu.VMEM((4, bw, tn), BF16)],
        compiler_params=_cparams(2, vm),
        name="merge",
    )(h, *zs, gate_w, gate_w, gate_w, gate_w, gate_b, gate_b, gate_b, gate_b, *w_outs)


def _token_shift(x, mu, row_len):
    tm = x.shape[0]
    pos = lax.broadcasted_iota(jnp.int32, x.shape, 0) & (row_len - 1)
    prev = jnp.where(pos == 0, 0.0, pltpu.roll(x, 1, 0))
    nxt = jnp.where(pos == row_len - 1, 0.0, pltpu.roll(x, tm - 1, 0))
    return x + mu * (0.5 * (prev + nxt) - x)


def _aprep_body(r_ref, k_ref, v_ref, wa_ref, g1_ref, g2_ref,
                mur_ref, muk_ref, muv_ref, muwa_ref, mug1_ref, mug2_ref,
                w0_ref, wup_ref, a0_ref, aup_ref, gup_ref, kk_ref, ka_ref, rk_ref, e_ref,
                ro_ref, vo_ref, kko_ref, lw_ref, kd_ref, b_ref, bonus_ref, g_ref, *, row_len):
    e = e_ref[...]
    xr = _token_shift(r_ref[...], mur_ref[...], row_len)
    xk = _token_shift(k_ref[...], muk_ref[...], row_len)
    xv = _token_shift(v_ref[...], muv_ref[...], row_len)
    xwa = _token_shift(wa_ref[...], muwa_ref[...], row_len)
    xg1 = _token_shift(g1_ref[...], mug1_ref[...], row_len)
    xg2 = _token_shift(g2_ref[...], mug2_ref[...], row_len)
    kk = xk * kk_ref[...]
    kk = kk * lax.rsqrt(jnp.maximum(_dot_exact_rhs(kk * kk, e), 1e-12))
    tw = jnp.tanh(xwa)
    bonus = None
    for d in range(2):
        w = -_softplus(-(w0_ref[d] + _dot(tw, wup_ref[d]))) - 0.5
        a = _sigmoid(a0_ref[d] + _dot(xwa, aup_ref[d]))
        kd = xk * (1.0 + (a - 1.0) * ka_ref[...])
        lw_ref[d] = -jnp.exp(w)
        kd_ref[d] = kd
        b_ref[d] = kk * a
        bn = _dot_exact_rhs(xr * kd * rk_ref[...], e) * xv
        bonus = bn if bonus is None else bonus + bn
    ro_ref[...] = xr
    vo_ref[...] = xv
    kko_ref[...] = kk
    bonus_ref[...] = bonus
    g_ref[...] = _dot(_sigmoid(xg1), gup_ref[0]) + _dot(_sigmoid(xg2), gup_ref[1])


def a_prep(pa, ap, row_len):
    t = pa.shape[0]
    tm = 256
    nb = BRANCH_W // LANES
    col = lambda c: pl.BlockSpec((tm, LANES), functools.partial(lambda i, p, c: (i, c + p), c=c))
    fix = lambda c: pl.BlockSpec((tm, LANES), functools.partial(lambda i, p, c: (i, c), c=c))
    mcol = lambda c: pl.BlockSpec((1, LANES), functools.partial(lambda i, p, c: (0, c + p), c=c))
    mfix = lambda c: pl.BlockSpec((1, LANES), functools.partial(lambda i, p, c: (0, c), c=c))
    prow = pl.BlockSpec((1, LANES), lambda i, p: (0, p))
    p2 = pl.BlockSpec((2, 1, LANES), lambda i, p: (0, 0, p))
    up2 = pl.BlockSpec((2, LANES, LANES), lambda i, p: (0, 0, p))
    out1 = pl.BlockSpec((tm, LANES), lambda i, p: (i, p))
    out2 = pl.BlockSpec((2, tm, LANES), lambda i, p: (0, i, p))
    s1 = jax.ShapeDtypeStruct((t, BRANCH_W), F32)
    s2 = jax.ShapeDtypeStruct((2, t, BRANCH_W), F32)
    lora, gc = A_LORA // LANES, A_GC // LANES
    return pl.pallas_call(
        functools.partial(_aprep_body, row_len=row_len),
        grid=(t // tm, nb),
        in_specs=[col(0), col(nb), col(2 * nb), fix(lora), fix(gc), fix(gc + 1),
                  mcol(0), mcol(nb), mcol(2 * nb), mfix(lora), mfix(gc), mfix(gc + 1),
                  p2, up2, p2, up2, up2, prow, prow, prow,
                  pl.BlockSpec((LANES, LANES), lambda i, p: (0, 0))],
        out_specs=[out1, out1, out1, out2, out2, out2, out1, out1],
        out_shape=[s1, s1, s1, s2, s2, s2, s1, s1],
        compiler_params=_cparams(2, 40 * tm * LANES * 4),
        name="rwkv_prep",
    )(pa, pa, pa, pa, pa, pa, ap["mu"], ap["mu"], ap["mu"], ap["mu"], ap["mu"], ap["mu"],
      ap["w0"], ap["w_up"], ap["a0"], ap["a_up"], ap["g_up"], ap["k_k"], ap["k_a"], ap["r_k"], ap["e2"])


def _pair_rows(x):
    lane = lax.broadcasted_iota(jnp.int32, x.shape, 1)
    return jnp.concatenate([jnp.where(lane < HEAD, x, 0.0), jnp.where(lane >= HEAD, x, 0.0)], axis=0)


MSK_STRICT, MSK_INCL, MSK_EYE, MSK_DIAG8, MSK_OFF8, MSK_OFF16, MSK_OFF32 = range(7)


def _chunk_mask_tables():
    n = 2 * CHUNK
    r = np.arange(n)[:, None]
    c = np.arange(n)[None, :]
    same = (r // CHUNK) == (c // CHUNK)
    out = np.zeros((2, 7, n, n), np.float32)
    for d in range(2):
        before = (c > r) if d else (c < r)
        out[d, MSK_STRICT] = same & before
        out[d, MSK_INCL] = same & (before | (r == c))
        out[d, MSK_EYE] = r == c
        out[d, MSK_DIAG8] = ((r // 8) == (c // 8)) & before
        for idx, s in ((MSK_OFF8, 8), (MSK_OFF16, 16), (MSK_OFF32, 32)):
            blk = (r // (2 * s)) == (c // (2 * s))
            rh, ch = (r // s) % 2, (c // s) % 2
            out[d, idx] = blk & ((rh == 0) & (ch == 1) if d else (rh == 1) & (ch == 0))
    return jnp.asarray(out)


def _scan_pre_body(r_ref, v_ref, kk_ref, lw_ref, kd_ref, b_ref, tri_ref, msk_ref, rhat_ref, y0_ref, m_ref, n_ref, *, nchunk):
    c, n = CHUNK, 2 * CHUNK
    chains = [(d, ci) for d in range(2) for ci in range(nchunk)]
    each = lambda f, *cols: [f(*xs) for xs in zip(*cols)]
    msk = lambda d, k: msk_ref[d, k]
    rows = lambda ci: slice(ci * c, (ci + 1) * c)

    lw = [lw_ref[d, rows(ci), :] for d, ci in chains]
    cum = [_dot_exact_lhs(tri_ref[d], x) for (d, _), x in zip(chains, lw)]
    tot = [x[0:1] if d else x[c - 1:c] for (d, _), x in zip(chains, cum)]
    g_inv = each(lambda x: jnp.exp(-x), cum)
    g_tail = each(lambda t, x: jnp.exp(t - x), tot, cum)
    atp = [_pair_rows(-kk_ref[rows(ci), :] * jnp.exp(x - l)) for (_, ci), x, l in zip(chains, cum, lw)]
    rtp = [_pair_rows(r_ref[rows(ci), :] * jnp.exp(x)) for (_, ci), x in zip(chains, cum)]
    btp = [_pair_rows(b_ref[d, rows(ci), :] * g) for (d, ci), g in zip(chains, g_inv)]
    ktp = [_pair_rows(kd_ref[d, rows(ci), :] * g) for (d, ci), g in zip(chains, g_inv)]
    bhp = [_pair_rows(b_ref[d, rows(ci), :] * g) for (d, ci), g in zip(chains, g_tail)]
    khp = [_pair_rows(kd_ref[d, rows(ci), :] * g) for (d, ci), g in zip(chains, g_tail)]
    vp = [_pair_rows(v_ref[rows(ci), :]) for _, ci in chains]

    big = each(lambda a, r, b, k: _dot_nt(jnp.concatenate([a, r], axis=0), jnp.concatenate([b, k], axis=0)),
               atp, rtp, btp, ktp)
    a_ab = [jnp.where(msk(d, MSK_STRICT) > 0.0, x[:n, :n], 0.0) for (d, _), x in zip(chains, big)]
    a_ak = [jnp.where(msk(d, MSK_STRICT) > 0.0, x[:n, n:], 0.0) for (d, _), x in zip(chains, big)]
    a_rb = [jnp.where(msk(d, MSK_INCL) > 0.0, x[n:, :n], 0.0) for (d, _), x in zip(chains, big)]
    a_rk = [jnp.where(msk(d, MSK_INCL) > 0.0, x[n:, n:], 0.0) for (d, _), x in zip(chains, big)]

    n8 = [x * msk(d, MSK_DIAG8) for (d, _), x in zip(chains, a_ab)]
    t = [msk(d, MSK_EYE) + x for (d, _), x in zip(chains, n8)]
    n2 = each(_dot, n8, n8)
    t = each(lambda x, p: x + _dot(p, x), t, n2)
    n4 = each(_dot, n2, n2)
    t = each(lambda x, p: x + _dot(p, x), t, n4)
    for idx in (MSK_OFF8, MSK_OFF16, MSK_OFF32):
        off = [x * msk(d, idx) for (d, _), x in zip(chains, a_ab)]
        tn = each(_dot, t, off)
        t = each(lambda x, p: x + _dot(p, x), t, tn)

    akv = each(_dot, a_ak, vp)
    w_u = each(lambda ti, a, x: _dot(ti, jnp.concatenate([a, x], axis=1)), t, atp, akv)
    ry = each(_dot, a_rb, w_u)
    rkv = each(_dot, a_rk, vp)
    mn = each(_dot_tn, bhp, w_u)
    kv = each(_dot_tn, khp, vp)
    for i, (d, ci) in enumerate(chains):
        rhat_p = rtp[i] + ry[i][:, :n]
        y0_p = ry[i][:, n:] + rkv[i]
        rhat_ref[d, rows(ci), :] = rhat_p[:c] + rhat_p[c:]
        y0_ref[d, rows(ci), :] = y0_p[:c] + y0_p[c:]
        m_ref[d, ci, 0] = mn[i][:, :n] + msk(d, MSK_EYE) * jnp.exp(tot[i])
        n_ref[d, ci, 0] = mn[i][:, n:] + kv[i]


def scan_pre(r, v, kk, lw, kd, b, tri, msk):
    t = r.shape[0]
    nchunk = 4
    tm = nchunk * CHUNK
    nc = t // CHUNK
    in1 = pl.BlockSpec((tm, LANES), lambda i, p: (i, p))
    in2 = pl.BlockSpec((2, tm, LANES), lambda i, p: (0, i, p))
    mspec = pl.BlockSpec((2, nchunk, 1, LANES, LANES), lambda i, p: (0, i, p, 0, 0))
    s2 = jax.ShapeDtypeStruct((2, t, BRANCH_W), F32)
    sm = jax.ShapeDtypeStruct((2, nc, PAIRS, LANES, LANES), F32)
    return pl.pallas_call(
        functools.partial(_scan_pre_body, nchunk=nchunk),
        grid=(t // tm, PAIRS),
        in_specs=[in1, in1, in1, in2, in2, in2, pl.BlockSpec((2, CHUNK, CHUNK), lambda i, p: (0, 0, 0)),
                  pl.BlockSpec(msk.shape, lambda i, p: (0, 0, 0, 0))],
        out_specs=[in2, in2, mspec, mspec],
        out_shape=[s2, s2, sm, sm],
        compiler_params=_cparams(2, 16 << 20),
        name="rwkv_chunk_maps",
    )(r, v, kk, lw, kd, b, tri, msk)


def _scan_seq_body(s0_ref, mf_ref, mb_ref, nf_ref, nb_ref, rf_ref, rb_ref, yf0_ref, yb0_ref,
                   yf_ref, yb_ref, sfin_ref, s_ref):
    c = pl.program_id(0)

    @pl.when(c == 0)
    def _():
        s_ref[...] = s0_ref[...]

    for d, (m_ref, n_ref, rh_ref, y0_ref, y_ref) in enumerate(((mf_ref, nf_ref, rf_ref, yf0_ref, yf_ref),
                                                                 (mb_ref, nb_ref, rb_ref, yb0_ref, yb_ref))):
        for p in range(PAIRS):
            ln = slice(p * LANES, (p + 1) * LANES)
            s = s_ref[d, p]
            y_ref[:, ln] = _dot(rh_ref[0, :, ln], s) + y0_ref[0, :, ln]
            s_ref[d, p] = _dot(m_ref[0, 0, p], s) + n_ref[0, 0, p]

    @pl.when(c == pl.num_programs(0) - 1)
    def _():
        sfin_ref[...] = s_ref[...]


def scan_seq(s0, rhat, y0, m, n):
    t = rhat.shape[1]
    nc = t // CHUNK
    fwd5 = lambda c: (0, c, 0, 0, 0)
    bwd5 = lambda c: (1, nc - 1 - c, 0, 0, 0)
    mblk = (1, 1, PAIRS, LANES, LANES)
    rblk = (1, CHUNK, BRANCH_W)
    sblk = pl.BlockSpec((2, PAIRS, LANES, LANES), lambda c: (0, 0, 0, 0))
    ys = jax.ShapeDtypeStruct((t, BRANCH_W), F32)
    return pl.pallas_call(
        _scan_seq_body,
        grid=(nc,),
        in_specs=[sblk,
                  pl.BlockSpec(mblk, fwd5), pl.BlockSpec(mblk, bwd5),
                  pl.BlockSpec(mblk, fwd5), pl.BlockSpec(mblk, bwd5),
                  pl.BlockSpec(rblk, lambda c: (0, c, 0)), pl.BlockSpec(rblk, lambda c: (1, nc - 1 - c, 0)),
                  pl.BlockSpec(rblk, lambda c: (0, c, 0)), pl.BlockSpec(rblk, lambda c: (1, nc - 1 - c, 0))],
        out_specs=[pl.BlockSpec((CHUNK, BRANCH_W), lambda c: (c, 0)),
                   pl.BlockSpec((CHUNK, BRANCH_W), lambda c: (nc - 1 - c, 0)),
                   sblk],
        out_shape=[ys, ys, jax.ShapeDtypeStruct((2, PAIRS, LANES, LANES), F32)],
        scratch_shapes=[pltpu.VMEM((2, PAIRS, LANES, LANES), F32)],
        compiler_params=_cparams(1, 16 << 20),
        name="rwkv_chain",
    )(s0, m, m, n, n, rhat, rhat, y0, y0)


def _areadout_body(yf_ref, yb_ref, bonus_ref, g_ref, lng_ref, lnb_ref, e_ref, o_ref):
    e = e_ref[...]
    y = yf_ref[...] + yb_ref[...]
    mu = _dot_exact_rhs(y, e) * (1.0 / HEAD)
    yc = y - mu
    var = _dot_exact_rhs(yc * yc, e) * (1.0 / HEAD)
    yn = (yc * lax.rsqrt(var + A_GN_EPS)) * lng_ref[...] + lnb_ref[...]
    o_ref[...] = ((yn + bonus_ref[...]) * g_ref[...]).astype(o_ref.dtype)


def a_readout(yf, yb, bonus, g, ap):
    t = yf.shape[0]
    tm = 256
    blk = pl.BlockSpec((tm, LANES), lambda i, p: (i, p))
    prow = pl.BlockSpec((1, LANES), lambda i, p: (0, p))
    return pl.pallas_call(
        _areadout_body,
        grid=(t // tm, BRANCH_W // LANES),
        in_specs=[blk, blk, blk, blk, prow, prow, pl.BlockSpec((LANES, LANES), lambda i, p: (0, 0))],
        out_specs=blk,
        out_shape=jax.ShapeDtypeStruct((t, BRANCH_W), BF16),
        compiler_params=_cparams(2, 16 * tm * LANES * 4),
        name="rwkv_readout",
    )(yf, yb, bonus, g, ap["ln_g"], ap["ln_b"], ap["e2"])


def _bmix_body(u_ref, v_ref, lng_ref, lnb_ref, ws_ref, bs_ref, o_ref):
    u = _gelu_tanh(u_ref[...])
    v = _layernorm(_gelu_tanh(v_ref[...]), lng_ref[...], lnb_ref[...]).astype(BF16)
    for ci in range(u.shape[0] // B_CHUNK):
        rows = slice(ci * B_CHUNK, (ci + 1) * B_CHUNK)
        for g in range(B_GROUPS):
            ln = slice(g * LANES, (g + 1) * LANES)
            s = jnp.dot(ws_ref[g], v[rows, ln], preferred_element_type=F32) + bs_ref[:, ln]
            o_ref[rows, ln] = (u[rows, ln] * s).astype(o_ref.dtype)


def b_mix(pb, bp):
    t = pb.shape[0]
    tm = 256
    row = pl.BlockSpec((1, BRANCH_W), lambda i: (0, 0))
    return pl.pallas_call(
        _bmix_body,
        grid=(t // tm,),
        in_specs=[pl.BlockSpec((tm, BRANCH_W), lambda i: (i, 0)), pl.BlockSpec((tm, BRANCH_W), lambda i: (i, 1)),
                  row, row, pl.BlockSpec((B_GROUPS, B_CHUNK, B_CHUNK), lambda i: (0, 0, 0)),
                  pl.BlockSpec((B_CHUNK, BRANCH_W), lambda i: (0, 0))],
        out_specs=pl.BlockSpec((tm, BRANCH_W), lambda i: (i, 0)),
        out_shape=jax.ShapeDtypeStruct((t, BRANCH_W), BF16),
        compiler_params=_cparams(1, 12 * tm * BRANCH_W * 4),
        name="gmlp_mix",
    )(pb, pb, bp["ln_g"], bp["ln_b"], bp["ws"], bp["bs"])


def _conv_body(ac_ref, gc_ref, ap_ref, gp_ref, an_ref, gn_ref, dw_ref, dwb_ref, lng_ref, lnb_ref,
               o_ref, zs_ref, cs_ref, *, tm):
    i = pl.program_id(0)
    last = pl.num_programs(0) - 1
    glu = lambda a, g: a * _sigmoid(g)
    zs_ref[0:CONV_HALO, :] = jnp.where(i == 0, 0.0, glu(ap_ref[...], gp_ref[...]))
    zs_ref[CONV_HALO:CONV_HALO + tm, :] = glu(ac_ref[...], gc_ref[...])
    zs_ref[CONV_HALO + tm:, :] = jnp.where(i == last, 0.0, glu(an_ref[...], gn_ref[...]))
    rb, lb = 32, 256
    half = C_KERNEL // 2
    for r0 in range(0, tm, rb):
        for l0 in range(0, BRANCH_W, lb):
            acc = jnp.zeros((rb, lb), F32) + dwb_ref[:, l0:l0 + lb]
            for j in range(C_KERNEL):
                off = CONV_HALO + r0 + j - half
                acc = acc + zs_ref[off:off + rb, l0:l0 + lb] * dw_ref[j:j + 1, l0:l0 + lb]
            cs_ref[r0:r0 + rb, l0:l0 + lb] = acc
    o_ref[...] = _silu(_layernorm(cs_ref[...], lng_ref[...], lnb_ref[...])).astype(o_ref.dtype)


def conv_mix(pc, cp):
    t = pc.shape[0]
    tm = 256
    hb = tm // CONV_HALO
    nh = t // CONV_HALO
    cur = lambda c: pl.BlockSpec((tm, BRANCH_W), functools.partial(lambda i, c: (i, c), c=c))
    prev = lambda c: pl.BlockSpec((CONV_HALO, BRANCH_W), functools.partial(lambda i, c: (jnp.maximum(i * hb - 1, 0), c), c=c))
    nxt = lambda c: pl.BlockSpec((CONV_HALO, BRANCH_W), functools.partial(lambda i, c: (jnp.minimum((i + 1) * hb, nh - 1), c), c=c))
    row = pl.BlockSpec((1, BRANCH_W), lambda i: (0, 0))
    return pl.pallas_call(
        functools.partial(_conv_body, tm=tm),
        grid=(t // tm,),
        in_specs=[cur(0), cur(1), prev(0), prev(1), nxt(0), nxt(1),
                  pl.BlockSpec((C_KERNEL + 1, BRANCH_W), lambda i: (0, 0)), row, row, row],
        out_specs=pl.BlockSpec((tm, BRANCH_W), lambda i: (i, 0)),
        out_shape=jax.ShapeDtypeStruct((t, BRANCH_W), BF16),
        scratch_shapes=[pltpu.VMEM((tm + 2 * CONV_HALO, BRANCH_W), F32), pltpu.VMEM((tm, BRANCH_W), F32)],
        compiler_params=_cparams(1, 16 * tm * BRANCH_W * 4),
        name="conv_mix",
    )(pc, pc, pc, pc, pc, pc, cp["dw"], cp["dw_b"], cp["ln_g"], cp["ln_b"])


FFT_N1, FFT_N2 = 64, 128


def _dft_tables(t):
    two_pi = 2.0 * np.pi
    cidx = np.arange(D_GROUP_CH)
    ph = two_pi * np.outer(cidx, cidx) / D_GROUP_CH
    chan = np.concatenate([np.cos(ph), np.sin(ph)], axis=0)
    if t <= 256:
        n = np.arange(t)
        th = two_pi * np.outer(n, n) / t
        m2 = np.concatenate([np.cos(th), -np.sin(th)], axis=0)
        return None, m2.astype(np.float32), chan.astype(np.float32)
    n1, n2 = FFT_N1, FFT_N2
    assert t == n1 * n2
    k1 = np.arange(n1)
    tok = (n2 * np.arange(n1))[None, None, :] + np.arange(n2)[:, None, None]
    th = two_pi * (k1[None, :, None] * tok) / t
    g1 = np.concatenate([np.cos(th), -np.sin(th)], axis=1)
    q = np.arange(n2)
    th2 = two_pi * np.outer(q, q) / n2
    c2, s2 = np.cos(th2), np.sin(th2)
    m2 = np.block([[c2, s2], [-s2, c2]])
    return g1.astype(np.float32), m2.astype(np.float32), chan.astype(np.float32)


def _fft1_body(x_ref, g_ref, o_ref):
    o_ref[...] = _dot(g_ref[0], x_ref[...])


def _fft2_body(z_ref, m2_ref, ch_ref, o_ref, *, n_out, scale, stacked):
    z = jnp.concatenate([z_ref[0], z_ref[1]], axis=0) if stacked else z_ref[...]
    x = _dot(m2_ref[...], z)
    xr, xi = x[:n_out], x[n_out:]
    for g in range(D_GROUPS):
        ln = slice(g * D_GROUP_CH, (g + 1) * D_GROUP_CH)
        f = _dot(xr[:, ln], ch_ref[0:D_GROUP_CH]) + _dot(xi[:, ln], ch_ref[D_GROUP_CH:])
        o_ref[:, ln] = (f * scale).astype(o_ref.dtype)


def fourier_mix(pd):
    t = pd.shape[0]
    g1, m2, chan = (None if a is None else jnp.asarray(a) for a in _dft_tables(t))
    scale = 1.0 / math.sqrt(t * D_GROUP_CH)
    chspec = pl.BlockSpec((2 * D_GROUP_CH, D_GROUP_CH), lambda i: (0, 0))
    if g1 is None:
        return pl.pallas_call(
            functools.partial(_fft2_body, n_out=t, scale=scale, stacked=False),
            grid=(1,),
            in_specs=[pl.BlockSpec((t, BRANCH_W), lambda i: (0, 0)), pl.BlockSpec((2 * t, t), lambda i: (0, 0)), chspec],
            out_specs=pl.BlockSpec((t, BRANCH_W), lambda i: (0, 0)),
            out_shape=jax.ShapeDtypeStruct((t, BRANCH_W), BF16),
            compiler_params=_cparams(1, 16 << 20),
            name="fourier_small",
        )(pd, m2, chan)
    n1, n2 = FFT_N1, FFT_N2
    z = pl.pallas_call(
        _fft1_body,
        grid=(n2,),
        in_specs=[pl.BlockSpec((n1, BRANCH_W), lambda q: (0, q)), pl.BlockSpec((1, 2 * n1, n1), lambda q: (q, 0, 0))],
        out_specs=pl.BlockSpec((2 * n1, BRANCH_W), lambda q: (0, q)),
        out_shape=jax.ShapeDtypeStruct((2 * n1, n2 * BRANCH_W), F32),
        compiler_params=_cparams(1, 8 << 20),
        name="fourier_stage1",
    )(pd.reshape(n1, n2 * BRANCH_W), g1)
    f = pl.pallas_call(
        functools.partial(_fft2_body, n_out=n2, scale=scale, stacked=True),
        grid=(n1,),
        in_specs=[pl.BlockSpec((2, None, n2, BRANCH_W), lambda k: (0, k, 0, 0)),
                  pl.BlockSpec((2 * n2, 2 * n2), lambda k: (0, 0)), chspec],
        out_specs=pl.BlockSpec((n2, BRANCH_W), lambda k: (0, k)),
        out_shape=jax.ShapeDtypeStruct((n2, n1 * BRANCH_W), BF16),
        compiler_params=_cparams(1, 16 << 20),
        name="fourier_stage2",
    )(z.reshape(2, n1, n2, BRANCH_W), m2, chan)
    return f.reshape(t, BRANCH_W)


def _align_a(x):
    return jnp.pad(x, [(0, 0)] * (x.ndim - 1) + [(0, A_PAD - x.shape[-1])])


def _pad_rows(x, before, rows):
    return jnp.pad(x, [(0, 0)] * (x.ndim - 2) + [(before, rows - before - x.shape[-2]), (0, 0)])


def _layer_params(l, w_in, a_mu, a_w0, a_w_up, a_a0, a_a_up, a_g_up, a_k_k, a_k_a, a_r_k, a_ln, a_w_out,
                  b_ln, b_ws, b_bs, b_w_out, c_dw, c_dw_b, c_ln, c_w_out, d_w_out, gate_w, gate_b, w_o,
                  ffn_w1, ffn_w3, ffn_w2):
    row = lambda v: v.reshape(1, -1)
    hid = np.arange(LANES) // HEAD
    e2 = jnp.asarray((hid[:, None] == hid[None, :]).astype(np.float32)).astype(BF16)
    g_up = jnp.stack([a_g_up[l][:LANES], _pad_rows(a_g_up[l][LANES:], 0, LANES)])
    ap = dict(mu=_align_a(row(a_mu[l])), w0=a_w0[l][:, None, :], w_up=_pad_rows(a_w_up[l], 0, LANES),
              a0=a_a0[l][:, None, :], a_up=_pad_rows(a_a_up[l], LORA_W, LANES), g_up=g_up,
              k_k=row(a_k_k[l]), k_a=row(a_k_a[l]), r_k=row(a_r_k[l]), ln_g=row(a_ln[l][0]), ln_b=row(a_ln[l][1]), e2=e2)
    bs_exp = jnp.repeat(jnp.swapaxes(b_bs[l], 0, 1), LANES, axis=1)
    return dict(
        layer=l, w_in=w_in, w_bcd=w_in[l][:, A_PROJ:].astype(BF16), ap=ap,
        bp=dict(ln_g=row(b_ln[l][0]), ln_b=row(b_ln[l][1]), ws=b_ws[l].astype(BF16), bs=bs_exp),
        cp=dict(dw=_pad_rows(c_dw[l], 0, C_KERNEL + 1), dw_b=row(c_dw_b[l]), ln_g=row(c_ln[l][0]), ln_b=row(c_ln[l][1])),
        w_outs=[a_w_out, b_w_out, c_w_out, d_w_out],
        gate_w=gate_w, gate_b=row(gate_b[l]), w_o=w_o,
        w1=ffn_w1, w3=ffn_w3, w2=ffn_w2[l].astype(BF16))


def _tri_tables():
    i = np.arange(CHUNK)
    lower = (i[None, :] <= i[:, None]).astype(np.float32)
    return jnp.asarray(np.stack([lower, lower.T])).astype(BF16)


def _rwkv_scan(pa, s0, ap, row_len):
    r, v, kk, lw, kd, b, bonus, g = a_prep(pa, ap, row_len)
    rhat, y0, m, n = scan_pre(r, v, kk, lw, kd, b, _tri_tables(), _chunk_mask_tables())
    yf, yb, s_fin = scan_seq(s0, rhat, y0, m, n)
    return yf, yb, bonus, g, s_fin


def _in_proj_a(h, lp):
    return matmul(h, lp["w_in"], A_PAD, 0, 512, F32, "in_proj_a", layer=lp["layer"])


def _token_mix(h, s0, row_len, lp):
    tn = 512
    pa = _in_proj_a(h, lp)
    yf, yb, bonus, g, s_fin = _rwkv_scan(pa, s0, lp["ap"], row_len)
    za = a_readout(yf, yb, bonus, g, lp["ap"])
    pb = matmul(h, lp["w_bcd"], 2 * BRANCH_W, 0, tn, F32, "in_proj_b")
    zb = b_mix(pb, lp["bp"])
    pc = matmul(h, lp["w_bcd"], 2 * BRANCH_W, 2 * BRANCH_W, tn, F32, "in_proj_c")
    zc = conv_mix(pc, lp["cp"])
    pd = matmul(h, lp["w_bcd"], BRANCH_W, 4 * BRANCH_W, tn, F32, "in_proj_d")
    zd = fourier_mix(pd)
    merged = merge(h, [za, zb, zc, zd], lp["gate_w"], lp["gate_b"], lp["w_outs"], lp["layer"])
    mix = matmul(merged, lp["w_o"], D_MODEL, 0, tn, F32, "out_proj", layer=lp["layer"])
    return mix, s_fin


def _ffn(h2, lp):
    u = ffn_up(h2, lp["w1"], lp["w3"], lp["layer"])
    return matmul(u, lp["w2"], D_MODEL, 0, 512, F32, "ffn_down")


def kernel(x, c, ctx, c_ctx, mod_w, mod_b, norm_g, w_in, a_mu, a_w0, a_w_up, a_a0, a_a_up, a_g_up, a_k_k, a_k_a,
           a_r_k, a_ln, a_w_out, b_ln, b_ws, b_bs, b_w_out, c_dw, c_dw_b, c_ln, c_w_out, d_w_out, gate_w, gate_b,
           w_o, ffn_w1, ffn_w3, ffn_w2):
    depth = mod_w.shape[0]
    d = D_MODEL
    c_cols = jnp.concatenate([c.reshape(d, 1), c_ctx.reshape(d, 1), jnp.zeros((d, 6), F32)], axis=1)
    mods = modulation(c_cols, mod_w, mod_b.reshape(depth, 1, 6 * d))
    x_lat, x_ctx = x[0], ctx[0]
    zero_state = jnp.zeros((2, PAIRS, LANES, LANES), F32)
    weights = (w_in, a_mu, a_w0, a_w_up, a_a0, a_a_up, a_g_up, a_k_k, a_k_a, a_r_k, a_ln, a_w_out, b_ln, b_ws, b_bs,
               b_w_out, c_dw, c_dw_b, c_ln, c_w_out, d_w_out, gate_w, gate_b, w_o, ffn_w1, ffn_w3, ffn_w2)
    h_lat = h_ctx = None
    for l in range(depth):
        last = l == depth - 1
        lp = _layer_params(l, *weights)
        ng = [norm_g[l, i].reshape(1, d) for i in range(4)]
        ml = [mods[l, 0:1, i * d:(i + 1) * d] for i in range(6)]
        mc = [mods[l, 1:2, i * d:(i + 1) * d] for i in range(6)]
        if l == 0:
            h_lat = normmod(x_lat, ng[0], ml[0], ml[1])
            h_ctx = normmod(x_ctx, ng[0], mc[0], mc[1])

        if last:
            ctx_states = _rwkv_scan(_in_proj_a(h_ctx, lp), zero_state, lp["ap"], x_ctx.shape[0])[4]
        else:
            mix_ctx, ctx_states = _token_mix(h_ctx, zero_state, x_ctx.shape[0], lp)

        mix_lat, _ = _token_mix(h_lat, ctx_states, GRID_W, lp)
        x_lat, h2 = resnorm(x_lat, mix_lat, ml[2], ng[1], (ng[2], ml[3], ml[4]))
        y = _ffn(h2, lp)
        if last:
            x_lat = resnorm(x_lat, y, ml[5], ng[3])
        else:
            ngn = norm_g[l + 1, 0].reshape(1, d)
            mln = [mods[l + 1, 0:1, i * d:(i + 1) * d] for i in range(2)]
            mcn = [mods[l + 1, 1:2, i * d:(i + 1) * d] for i in range(2)]
            x_lat, h_lat = resnorm(x_lat, y, ml[5], ng[3], (ngn, mln[0], mln[1]))
            x_ctx, h2c = resnorm(x_ctx, mix_ctx, mc[2], ng[1], (ng[2], mc[3], mc[4]))
            yc = _ffn(h2c, lp)
            x_ctx, h_ctx = resnorm(x_ctx, yc, mc[5], ng[3], (ngn, mcn[0], mcn[1]))
    return x_lat[None]
```

```python
import functools
import math

import numpy as np
import jax
import jax.numpy as jnp
from jax import lax
from jax.experimental import pallas as pl
from jax.experimental.pallas import tpu as pltpu

F32, BF16 = jnp.float32, jnp.bfloat16

D_MODEL = 4096
DEPTH = 2
GRID_W = 64
BRANCH_W = 1024
HEAD = 64
HEADS = BRANCH_W // HEAD
PAIRS = HEADS // 2
LORA_W, LORA_A, LORA_G = 64, 64, 160
A_PROJ = 3 * BRANCH_W + LORA_W + LORA_A + LORA_G
A_GN_EPS = 64e-5
B_CHUNK = 128
B_GROUPS = 8
C_KERNEL = 31
D_GROUPS = 4
D_GROUP_CH = BRANCH_W // D_GROUPS
FFN_HIDDEN = 11008
RMS_EPS = 1e-6
LN_EPS = 1e-5

LANES = 128
SUBLANES = 8
VMEM_BUDGET = 60 * 1024 * 1024

A_LORA = 3 * BRANCH_W
A_GC = A_LORA + LANES
A_PAD = A_GC + 3 * LANES

CHUNK = 64
CONV_HALO = 16


def _cparams(n_axes, vmem_bytes):
    limit = int(min(max(vmem_bytes + (8 << 20), 32 << 20), VMEM_BUDGET))
    return pltpu.CompilerParams(dimension_semantics=("arbitrary",) * n_axes, vmem_limit_bytes=limit)


def _dot(a, b):
    return jnp.dot(a.astype(BF16), b.astype(BF16), preferred_element_type=F32)


def _dot_nt(a, b):
    return lax.dot_general(a.astype(BF16), b.astype(BF16), (((1,), (1,)), ((), ())), preferred_element_type=F32)


def _dot_tn(a, b):
    return lax.dot_general(a.astype(BF16), b.astype(BF16), (((0,), (0,)), ((), ())), preferred_element_type=F32)


def _split3(x):
    hi = x.astype(BF16)
    r1 = x - hi.astype(F32)
    mid = r1.astype(BF16)
    lo = (r1 - mid.astype(F32)).astype(BF16)
    return hi, mid, lo


def _dot_exact_rhs(x, e):
    hi, mid, lo = _split3(x)
    f = lambda p: jnp.dot(p, e, preferred_element_type=F32)
    return f(hi) + f(mid) + f(lo)


def _dot_exact_lhs(e, x):
    hi, mid, lo = _split3(x)
    f = lambda p: jnp.dot(e, p, preferred_element_type=F32)
    return f(hi) + f(mid) + f(lo)


def _sigmoid(x):
    return 1.0 / (1.0 + jnp.exp(-x))


def _silu(x):
    return x * _sigmoid(x)


def _softplus(x):
    return jnp.maximum(x, 0.0) + jnp.log(1.0 + jnp.exp(-jnp.abs(x)))


def _gelu_tanh(x):
    return 0.5 * x * (1.0 + jnp.tanh(math.sqrt(2.0 / math.pi) * (x + 0.044715 * (x * x * x))))


def _rmsnorm(x, g):
    return (x * lax.rsqrt(jnp.mean(x * x, axis=-1, keepdims=True) + RMS_EPS)) * g


def _layernorm(x, g, b):
    mu = jnp.mean(x, axis=-1, keepdims=True)
    xc = x - mu
    var = jnp.mean(xc * xc, axis=-1, keepdims=True)
    return (xc * lax.rsqrt(var + LN_EPS)) * g + b


MOD_ROWS = 32


def _mod_body(c_ref, w_ref, b_ref, o_ref, s0_ref, s1_ref):
    d, tn = w_ref.shape[1], w_ref.shape[2]

    @pl.when((pl.program_id(0) == 0) & (pl.program_id(1) == 0))
    def _():
        s = _silu(c_ref[...])
        s0_ref[...] = jnp.broadcast_to(s[:, 0:1], (d, LANES))
        s1_ref[...] = jnp.broadcast_to(s[:, 1:2], (d, LANES))

    def step(k, acc):
        r0 = pl.multiple_of(k * MOD_ROWS, MOD_ROWS)
        w = w_ref[0, pl.ds(r0, MOD_ROWS), :]
        s0 = jnp.concatenate([s0_ref[pl.ds(r0, MOD_ROWS), :]] * (tn // LANES), axis=1)
        s1 = jnp.concatenate([s1_ref[pl.ds(r0, MOD_ROWS), :]] * (tn // LANES), axis=1)
        return acc[0] + w * s0, acc[1] + w * s1

    z = jnp.zeros((MOD_ROWS, tn), F32)
    a0, a1 = lax.fori_loop(0, d // MOD_ROWS, step, (z, z), unroll=2)
    bias = b_ref[0]
    o_ref[0] = jnp.concatenate([jnp.sum(a0, axis=0, keepdims=True) + bias, jnp.sum(a1, axis=0, keepdims=True) + bias,
                                jnp.zeros((6, tn), F32)], axis=0)


def modulation(c_cols, mod_w, mod_b):
    depth, d, n = mod_w.shape
    tn = 512
    return pl.pallas_call(
        _mod_body,
        grid=(depth, n // tn),
        in_specs=[pl.BlockSpec((d, 8), lambda l, j: (0, 0)),
                  pl.BlockSpec((1, d, tn), lambda l, j: (l, 0, j)),
                  pl.BlockSpec((1, 1, tn), lambda l, j: (l, 0, j))],
        out_specs=pl.BlockSpec((1, 8, tn), lambda l, j: (l, 0, j)),
        out_shape=jax.ShapeDtypeStruct((depth, 8, n), F32),
        scratch_shapes=[pltpu.VMEM((d, LANES), F32), pltpu.VMEM((d, LANES), F32)],
        compiler_params=_cparams(2, 2 * d * tn * 4 + 5 * d * LANES * 4),
        name="modulation",
    )(c_cols, mod_w, mod_b)


def _normmod_body(x_ref, g_ref, sh_ref, sc_ref, h_ref):
    h = _rmsnorm(x_ref[...], g_ref[...]) * (1.0 + sc_ref[...]) + sh_ref[...]
    h_ref[...] = h.astype(h_ref.dtype)


def normmod(x, g, shift, scale):
    t, d = x.shape
    tm = 256
    row = pl.BlockSpec((1, d), lambda i: (0, 0))
    return pl.pallas_call(
        _normmod_body,
        grid=(t // tm,),
        in_specs=[pl.BlockSpec((tm, d), lambda i: (i, 0)), row, row, row],
        out_specs=pl.BlockSpec((tm, d), lambda i: (i, 0)),
        out_shape=jax.ShapeDtypeStruct((t, d), BF16),
        compiler_params=_cparams(1, 2 * tm * d * 6),
        name="normmod",
    )(x, g, shift, scale)


def _resnorm_body(x_ref, y_ref, gate_ref, gpost_ref, gpre_ref, sh_ref, sc_ref, xo_ref, h_ref):
    xn = x_ref[...] + gate_ref[...] * _rmsnorm(y_ref[...].astype(F32), gpost_ref[...])
    xo_ref[...] = xn
    h = _rmsnorm(xn, gpre_ref[...]) * (1.0 + sc_ref[...]) + sh_ref[...]
    h_ref[...] = h.astype(h_ref.dtype)


def _res_body(x_ref, y_ref, gate_ref, gpost_ref, xo_ref):
    xo_ref[...] = x_ref[...] + gate_ref[...] * _rmsnorm(y_ref[...].astype(F32), gpost_ref[...])


def resnorm(x, y, gate, g_post, nxt=None):
    t, d = x.shape
    tm = 256
    row = pl.BlockSpec((1, d), lambda i: (0, 0))
    tile = pl.BlockSpec((tm, d), lambda i: (i, 0))
    if nxt is None:
        return pl.pallas_call(
            _res_body, grid=(t // tm,), in_specs=[tile, tile, row, row], out_specs=tile,
            out_shape=jax.ShapeDtypeStruct((t, d), F32),
            compiler_params=_cparams(1, 2 * tm * d * 12), name="residual",
        )(x, y, gate, g_post)
    return pl.pallas_call(
        _resnorm_body, grid=(t // tm,), in_specs=[tile, tile, row, row, row, row, row],
        out_specs=[tile, tile],
        out_shape=[jax.ShapeDtypeStruct((t, d), F32), jax.ShapeDtypeStruct((t, d), BF16)],
        compiler_params=_cparams(1, 2 * tm * d * 14), name="residual_norm",
    )(x, y, gate, g_post, *nxt)


def _mm_body(x_ref, w_ref, o_ref):
    o_ref[...] = jnp.dot(x_ref[...], w_ref[...], preferred_element_type=F32).astype(o_ref.dtype)


def _mm_cast_body(x_ref, w_ref, o_ref, wb_ref):
    @pl.when(pl.program_id(1) == 0)
    def _():
        wb_ref[...] = w_ref[...].astype(BF16)

    o_ref[...] = jnp.dot(x_ref[...], wb_ref[...], preferred_element_type=F32).astype(o_ref.dtype)


def _wspec(w, layer, rows, tn, col_block):
    if w.ndim == 3:
        return pl.BlockSpec((None, rows, tn), lambda j, i: (layer, 0, col_block(j)))
    return pl.BlockSpec((rows, tn), lambda j, i: (0, col_block(j)))


def matmul(x, w, n, col0, tn, out_dtype, name, layer=None):
    m, k = x.shape
    tm = min(m, 1024 if k <= 4096 else 512)
    j0 = col0 // tn
    assert col0 % tn == 0 and n % tn == 0 and m % tm == 0
    cast = w.dtype == F32
    vm = 2 * (tm * k * 2 + k * tn * w.dtype.itemsize + tm * tn * 4) + (k * tn * 2 if cast else 0)
    return pl.pallas_call(
        _mm_cast_body if cast else _mm_body,
        grid=(n // tn, m // tm),
        in_specs=[pl.BlockSpec((tm, k), lambda j, i: (i, 0)),
                  _wspec(w, layer, k, tn, lambda j: j + j0)],
        out_specs=pl.BlockSpec((tm, tn), lambda j, i: (i, j)),
        out_shape=jax.ShapeDtypeStruct((m, n), out_dtype),
        scratch_shapes=[pltpu.VMEM((k, tn), BF16)] if cast else [],
        compiler_params=_cparams(2, vm),
        name=name,
    )(x, w)


def _ffn_up_body(x_ref, w1_ref, w3_ref, o_ref, w1b_ref, w3b_ref):
    @pl.when(pl.program_id(1) == 0)
    def _():
        w1b_ref[...] = w1_ref[...].astype(BF16)
        w3b_ref[...] = w3_ref[...].astype(BF16)

    x = x_ref[...]
    a = jnp.dot(x, w1b_ref[...], preferred_element_type=F32)
    b = jnp.dot(x, w3b_ref[...], preferred_element_type=F32)
    o_ref[...] = (_silu(a) * b).astype(o_ref.dtype)


def ffn_up(h, w1, w3, layer):
    m, k = h.shape
    n = w1.shape[-1]
    tm, tn = min(m, 1024), 256
    vm = 2 * (tm * k * 2 + 2 * k * tn * 4 + tm * tn * 2) + 2 * k * tn * 2 + 3 * tm * tn * 4
    return pl.pallas_call(
        _ffn_up_body,
        grid=(n // tn, m // tm),
        in_specs=[pl.BlockSpec((tm, k), lambda j, i: (i, 0)),
                  _wspec(w1, layer, k, tn, lambda j: j),
                  _wspec(w3, layer, k, tn, lambda j: j)],
        out_specs=pl.BlockSpec((tm, tn), lambda j, i: (i, j)),
        out_shape=jax.ShapeDtypeStruct((m, n), BF16),
        scratch_shapes=[pltpu.VMEM((k, tn), BF16), pltpu.VMEM((k, tn), BF16)],
        compiler_params=_cparams(2, vm),
        name="ffn_up",
    )(h, w1, w3)


def _merge_body(h_ref, za_ref, zb_ref, zc_ref, zd_ref,
                ga_ref, gb_ref, gc_ref, gd_ref, ba_ref, bb_ref, bc_ref, bd_ref,
                wa_ref, wb_ref, wc_ref, wd_ref, o_ref, gbf_ref, wbf_ref):
    branches = ((za_ref, ga_ref, ba_ref, wa_ref), (zb_ref, gb_ref, bb_ref, wb_ref),
                (zc_ref, gc_ref, bc_ref, wc_ref), (zd_ref, gd_ref, bd_ref, wd_ref))

    @pl.when(pl.program_id(1) == 0)
    def _():
        for br, (_, g_ref, _, w_ref) in enumerate(branches):
            gbf_ref[br] = g_ref[...].astype(BF16)
            wbf_ref[br] = w_ref[...].astype(BF16)

    h = h_ref[...]
    acc = None
    for br, (z_ref, _, b_ref, _) in enumerate(branches):
        gate = _sigmoid(jnp.dot(h, gbf_ref[br], preferred_element_type=F32) + b_ref[...])
        y = jnp.dot(z_ref[...], wbf_ref[br], preferred_element_type=F32)
        acc = gate * y if acc is None else acc + gate * y
    o_ref[...] = acc.astype(o_ref.dtype)


def merge(h, zs, gate_w, gate_b, w_outs, layer):
    m, d = h.shape
    bw = zs[0].shape[1]
    tm, tn = min(m, 512), 256
    nj = d // tn
    once = pl.Buffered(1)
    hspec = pl.BlockSpec((tm, d), lambda j, i: (i, 0))
    zspec = pl.BlockSpec((tm, bw), lambda j, i: (i, 0))
    gspecs = [pl.BlockSpec((None, d, tn), functools.partial(lambda j, i, br: (layer, 0, br * nj + j), br=br),
                           pipeline_mode=once) for br in range(4)]
    bspecs = [pl.BlockSpec((1, tn), functools.partial(lambda j, i, br: (0, br * nj + j), br=br)) for br in range(4)]
    wspec = pl.BlockSpec((None, bw, tn), lambda j, i: (layer, 0, j), pipeline_mode=once)
    vm = (2 * (tm * d * 2 + 4 * tm * bw * 2 + tm * tn * 2) + 4 * (d + bw) * tn * (4 + 2) + 4 * tm * tn * 4)
    return pl.pallas_call(
        _merge_body,
        grid=(nj, m // tm),
        in_specs=[hspec] + [zspec] * 4 + gspecs + bspecs + [wspec] * 4,
        out_specs=pl.BlockSpec((tm, tn), lambda j, i: (i, j)),
        out_shape=jax.ShapeDtypeStruct((m, d), BF16),
        scratch_shapes=[pltpu.VMEM((4, d, tn), BF16), pltpu.VMEM((4, bw, tn), BF16)],
        compiler_params=_cparams(2, vm),
        name="merge",
    )(h, *zs, gate_w, gate_w, gate_w, gate_w, gate_b, gate_b, gate_b, gate_b, *w_outs)


def _token_shift(x, mu, row_len):
    tm = x.shape[0]
    pos = lax.broadcasted_iota(jnp.int32, x.shape, 0) & (row_len - 1)
    prev = jnp.where(pos == 0, 0.0, pltpu.roll(x, 1, 0))
    nxt = jnp.where(pos == row_len - 1, 0.0, pltpu.roll(x, tm - 1, 0))
    return x + mu * (0.5 * (prev + nxt) - x)


def _aprep_body(r_ref, k_ref, v_ref, wa_ref, g1_ref, g2_ref,
                mur_ref, muk_ref, muv_ref, muwa_ref, mug1_ref, mug2_ref,
                w0_ref, wup_ref, a0_ref, aup_ref, gup_ref, kk_ref, ka_ref, rk_ref, e_ref,
                ro_ref, vo_ref, kko_ref, lw_ref, kd_ref, b_ref, bonus_ref, g_ref, *, row_len):
    e = e_ref[...]
    xr = _token_shift(r_ref[...], mur_ref[...], row_len)
    xk = _token_shift(k_ref[...], muk_ref[...], row_len)
    xv = _token_shift(v_ref[...], muv_ref[...], row_len)
    xwa = _token_shift(wa_ref[...], muwa_ref[...], row_len)
    xg1 = _token_shift(g1_ref[...], mug1_ref[...], row_len)
    xg2 = _token_shift(g2_ref[...], mug2_ref[...], row_len)
    kk = xk * kk_ref[...]
    kk = kk * lax.rsqrt(jnp.maximum(_dot_exact_rhs(kk * kk, e), 1e-12))
    tw = jnp.tanh(xwa)
    bonus = None
    for d in range(2):
        w = -_softplus(-(w0_ref[d] + _dot(tw, wup_ref[d]))) - 0.5
        a = _sigmoid(a0_ref[d] + _dot(xwa, aup_ref[d]))
        kd = xk * (1.0 + (a - 1.0) * ka_ref[...])
        lw_ref[d] = -jnp.exp(w)
        kd_ref[d] = kd
        b_ref[d] = kk * a
        bn = _dot_exact_rhs(xr * kd * rk_ref[...], e) * xv
        bonus = bn if bonus is None else bonus + bn
    ro_ref[...] = xr
    vo_ref[...] = xv
    kko_ref[...] = kk
    bonus_ref[...] = bonus
    g_ref[...] = _dot(_sigmoid(xg1), gup_ref[0]) + _dot(_sigmoid(xg2), gup_ref[1])


N_PREP_IN = 21


def _rwkv_maps_body(*refs, row_len, nchunk):
    prep_in, (tri_ref, msk_ref) = refs[:N_PREP_IN], refs[N_PREP_IN:N_PREP_IN + 2]
    rhat_ref, y0_ref, m_ref, n_ref, bonus_ref, g_ref = refs[N_PREP_IN + 2:N_PREP_IN + 8]
    r_s, v_s, kk_s, lw_s, kd_s, b_s = refs[N_PREP_IN + 8:]
    _aprep_body(*prep_in, r_s, v_s, kk_s, lw_s, kd_s, b_s, bonus_ref, g_ref, row_len=row_len)
    _scan_pre_body(r_s, v_s, kk_s, lw_s, kd_s, b_s, tri_ref, msk_ref, rhat_ref, y0_ref, m_ref, n_ref, nchunk=nchunk)


def rwkv_maps(pa, ap, row_len, tri, msk):
    t = pa.shape[0]
    nchunk = 4
    tm = nchunk * CHUNK
    nc = t // CHUNK
    nb = BRANCH_W // LANES
    col = lambda c: pl.BlockSpec((tm, LANES), functools.partial(lambda i, p, c: (i, c + p), c=c))
    fix = lambda c: pl.BlockSpec((tm, LANES), functools.partial(lambda i, p, c: (i, c), c=c))
    mcol = lambda c: pl.BlockSpec((1, LANES), functools.partial(lambda i, p, c: (0, c + p), c=c))
    mfix = lambda c: pl.BlockSpec((1, LANES), functools.partial(lambda i, p, c: (0, c), c=c))
    prow = pl.BlockSpec((1, LANES), lambda i, p: (0, p))
    p2 = pl.BlockSpec((2, 1, LANES), lambda i, p: (0, 0, p))
    up2 = pl.BlockSpec((2, LANES, LANES), lambda i, p: (0, 0, p))
    out1 = pl.BlockSpec((tm, LANES), lambda i, p: (i, p))
    out2 = pl.BlockSpec((2, tm, LANES), lambda i, p: (0, i, p))
    mspec = pl.BlockSpec((2, nchunk, 1, LANES, LANES), lambda i, p: (0, i, p, 0, 0))
    s1 = jax.ShapeDtypeStruct((t, BRANCH_W), F32)
    s2 = jax.ShapeDtypeStruct((2, t, BRANCH_W), F32)
    sm = jax.ShapeDtypeStruct((2, nc, PAIRS, LANES, LANES), F32)
    lora, gc = A_LORA // LANES, A_GC // LANES
    tile = lambda lead=(): pltpu.VMEM(lead + (tm, LANES), F32)
    return pl.pallas_call(
        functools.partial(_rwkv_maps_body, row_len=row_len, nchunk=nchunk),
        grid=(t // tm, nb),
        in_specs=[col(0), col(nb), col(2 * nb), fix(lora), fix(gc), fix(gc + 1),
                  mcol(0), mcol(nb), mcol(2 * nb), mfix(lora), mfix(gc), mfix(gc + 1),
                  p2, up2, p2, up2, up2, prow, prow, prow,
                  pl.BlockSpec((LANES, LANES), lambda i, p: (0, 0)),
                  pl.BlockSpec((2, CHUNK, CHUNK), lambda i, p: (0, 0, 0)),
                  pl.BlockSpec(msk.shape, lambda i, p: (0, 0, 0, 0))],
        out_specs=[out2, out2, mspec, mspec, out1, out1],
        out_shape=[s2, s2, sm, sm, s1, s1],
        scratch_shapes=[tile(), tile(), tile(), tile((2,)), tile((2,)), tile((2,))],
        compiler_params=_cparams(2, 24 << 20),
        name="rwkv_chunk_maps",
    )(pa, pa, pa, pa, pa, pa, ap["mu"], ap["mu"], ap["mu"], ap["mu"], ap["mu"], ap["mu"],
      ap["w0"], ap["w_up"], ap["a0"], ap["a_up"], ap["g_up"], ap["k_k"], ap["k_a"], ap["r_k"], ap["e2"], tri, msk)


def _pair_rows(x):
    lane = lax.broadcasted_iota(jnp.int32, x.shape, 1)
    return jnp.concatenate([jnp.where(lane < HEAD, x, 0.0), jnp.where(lane >= HEAD, x, 0.0)], axis=0)


MSK_STRICT, MSK_INCL, MSK_EYE, MSK_DIAG8, MSK_OFF8, MSK_OFF16, MSK_OFF32 = range(7)


def _chunk_mask_tables():
    n = 2 * CHUNK
    r = np.arange(n)[:, None]
    c = np.arange(n)[None, :]
    same = (r // CHUNK) == (c // CHUNK)
    out = np.zeros((2, 7, n, n), np.float32)
    for d in range(2):
        before = (c > r) if d else (c < r)
        out[d, MSK_STRICT] = same & before
        out[d, MSK_INCL] = same & (before | (r == c))
        out[d, MSK_EYE] = r == c
        out[d, MSK_DIAG8] = ((r // 8) == (c // 8)) & before
        for idx, s in ((MSK_OFF8, 8), (MSK_OFF16, 16), (MSK_OFF32, 32)):
            blk = (r // (2 * s)) == (c // (2 * s))
            rh, ch = (r // s) % 2, (c // s) % 2
            out[d, idx] = blk & ((rh == 0) & (ch == 1) if d else (rh == 1) & (ch == 0))
    return jnp.asarray(out)


def _scan_pre_body(r_ref, v_ref, kk_ref, lw_ref, kd_ref, b_ref, tri_ref, msk_ref, rhat_ref, y0_ref, m_ref, n_ref, *, nchunk):
    c, n = CHUNK, 2 * CHUNK
    chains = [(d, ci) for d in range(2) for ci in range(nchunk)]
    each = lambda f, *cols: [f(*xs) for xs in zip(*cols)]
    msk = lambda d, k: msk_ref[d, k]
    rows = lambda ci: slice(ci * c, (ci + 1) * c)

    lw = [lw_ref[d, rows(ci), :] for d, ci in chains]
    cum = [_dot_exact_lhs(tri_ref[d], x) for (d, _), x in zip(chains, lw)]
    tot = [x[0:1] if d else x[c - 1:c] for (d, _), x in zip(chains, cum)]
    g_inv = each(lambda x: jnp.exp(-x), cum)
    g_tail = each(lambda t, x: jnp.exp(t - x), tot, cum)
    atp = [_pair_rows(-kk_ref[rows(ci), :] * jnp.exp(x - l)) for (_, ci), x, l in zip(chains, cum, lw)]
    rtp = [_pair_rows(r_ref[rows(ci), :] * jnp.exp(x)) for (_, ci), x in zip(chains, cum)]
    btp = [_pair_rows(b_ref[d, rows(ci), :] * g) for (d, ci), g in zip(chains, g_inv)]
    ktp = [_pair_rows(kd_ref[d, rows(ci), :] * g) for (d, ci), g in zip(chains, g_inv)]
    bhp = [_pair_rows(b_ref[d, rows(ci), :] * g) for (d, ci), g in zip(chains, g_tail)]
    khp = [_pair_rows(kd_ref[d, rows(ci), :] * g) for (d, ci), g in zip(chains, g_tail)]
    vp = [_pair_rows(v_ref[rows(ci), :]) for _, ci in chains]

    big = each(lambda a, r, b, k: _dot_nt(jnp.concatenate([a, r], axis=0), jnp.concatenate([b, k], axis=0)),
               atp, rtp, btp, ktp)
    a_ab = [jnp.where(msk(d, MSK_STRICT) > 0.0, x[:n, :n], 0.0) for (d, _), x in zip(chains, big)]
    a_ak = [jnp.where(msk(d, MSK_STRICT) > 0.0, x[:n, n:], 0.0) for (d, _), x in zip(chains, big)]
    a_rb = [jnp.where(msk(d, MSK_INCL) > 0.0, x[n:, :n], 0.0) for (d, _), x in zip(chains, big)]
    a_rk = [jnp.where(msk(d, MSK_INCL) > 0.0, x[n:, n:], 0.0) for (d, _), x in zip(chains, big)]

    n8 = [x * msk(d, MSK_DIAG8) for (d, _), x in zip(chains, a_ab)]
    t = [msk(d, MSK_EYE) + x for (d, _), x in zip(chains, n8)]
    n2 = each(_dot, n8, n8)
    t = each(lambda x, p: x + _dot(p, x), t, n2)
    n4 = each(_dot, n2, n2)
    t = each(lambda x, p: x + _dot(p, x), t, n4)
    for idx in (MSK_OFF8, MSK_OFF16, MSK_OFF32):
        off = [x * msk(d, idx) for (d, _), x in zip(chains, a_ab)]
        tn = each(_dot, t, off)
        t = each(lambda x, p: x + _dot(p, x), t, tn)

    akv = each(_dot, a_ak, vp)
    w_u = each(lambda ti, a, x: _dot(ti, jnp.concatenate([a, x], axis=1)), t, atp, akv)
    ry = each(_dot, a_rb, w_u)
    rkv = each(_dot, a_rk, vp)
    mn = each(_dot_tn, bhp, w_u)
    kv = each(_dot_tn, khp, vp)
    for i, (d, ci) in enumerate(chains):
        rhat_p = rtp[i] + ry[i][:, :n]
        y0_p = ry[i][:, n:] + rkv[i]
        rhat_ref[d, rows(ci), :] = rhat_p[:c] + rhat_p[c:]
        y0_ref[d, rows(ci), :] = y0_p[:c] + y0_p[c:]
        m_ref[d, ci, 0] = mn[i][:, :n] + msk(d, MSK_EYE) * jnp.exp(tot[i])
        n_ref[d, ci, 0] = mn[i][:, n:] + kv[i]


def _scan_seq_body(s0_ref, mf_ref, mb_ref, nf_ref, nb_ref, rf_ref, rb_ref, yf0_ref, yb0_ref,
                   yf_ref, yb_ref, sfin_ref, s_ref):
    c = pl.program_id(0)

    @pl.when(c == 0)
    def _():
        s_ref[...] = s0_ref[...]

    for d, (m_ref, n_ref, rh_ref, y0_ref, y_ref) in enumerate(((mf_ref, nf_ref, rf_ref, yf0_ref, yf_ref),
                                                                 (mb_ref, nb_ref, rb_ref, yb0_ref, yb_ref))):
        for p in range(PAIRS):
            ln = slice(p * LANES, (p + 1) * LANES)
            s = s_ref[d, p]
            y_ref[:, ln] = _dot(rh_ref[0, :, ln], s) + y0_ref[0, :, ln]
            s_ref[d, p] = _dot(m_ref[0, 0, p], s) + n_ref[0, 0, p]

    @pl.when(c == pl.num_programs(0) - 1)
    def _():
        sfin_ref[...] = s_ref[...]


def scan_seq(s0, rhat, y0, m, n):
    t = rhat.shape[1]
    nc = t // CHUNK
    fwd5 = lambda c: (0, c, 0, 0, 0)
    bwd5 = lambda c: (1, nc - 1 - c, 0, 0, 0)
    mblk = (1, 1, PAIRS, LANES, LANES)
    rblk = (1, CHUNK, BRANCH_W)
    sblk = pl.BlockSpec((2, PAIRS, LANES, LANES), lambda c: (0, 0, 0, 0))
    ys = jax.ShapeDtypeStruct((t, BRANCH_W), F32)
    return pl.pallas_call(
        _scan_seq_body,
        grid=(nc,),
        in_specs=[sblk,
                  pl.BlockSpec(mblk, fwd5), pl.BlockSpec(mblk, bwd5),
                  pl.BlockSpec(mblk, fwd5), pl.BlockSpec(mblk, bwd5),
                  pl.BlockSpec(rblk, lambda c: (0, c, 0)), pl.BlockSpec(rblk, lambda c: (1, nc - 1 - c, 0)),
                  pl.BlockSpec(rblk, lambda c: (0, c, 0)), pl.BlockSpec(rblk, lambda c: (1, nc - 1 - c, 0))],
        out_specs=[pl.BlockSpec((CHUNK, BRANCH_W), lambda c: (c, 0)),
                   pl.BlockSpec((CHUNK, BRANCH_W), lambda c: (nc - 1 - c, 0)),
                   sblk],
        out_shape=[ys, ys, jax.ShapeDtypeStruct((2, PAIRS, LANES, LANES), F32)],
        scratch_shapes=[pltpu.VMEM((2, PAIRS, LANES, LANES), F32)],
        compiler_params=_cparams(1, 16 << 20),
        name="rwkv_chain",
    )(s0, m, m, n, n, rhat, rhat, y0, y0)


def _areadout_body(yf_ref, yb_ref, bonus_ref, g_ref, lng_ref, lnb_ref, e_ref, o_ref):
    e = e_ref[...]
    y = yf_ref[...] + yb_ref[...]
    mu = _dot_exact_rhs(y, e) * (1.0 / HEAD)
    yc = y - mu
    var = _dot_exact_rhs(yc * yc, e) * (1.0 / HEAD)
    yn = (yc * lax.rsqrt(var + A_GN_EPS)) * lng_ref[...] + lnb_ref[...]
    o_ref[...] = ((yn + bonus_ref[...]) * g_ref[...]).astype(o_ref.dtype)


def a_readout(yf, yb, bonus, g, ap):
    t = yf.shape[0]
    tm = 256
    blk = pl.BlockSpec((tm, LANES), lambda i, p: (i, p))
    prow = pl.BlockSpec((1, LANES), lambda i, p: (0, p))
    return pl.pallas_call(
        _areadout_body,
        grid=(t // tm, BRANCH_W // LANES),
        in_specs=[blk, blk, blk, blk, prow, prow, pl.BlockSpec((LANES, LANES), lambda i, p: (0, 0))],
        out_specs=blk,
        out_shape=jax.ShapeDtypeStruct((t, BRANCH_W), BF16),
        compiler_params=_cparams(2, 16 * tm * LANES * 4),
        name="rwkv_readout",
    )(yf, yb, bonus, g, ap["ln_g"], ap["ln_b"], ap["e2"])


def _bmix_body(u_ref, v_ref, lng_ref, lnb_ref, ws_ref, bs_ref, o_ref):
    u = _gelu_tanh(u_ref[...].astype(F32))
    v = _layernorm(_gelu_tanh(v_ref[...].astype(F32)), lng_ref[...], lnb_ref[...]).astype(BF16)
    for ci in range(u.shape[0] // B_CHUNK):
        rows = slice(ci * B_CHUNK, (ci + 1) * B_CHUNK)
        for g in range(B_GROUPS):
            ln = slice(g * LANES, (g + 1) * LANES)
            s = jnp.dot(ws_ref[g], v[rows, ln], preferred_element_type=F32) + bs_ref[:, ln]
            o_ref[rows, ln] = (u[rows, ln] * s).astype(o_ref.dtype)


def b_mix(pb, bp):
    t = pb.shape[0]
    tm = 256
    row = pl.BlockSpec((1, BRANCH_W), lambda i: (0, 0))
    return pl.pallas_call(
        _bmix_body,
        grid=(t // tm,),
        in_specs=[pl.BlockSpec((tm, BRANCH_W), lambda i: (i, 0)), pl.BlockSpec((tm, BRANCH_W), lambda i: (i, 1)),
                  row, row, pl.BlockSpec((B_GROUPS, B_CHUNK, B_CHUNK), lambda i: (0, 0, 0)),
                  pl.BlockSpec((B_CHUNK, BRANCH_W), lambda i: (0, 0))],
        out_specs=pl.BlockSpec((tm, BRANCH_W), lambda i: (i, 0)),
        out_shape=jax.ShapeDtypeStruct((t, BRANCH_W), BF16),
        compiler_params=_cparams(1, 12 * tm * BRANCH_W * 4),
        name="gmlp_mix",
    )(pb, pb, bp["ln_g"], bp["ln_b"], bp["ws"], bp["bs"])


def _conv_body(ac_ref, gc_ref, ap_ref, gp_ref, an_ref, gn_ref, dw_ref, dwb_ref, lng_ref, lnb_ref,
               o_ref, zs_ref, cs_ref, *, tm):
    i = pl.program_id(0)
    last = pl.num_programs(0) - 1
    glu = lambda a_ref, g_ref: a_ref[...].astype(F32) * _sigmoid(g_ref[...].astype(F32))
    zs_ref[0:CONV_HALO, :] = jnp.where(i == 0, 0.0, glu(ap_ref, gp_ref))
    zs_ref[CONV_HALO:CONV_HALO + tm, :] = glu(ac_ref, gc_ref)
    zs_ref[CONV_HALO + tm:, :] = jnp.where(i == last, 0.0, glu(an_ref, gn_ref))
    rb, lb = 32, 256
    win = rb + 2 * CONV_HALO
    first = CONV_HALO - C_KERNEL // 2
    for r0 in range(0, tm, rb):
        for l0 in range(0, BRANCH_W, lb):
            w = zs_ref[r0:r0 + win, l0:l0 + lb]
            acc = jnp.zeros((rb, lb), F32) + dwb_ref[:, l0:l0 + lb]
            for s in range(SUBLANES):
                ws = w if s == 0 else pltpu.roll(w, win - s, 0)
                for j in range(C_KERNEL):
                    if (first + j) % SUBLANES == s:
                        a = (first + j) - s
                        acc = acc + ws[a:a + rb] * dw_ref[j:j + 1, l0:l0 + lb]
            cs_ref[r0:r0 + rb, l0:l0 + lb] = acc
    o_ref[...] = _silu(_layernorm(cs_ref[...], lng_ref[...], lnb_ref[...])).astype(o_ref.dtype)


def conv_mix(pc, cp):
    t = pc.shape[0]
    tm = 256
    hb = tm // CONV_HALO
    nh = t // CONV_HALO
    cur = lambda c: pl.BlockSpec((tm, BRANCH_W), functools.partial(lambda i, c: (i, c), c=c))
    prev = lambda c: pl.BlockSpec((CONV_HALO, BRANCH_W), functools.partial(lambda i, c: (jnp.maximum(i * hb - 1, 0), c), c=c))
    nxt = lambda c: pl.BlockSpec((CONV_HALO, BRANCH_W), functools.partial(lambda i, c: (jnp.minimum((i + 1) * hb, nh - 1), c), c=c))
    row = pl.BlockSpec((1, BRANCH_W), lambda i: (0, 0))
    return pl.pallas_call(
        functools.partial(_conv_body, tm=tm),
        grid=(t // tm,),
        in_specs=[cur(0), cur(1), prev(0), prev(1), nxt(0), nxt(1),
                  pl.BlockSpec((C_KERNEL + 1, BRANCH_W), lambda i: (0, 0)), row, row, row],
        out_specs=pl.BlockSpec((tm, BRANCH_W), lambda i: (i, 0)),
        out_shape=jax.ShapeDtypeStruct((t, BRANCH_W), BF16),
        scratch_shapes=[pltpu.VMEM((tm + 2 * CONV_HALO, BRANCH_W), F32), pltpu.VMEM((tm, BRANCH_W), F32)],
        compiler_params=_cparams(1, 16 * tm * BRANCH_W * 4),
        name="conv_mix",
    )(pc, pc, pc, pc, pc, pc, cp["dw"], cp["dw_b"], cp["ln_g"], cp["ln_b"])


FFT_N1, FFT_N2 = 64, 128


def _dft_tables(t):
    two_pi = 2.0 * np.pi
    cidx = np.arange(D_GROUP_CH)
    ph = two_pi * np.outer(cidx, cidx) / D_GROUP_CH
    chan = np.concatenate([np.cos(ph), np.sin(ph)], axis=0)
    if t <= 256:
        n = np.arange(t)
        th = two_pi * np.outer(n, n) / t
        m2 = np.concatenate([np.cos(th), -np.sin(th)], axis=0)
        return None, m2.astype(np.float32), chan.astype(np.float32)
    n1, n2 = FFT_N1, FFT_N2
    assert t == n1 * n2
    k1 = np.arange(n1)
    tok = (n2 * np.arange(n1))[None, None, :] + np.arange(n2)[:, None, None]
    th = two_pi * (k1[None, :, None] * tok) / t
    g1 = np.concatenate([np.cos(th), -np.sin(th)], axis=1)
    q = np.arange(n2)
    th2 = two_pi * np.outer(q, q) / n2
    c2, s2 = np.cos(th2), np.sin(th2)
    m2 = np.block([[c2, s2], [-s2, c2]])
    return g1.astype(np.float32), m2.astype(np.float32), chan.astype(np.float32)


def _fft1_body(x_ref, g_ref, o_ref):
    o_ref[...] = _dot(g_ref[0], x_ref[...])


def _fft2_body(z_ref, m2_ref, ch_ref, o_ref, *, n_out, scale, stacked):
    z = jnp.concatenate([z_ref[0], z_ref[1]], axis=0) if stacked else z_ref[...]
    x = _dot(m2_ref[...], z)
    xr, xi = x[:n_out], x[n_out:]
    for g in range(D_GROUPS):
        ln = slice(g * D_GROUP_CH, (g + 1) * D_GROUP_CH)
        f = _dot(xr[:, ln], ch_ref[0:D_GROUP_CH]) + _dot(xi[:, ln], ch_ref[D_GROUP_CH:])
        o_ref[:, ln] = (f * scale).astype(o_ref.dtype)


def fourier_mix(pd):
    t = pd.shape[0]
    g1, m2, chan = (None if a is None else jnp.asarray(a) for a in _dft_tables(t))
    scale = 1.0 / math.sqrt(t * D_GROUP_CH)
    chspec = pl.BlockSpec((2 * D_GROUP_CH, D_GROUP_CH), lambda i: (0, 0))
    if g1 is None:
        return pl.pallas_call(
            functools.partial(_fft2_body, n_out=t, scale=scale, stacked=False),
            grid=(1,),
            in_specs=[pl.BlockSpec((t, BRANCH_W), lambda i: (0, 0)), pl.BlockSpec((2 * t, t), lambda i: (0, 0)), chspec],
            out_specs=pl.BlockSpec((t, BRANCH_W), lambda i: (0, 0)),
            out_shape=jax.ShapeDtypeStruct((t, BRANCH_W), BF16),
            compiler_params=_cparams(1, 16 << 20),
            name="fourier_small",
        )(pd, m2, chan)
    n1, n2 = FFT_N1, FFT_N2
    z = pl.pallas_call(
        _fft1_body,
        grid=(n2,),
        in_specs=[pl.BlockSpec((n1, BRANCH_W), lambda q: (0, q)), pl.BlockSpec((1, 2 * n1, n1), lambda q: (q, 0, 0))],
        out_specs=pl.BlockSpec((2 * n1, BRANCH_W), lambda q: (0, q)),
        out_shape=jax.ShapeDtypeStruct((2 * n1, n2 * BRANCH_W), F32),
        compiler_params=_cparams(1, 8 << 20),
        name="fourier_stage1",
    )(pd.reshape(n1, n2 * BRANCH_W), g1)
    f = pl.pallas_call(
        functools.partial(_fft2_body, n_out=n2, scale=scale, stacked=True),
        grid=(n1,),
        in_specs=[pl.BlockSpec((2, None, n2, BRANCH_W), lambda k: (0, k, 0, 0)),
                  pl.BlockSpec((2 * n2, 2 * n2), lambda k: (0, 0)), chspec],
        out_specs=pl.BlockSpec((n2, BRANCH_W), lambda k: (0, k)),
        out_shape=jax.ShapeDtypeStruct((n2, n1 * BRANCH_W), BF16),
        compiler_params=_cparams(1, 16 << 20),
        name="fourier_stage2",
    )(z.reshape(2, n1, n2, BRANCH_W), m2, chan)
    return f.reshape(t, BRANCH_W)


def _align_a(x):
    return jnp.pad(x, [(0, 0)] * (x.ndim - 1) + [(0, A_PAD - x.shape[-1])])


def _pad_rows(x, before, rows):
    return jnp.pad(x, [(0, 0)] * (x.ndim - 2) + [(before, rows - before - x.shape[-2]), (0, 0)])


def _layer_params(l, w_in, a_mu, a_w0, a_w_up, a_a0, a_a_up, a_g_up, a_k_k, a_k_a, a_r_k, a_ln, a_w_out,
                  b_ln, b_ws, b_bs, b_w_out, c_dw, c_dw_b, c_ln, c_w_out, d_w_out, gate_w, gate_b, w_o,
                  ffn_w1, ffn_w3, ffn_w2):
    row = lambda v: v.reshape(1, -1)
    hid = np.arange(LANES) // HEAD
    e2 = jnp.asarray((hid[:, None] == hid[None, :]).astype(np.float32)).astype(BF16)
    g_up = jnp.stack([a_g_up[l][:LANES], _pad_rows(a_g_up[l][LANES:], 0, LANES)])
    ap = dict(mu=_align_a(row(a_mu[l])), w0=a_w0[l][:, None, :], w_up=_pad_rows(a_w_up[l], 0, LANES),
              a0=a_a0[l][:, None, :], a_up=_pad_rows(a_a_up[l], LORA_W, LANES), g_up=g_up,
              k_k=row(a_k_k[l]), k_a=row(a_k_a[l]), r_k=row(a_r_k[l]), ln_g=row(a_ln[l][0]), ln_b=row(a_ln[l][1]), e2=e2)
    bs_exp = jnp.repeat(jnp.swapaxes(b_bs[l], 0, 1), LANES, axis=1)
    return dict(
        layer=l, w_in=w_in, w_bcd=w_in[l][:, A_PROJ:], ap=ap,
        bp=dict(ln_g=row(b_ln[l][0]), ln_b=row(b_ln[l][1]), ws=b_ws[l].astype(BF16), bs=bs_exp),
        cp=dict(dw=_pad_rows(c_dw[l], 0, C_KERNEL + 1), dw_b=row(c_dw_b[l]), ln_g=row(c_ln[l][0]), ln_b=row(c_ln[l][1])),
        w_outs=[a_w_out, b_w_out, c_w_out, d_w_out],
        gate_w=gate_w, gate_b=row(gate_b[l]), w_o=w_o,
        w1=ffn_w1, w3=ffn_w3, w2=ffn_w2)


def _tri_tables():
    i = np.arange(CHUNK)
    lower = (i[None, :] <= i[:, None]).astype(np.float32)
    return jnp.asarray(np.stack([lower, lower.T])).astype(BF16)


def _rwkv_scan(pa, s0, ap, row_len):
    rhat, y0, m, n, bonus, g = rwkv_maps(pa, ap, row_len, _tri_tables(), _chunk_mask_tables())
    yf, yb, s_fin = scan_seq(s0, rhat, y0, m, n)
    return yf, yb, bonus, g, s_fin


def _in_proj_a(h, lp):
    return matmul(h, lp["w_in"], A_PAD, 0, 512, F32, "in_proj_a", layer=lp["layer"])


def _token_mix(h, s0, row_len, lp):
    tn = 512
    pa = _in_proj_a(h, lp)
    yf, yb, bonus, g, s_fin = _rwkv_scan(pa, s0, lp["ap"], row_len)
    za = a_readout(yf, yb, bonus, g, lp["ap"])
    pb = matmul(h, lp["w_bcd"], 2 * BRANCH_W, 0, tn, BF16, "in_proj_b")
    zb = b_mix(pb, lp["bp"])
    pc = matmul(h, lp["w_bcd"], 2 * BRANCH_W, 2 * BRANCH_W, tn, BF16, "in_proj_c")
    zc = conv_mix(pc, lp["cp"])
    pd = matmul(h, lp["w_bcd"], BRANCH_W, 4 * BRANCH_W, tn, BF16, "in_proj_d")
    zd = fourier_mix(pd)
    merged = merge(h, [za, zb, zc, zd], lp["gate_w"], lp["gate_b"], lp["w_outs"], lp["layer"])
    mix = matmul(merged, lp["w_o"], D_MODEL, 0, tn, BF16, "out_proj", layer=lp["layer"])
    return mix, s_fin


def _ffn(h2, lp):
    u = ffn_up(h2, lp["w1"], lp["w3"], lp["layer"])
    return matmul(u, lp["w2"], D_MODEL, 0, 512, BF16, "ffn_down", layer=lp["layer"])


def kernel(x, c, ctx, c_ctx, mod_w, mod_b, norm_g, w_in, a_mu, a_w0, a_w_up, a_a0, a_a_up, a_g_up, a_k_k, a_k_a,
           a_r_k, a_ln, a_w_out, b_ln, b_ws, b_bs, b_w_out, c_dw, c_dw_b, c_ln, c_w_out, d_w_out, gate_w, gate_b,
           w_o, ffn_w1, ffn_w3, ffn_w2):
    depth = mod_w.shape[0]
    d = D_MODEL
    c_cols = jnp.concatenate([c.reshape(d, 1), c_ctx.reshape(d, 1), jnp.zeros((d, 6), F32)], axis=1)
    mods = modulation(c_cols, mod_w, mod_b.reshape(depth, 1, 6 * d))
    x_lat, x_ctx = x[0], ctx[0]
    zero_state = jnp.zeros((2, PAIRS, LANES, LANES), F32)
    weights = (w_in.astype(BF16), a_mu, a_w0, a_w_up, a_a0, a_a_up, a_g_up, a_k_k, a_k_a, a_r_k, a_ln, a_w_out, b_ln, b_ws, b_bs,
               b_w_out, c_dw, c_dw_b, c_ln, c_w_out, d_w_out, gate_w, gate_b, w_o, ffn_w1, ffn_w3, ffn_w2.astype(BF16))
    h_lat = h_ctx = None
    for l in range(depth):
        last = l == depth - 1
        lp = _layer_params(l, *weights)
        ng = [norm_g[l, i].reshape(1, d) for i in range(4)]
        ml = [mods[l, 0:1, i * d:(i + 1) * d] for i in range(6)]
        mc = [mods[l, 1:2, i * d:(i + 1) * d] for i in range(6)]
        if l == 0:
            h_lat = normmod(x_lat, ng[0], ml[0], ml[1])
            h_ctx = normmod(x_ctx, ng[0], mc[0], mc[1])

        if last:
            ctx_states = _rwkv_scan(_in_proj_a(h_ctx, lp), zero_state, lp["ap"], x_ctx.shape[0])[4]
        else:
            mix_ctx, ctx_states = _token_mix(h_ctx, zero_state, x_ctx.shape[0], lp)

        mix_lat, _ = _token_mix(h_lat, ctx_states, GRID_W, lp)
        x_lat, h2 = resnorm(x_lat, mix_lat, ml[2], ng[1], (ng[2], ml[3], ml[4]))
        y = _ffn(h2, lp)
        if last:
            x_lat = resnorm(x_lat, y, ml[5], ng[3])
        else:
            ngn = norm_g[l + 1, 0].reshape(1, d)
            mln = [mods[l + 1, 0:1, i * d:(i + 1) * d] for i in range(2)]
            mcn = [mods[l + 1, 1:2, i * d:(i + 1) * d] for i in range(2)]
            x_lat, h_lat = resnorm(x_lat, y, ml[5], ng[3], (ngn, mln[0], mln[1]))
            x_ctx, h2c = resnorm(x_ctx, mix_ctx, mc[2], ng[1], (ng[2], mc[3], mc[4]))
            yc = _ffn(h2c, lp)
            x_ctx, h_ctx = resnorm(x_ctx, yc, mc[5], ng[3], (ngn, mcn[0], mcn[1]))
    return x_lat[None]
```

```python
import functools
import math

import numpy as np
import jax
import jax.numpy as jnp
from jax import lax
from jax.experimental import pallas as pl
from jax.experimental.pallas import tpu as pltpu

F32, BF16 = jnp.float32, jnp.bfloat16

D_MODEL = 4096
DEPTH = 2
GRID_W = 64
BRANCH_W = 1024
HEAD = 64
HEADS = BRANCH_W // HEAD
PAIRS = HEADS // 2
LORA_W, LORA_A, LORA_G = 64, 64, 160
A_PROJ = 3 * BRANCH_W + LORA_W + LORA_A + LORA_G
A_GN_EPS = 64e-5
B_CHUNK = 128
B_GROUPS = 8
C_KERNEL = 31
D_GROUPS = 4
D_GROUP_CH = BRANCH_W // D_GROUPS
FFN_HIDDEN = 11008
RMS_EPS = 1e-6
LN_EPS = 1e-5

LANES = 128
SUBLANES = 8
VMEM_BUDGET = 60 * 1024 * 1024

A_LORA = 3 * BRANCH_W
A_GC = A_LORA + LANES
A_PAD = A_GC + 3 * LANES

MERGE_TM = 512
CHUNK = 64
CONV_HALO = 16


def _cparams(n_axes, vmem_bytes):
    limit = int(min(max(vmem_bytes + (8 << 20), 32 << 20), VMEM_BUDGET))
    return pltpu.CompilerParams(dimension_semantics=("arbitrary",) * n_axes, vmem_limit_bytes=limit)


def _dot(a, b):
    return jnp.dot(a.astype(BF16), b.astype(BF16), preferred_element_type=F32)


def _dot_nt(a, b):
    return lax.dot_general(a.astype(BF16), b.astype(BF16), (((1,), (1,)), ((), ())), preferred_element_type=F32)


def _dot_tn(a, b):
    return lax.dot_general(a.astype(BF16), b.astype(BF16), (((0,), (0,)), ((), ())), preferred_element_type=F32)


def _split3(x):
    hi = x.astype(BF16)
    r1 = x - hi.astype(F32)
    mid = r1.astype(BF16)
    lo = (r1 - mid.astype(F32)).astype(BF16)
    return hi, mid, lo


def _dot_split_rhs(x, e):
    hi = x.astype(BF16)
    lo = (x - hi.astype(F32)).astype(BF16)
    f = lambda p: jnp.dot(p, e, preferred_element_type=F32)
    return f(hi) + f(lo)


def _dot_exact_lhs(e, x):
    hi, mid, lo = _split3(x)
    f = lambda p: jnp.dot(e, p, preferred_element_type=F32)
    return f(hi) + f(mid) + f(lo)


def _sigmoid(x):
    return 1.0 / (1.0 + jnp.exp(-x))


def _silu(x):
    return x * _sigmoid(x)


def _softplus(x):
    return jnp.maximum(x, 0.0) + jnp.log(1.0 + jnp.exp(-jnp.abs(x)))


def _gelu_tanh(x):
    return 0.5 * x * (1.0 + jnp.tanh(math.sqrt(2.0 / math.pi) * (x + 0.044715 * (x * x * x))))


def _rmsnorm(x, g):
    return (x * lax.rsqrt(jnp.mean(x * x, axis=-1, keepdims=True) + RMS_EPS)) * g


def _layernorm(x, g, b):
    mu = jnp.mean(x, axis=-1, keepdims=True)
    xc = x - mu
    var = jnp.mean(xc * xc, axis=-1, keepdims=True)
    return (xc * lax.rsqrt(var + LN_EPS)) * g + b


MOD_ROWS = 32


def _mod_body(c_ref, w_ref, b_ref, o_ref, s0_ref, s1_ref):
    d, tn = w_ref.shape[1], w_ref.shape[2]

    @pl.when((pl.program_id(0) == 0) & (pl.program_id(1) == 0))
    def _():
        s = _silu(c_ref[...])
        s0_ref[...] = jnp.broadcast_to(s[:, 0:1], (d, LANES))
        s1_ref[...] = jnp.broadcast_to(s[:, 1:2], (d, LANES))

    def step(k, acc):
        r0 = pl.multiple_of(k * MOD_ROWS, MOD_ROWS)
        w = w_ref[0, pl.ds(r0, MOD_ROWS), :]
        s0 = jnp.concatenate([s0_ref[pl.ds(r0, MOD_ROWS), :]] * (tn // LANES), axis=1)
        s1 = jnp.concatenate([s1_ref[pl.ds(r0, MOD_ROWS), :]] * (tn // LANES), axis=1)
        return acc[0] + w * s0, acc[1] + w * s1

    z = jnp.zeros((MOD_ROWS, tn), F32)
    a0, a1 = lax.fori_loop(0, d // MOD_ROWS, step, (z, z), unroll=2)
    bias = b_ref[0]
    o_ref[0] = jnp.concatenate([jnp.sum(a0, axis=0, keepdims=True) + bias, jnp.sum(a1, axis=0, keepdims=True) + bias,
                                jnp.zeros((6, tn), F32)], axis=0)


def modulation(c_cols, mod_w, mod_b):
    depth, d, n = mod_w.shape
    tn = 512
    return pl.pallas_call(
        _mod_body,
        grid=(depth, n // tn),
        in_specs=[pl.BlockSpec((d, 8), lambda l, j: (0, 0)),
                  pl.BlockSpec((1, d, tn), lambda l, j: (l, 0, j)),
                  pl.BlockSpec((1, 1, tn), lambda l, j: (l, 0, j))],
        out_specs=pl.BlockSpec((1, 8, tn), lambda l, j: (l, 0, j)),
        out_shape=jax.ShapeDtypeStruct((depth, 8, n), F32),
        scratch_shapes=[pltpu.VMEM((d, LANES), F32), pltpu.VMEM((d, LANES), F32)],
        compiler_params=_cparams(2, 2 * d * tn * 4 + 5 * d * LANES * 4),
        name="modulation",
    )(c_cols, mod_w, mod_b)


def _normmod_body(x_ref, g_ref, sh_ref, sc_ref, h_ref):
    h = _rmsnorm(x_ref[...], g_ref[...]) * (1.0 + sc_ref[...]) + sh_ref[...]
    h_ref[...] = h.astype(h_ref.dtype)


def normmod(x, g, shift, scale):
    t, d = x.shape
    tm = 256
    row = pl.BlockSpec((1, d), lambda i: (0, 0))
    return pl.pallas_call(
        _normmod_body,
        grid=(t // tm,),
        in_specs=[pl.BlockSpec((tm, d), lambda i: (i, 0)), row, row, row],
        out_specs=pl.BlockSpec((tm, d), lambda i: (i, 0)),
        out_shape=jax.ShapeDtypeStruct((t, d), BF16),
        compiler_params=_cparams(1, 2 * tm * d * 6),
        name="normmod",
    )(x, g, shift, scale)


def _resnorm_body(x_ref, y_ref, gate_ref, gpost_ref, gpre_ref, sh_ref, sc_ref, xo_ref, h_ref):
    xn = x_ref[...] + gate_ref[...] * _rmsnorm(y_ref[...].astype(F32), gpost_ref[...])
    xo_ref[...] = xn
    h = _rmsnorm(xn, gpre_ref[...]) * (1.0 + sc_ref[...]) + sh_ref[...]
    h_ref[...] = h.astype(h_ref.dtype)


def _res_body(x_ref, y_ref, gate_ref, gpost_ref, xo_ref):
    xo_ref[...] = x_ref[...] + gate_ref[...] * _rmsnorm(y_ref[...].astype(F32), gpost_ref[...])


def resnorm(x, y, gate, g_post, nxt=None):
    t, d = x.shape
    tm = 256
    row = pl.BlockSpec((1, d), lambda i: (0, 0))
    tile = pl.BlockSpec((tm, d), lambda i: (i, 0))
    if nxt is None:
        return pl.pallas_call(
            _res_body, grid=(t // tm,), in_specs=[tile, tile, row, row], out_specs=tile,
            out_shape=jax.ShapeDtypeStruct((t, d), F32),
            compiler_params=_cparams(1, 2 * tm * d * 12), name="residual",
        )(x, y, gate, g_post)
    return pl.pallas_call(
        _resnorm_body, grid=(t // tm,), in_specs=[tile, tile, row, row, row, row, row],
        out_specs=[tile, tile],
        out_shape=[jax.ShapeDtypeStruct((t, d), F32), jax.ShapeDtypeStruct((t, d), BF16)],
        compiler_params=_cparams(1, 2 * tm * d * 14), name="residual_norm",
    )(x, y, gate, g_post, *nxt)


def _mm_body(x_ref, w_ref, *rest, n_side, cast):
    side_in, o_ref, side_out = rest[:n_side], rest[n_side], rest[n_side + 1:2 * n_side + 1]
    for s_in, s_out in zip(side_in, side_out):
        s_out[...] = s_in[...].astype(BF16)
    if cast:
        wb_ref = rest[-1]

        @pl.when(pl.program_id(1) == 0)
        def _():
            wb_ref[...] = w_ref[...].astype(BF16)

        w = wb_ref[...]
    else:
        w = w_ref[...]
    o_ref[...] = jnp.dot(x_ref[...], w, preferred_element_type=F32).astype(o_ref.dtype)


def _wspec(w, layer, rows, tn, col_block):
    if w.ndim == 3:
        return pl.BlockSpec((None, rows, tn), lambda j, i: (layer, 0, col_block(j)))
    return pl.BlockSpec((rows, tn), lambda j, i: (0, col_block(j)))


def matmul(x, w, n, col0, tn, out_dtype, name, layer=None, sides=()):
    m, k = x.shape
    tm = min(m, 1024 if k <= 4096 else 512)
    j0 = col0 // tn
    assert col0 % tn == 0 and n % tn == 0 and m % tm == 0
    nj, ni = n // tn, m // tm
    cast = w.dtype == F32
    vm = 2 * (tm * k * 2 + k * tn * w.dtype.itemsize + tm * tn * 4) + (k * tn * 2 if cast else 0)
    side_in, side_out, side_shape, side_args = [], [], [], []
    for arr, lyr, row0, nrows in sides:
        rps, cols = nrows // (nj * ni), arr.shape[2]
        assert rps * nj * ni == nrows and rps % 16 == 0 and row0 % rps == 0
        side_in.append(pl.BlockSpec((None, rps, cols),
                                    functools.partial(lambda j, i, lyr, b0: (lyr, b0 + j * ni + i, 0), lyr=lyr, b0=row0 // rps)))
        side_out.append(pl.BlockSpec((rps, cols), lambda j, i: (j * ni + i, 0)))
        side_shape.append(jax.ShapeDtypeStruct((nrows, cols), BF16))
        side_args.append(arr)
        vm += 2 * rps * cols * 6
    out = pl.pallas_call(
        functools.partial(_mm_body, n_side=len(sides), cast=cast),
        grid=(nj, ni),
        in_specs=[pl.BlockSpec((tm, k), lambda j, i: (i, 0)),
                  _wspec(w, layer, k, tn, lambda j: j + j0)] + side_in,
        out_specs=[pl.BlockSpec((tm, tn), lambda j, i: (i, j))] + side_out,
        out_shape=[jax.ShapeDtypeStruct((m, n), out_dtype)] + side_shape,
        scratch_shapes=[pltpu.VMEM((k, tn), BF16)] if cast else [],
        compiler_params=_cparams(2, vm),
        name=name,
    )(x, w, *side_args)
    return (out[0], out[1:]) if sides else out[0]


def _ffn_up_body(x_ref, w1_ref, w3_ref, *rest):
    if len(rest) == 3:
        o_ref, w1b_ref, w3b_ref = rest
    else:
        w2_ref, o_ref, w2b_ref, w1b_ref, w3b_ref = rest
        w2b_ref[...] = w2_ref[...].astype(BF16)

    @pl.when(pl.program_id(1) == 0)
    def _():
        w1b_ref[...] = w1_ref[...].astype(BF16)
        w3b_ref[...] = w3_ref[...].astype(BF16)

    x = x_ref[...]
    a = jnp.dot(x, w1b_ref[...], preferred_element_type=F32)
    b = jnp.dot(x, w3b_ref[...], preferred_element_type=F32)
    o_ref[...] = (_silu(a) * b).astype(o_ref.dtype)


def ffn_up(h, w1, w3, layer, w2=None):
    m, k = h.shape
    n = w1.shape[-1]
    tm, tn = min(m, 1024), 256
    nj, ni = n // tn, m // tm
    vm = 2 * (tm * k * 2 + 2 * k * tn * 4 + tm * tn * 2) + 2 * k * tn * 2 + 3 * tm * tn * 4
    in_specs = [pl.BlockSpec((tm, k), lambda j, i: (i, 0)),
                _wspec(w1, layer, k, tn, lambda j: j),
                _wspec(w3, layer, k, tn, lambda j: j)]
    out_specs = [pl.BlockSpec((tm, tn), lambda j, i: (i, j))]
    out_shape = [jax.ShapeDtypeStruct((m, n), BF16)]
    args = [h, w1, w3]
    if w2 is not None:
        rows, dm = w2.shape[1] // (nj * ni), w2.shape[2]
        assert rows * nj * ni == w2.shape[1] and rows % 16 == 0
        in_specs.append(pl.BlockSpec((None, rows, dm), lambda j, i: (layer, j * ni + i, 0)))
        out_specs.append(pl.BlockSpec((rows, dm), lambda j, i: (j * ni + i, 0)))
        out_shape.append(jax.ShapeDtypeStruct(w2.shape[1:], BF16))
        args.append(w2)
        vm += 2 * rows * dm * 6
    out = pl.pallas_call(
        _ffn_up_body,
        grid=(nj, ni),
        in_specs=in_specs,
        out_specs=out_specs,
        out_shape=out_shape,
        scratch_shapes=[pltpu.VMEM((k, tn), BF16), pltpu.VMEM((k, tn), BF16)],
        compiler_params=_cparams(2, vm),
        name="ffn_up",
    )(*args)
    return out if w2 is not None else out[0]


def _merge_body(h_ref, *refs):
    z, g_top, g_bot, bias, w_out, o_ref = refs[0:4], refs[4:8], refs[8:12], refs[12:16], refs[16:20], refs[20]
    half = g_top[0].shape[0]
    h_top, h_bot = h_ref[:, :half], h_ref[:, half:]
    acc = None
    for br in range(4):
        logits = (jnp.dot(h_top, g_top[br][...], preferred_element_type=F32)
                  + jnp.dot(h_bot, g_bot[br][...], preferred_element_type=F32) + bias[br][...])
        y = jnp.dot(z[br][...], w_out[br][...], preferred_element_type=F32)
        acc = _sigmoid(logits) * y if acc is None else acc + _sigmoid(logits) * y
    o_ref[...] = acc.astype(o_ref.dtype)


def merge(h, zs, gate_halves, gate_b, w_outs):
    m, d = h.shape
    bw = zs[0].shape[1]
    half = gate_halves[0].shape[0]
    tm, tn = min(m, MERGE_TM), 256
    nj = d // tn
    hspec = pl.BlockSpec((tm, d), lambda j, i: (i, 0))
    zspec = pl.BlockSpec((tm, bw), lambda j, i: (i, 0))
    gspecs = [pl.BlockSpec((half, tn), functools.partial(lambda j, i, br: (0, br * nj + j), br=br)) for br in range(4)]
    bspecs = [pl.BlockSpec((1, tn), functools.partial(lambda j, i, br: (0, br * nj + j), br=br)) for br in range(4)]
    wspec = pl.BlockSpec((bw, tn), lambda j, i: (0, j))
    vm = 2 * (tm * d * 2 + 4 * tm * bw * 2 + tm * tn * 2 + 4 * (d + bw) * tn * 2) + 3 * tm * tn * 4
    return pl.pallas_call(
        _merge_body,
        grid=(nj, m // tm),
        in_specs=[hspec] + [zspec] * 4 + gspecs + gspecs + bspecs + [wspec] * 4,
        out_specs=pl.BlockSpec((tm, tn), lambda j, i: (i, j)),
        out_shape=jax.ShapeDtypeStruct((m, d), BF16),
        compiler_params=_cparams(2, vm),
        name="merge",
    )(h, *zs, *([gate_halves[0]] * 4), *([gate_halves[1]] * 4), gate_b, gate_b, gate_b, gate_b, *w_outs)


def _token_shift(x, mu, row_len):
    tm = x.shape[0]
    pos = lax.broadcasted_iota(jnp.int32, x.shape, 0) & (row_len - 1)
    prev = jnp.where(pos == 0, 0.0, pltpu.roll(x, 1, 0))
    nxt = jnp.where(pos == row_len - 1, 0.0, pltpu.roll(x, tm - 1, 0))
    return x + mu * (0.5 * (prev + nxt) - x)


def _aprep_body(r_ref, k_ref, v_ref, wa_ref, g1_ref, g2_ref,
                mur_ref, muk_ref, muv_ref, muwa_ref, mug1_ref, mug2_ref,
                w0_ref, wup_ref, a0_ref, aup_ref, gup_ref, kk_ref, ka_ref, rk_ref, e_ref,
                ro_ref, vo_ref, kko_ref, lw_ref, kd_ref, b_ref, bonus_ref, g_ref, *, row_len):
    e = e_ref[...]
    xr = _token_shift(r_ref[...], mur_ref[...], row_len)
    xk = _token_shift(k_ref[...], muk_ref[...], row_len)
    xv = _token_shift(v_ref[...], muv_ref[...], row_len)
    xwa = _token_shift(wa_ref[...], muwa_ref[...], row_len)
    xg1 = _token_shift(g1_ref[...], mug1_ref[...], row_len)
    xg2 = _token_shift(g2_ref[...], mug2_ref[...], row_len)
    kk = xk * kk_ref[...]
    kk = kk * lax.rsqrt(jnp.maximum(_dot_split_rhs(kk * kk, e), 1e-12))
    tw = jnp.tanh(xwa)
    bonus = None
    for d in range(2):
        w = -_softplus(-(w0_ref[d] + _dot(tw, wup_ref[d]))) - 0.5
        a = _sigmoid(a0_ref[d] + _dot(xwa, aup_ref[d]))
        kd = xk * (1.0 + (a - 1.0) * ka_ref[...])
        lw_ref[d] = -jnp.exp(w)
        kd_ref[d] = kd
        b_ref[d] = kk * a
        bn = _dot_split_rhs(xr * kd * rk_ref[...], e) * xv
        bonus = bn if bonus is None else bonus + bn
    ro_ref[...] = xr
    vo_ref[...] = xv
    kko_ref[...] = kk
    bonus_ref[...] = bonus
    g_ref[...] = _dot(_sigmoid(xg1), gup_ref[0]) + _dot(_sigmoid(xg2), gup_ref[1])


N_PREP_IN = 21


def _rwkv_maps_body(*refs, row_len, nchunk):
    prep_in, (tri_ref, msk_ref) = refs[:N_PREP_IN], refs[N_PREP_IN:N_PREP_IN + 2]
    rhat_ref, y0_ref, m_ref, n_ref, bonus_ref, g_ref = refs[N_PREP_IN + 2:N_PREP_IN + 8]
    r_s, v_s, kk_s, lw_s, kd_s, b_s = refs[N_PREP_IN + 8:]
    _aprep_body(*prep_in, r_s, v_s, kk_s, lw_s, kd_s, b_s, bonus_ref, g_ref, row_len=row_len)
    _scan_pre_body(r_s, v_s, kk_s, lw_s, kd_s, b_s, tri_ref, msk_ref, rhat_ref, y0_ref, m_ref, n_ref, nchunk=nchunk)


def rwkv_maps(pa, ap, row_len, tri, msk):
    t = pa.shape[0]
    nchunk = 4
    tm = nchunk * CHUNK
    nc = t // CHUNK
    nb = BRANCH_W // LANES
    col = lambda c: pl.BlockSpec((tm, LANES), functools.partial(lambda i, p, c: (i, c + p), c=c))
    fix = lambda c: pl.BlockSpec((tm, LANES), functools.partial(lambda i, p, c: (i, c), c=c))
    mcol = lambda c: pl.BlockSpec((1, LANES), functools.partial(lambda i, p, c: (0, c + p), c=c))
    mfix = lambda c: pl.BlockSpec((1, LANES), functools.partial(lambda i, p, c: (0, c), c=c))
    prow = pl.BlockSpec((1, LANES), lambda i, p: (0, p))
    p2 = pl.BlockSpec((2, 1, LANES), lambda i, p: (0, 0, p))
    up2 = pl.BlockSpec((2, LANES, LANES), lambda i, p: (0, 0, p))
    out1 = pl.BlockSpec((tm, LANES), lambda i, p: (i, p))
    out2 = pl.BlockSpec((2, tm, LANES), lambda i, p: (0, i, p))
    mspec = pl.BlockSpec((2, nchunk, 1, HEAD, LANES), lambda i, p: (0, i, p, 0, 0))
    s1 = jax.ShapeDtypeStruct((t, BRANCH_W), F32)
    s2 = jax.ShapeDtypeStruct((2, t, BRANCH_W), BF16)
    sm = jax.ShapeDtypeStruct((2, nc, PAIRS, HEAD, LANES), F32)
    lora, gc = A_LORA // LANES, A_GC // LANES
    tile = lambda lead=(): pltpu.VMEM(lead + (tm, LANES), F32)
    return pl.pallas_call(
        functools.partial(_rwkv_maps_body, row_len=row_len, nchunk=nchunk),
        grid=(t // tm, nb),
        in_specs=[col(0), col(nb), col(2 * nb), fix(lora), fix(gc), fix(gc + 1),
                  mcol(0), mcol(nb), mcol(2 * nb), mfix(lora), mfix(gc), mfix(gc + 1),
                  p2, up2, p2, up2, up2, prow, prow, prow,
                  pl.BlockSpec((LANES, LANES), lambda i, p: (0, 0)),
                  pl.BlockSpec((2, CHUNK, CHUNK), lambda i, p: (0, 0, 0)),
                  pl.BlockSpec(msk.shape, lambda i, p: (0, 0, 0, 0))],
        out_specs=[out2, out2, mspec, mspec, out1, out1],
        out_shape=[s2, s2, sm, sm, s1, s1],
        scratch_shapes=[tile(), tile(), tile(), tile((2,)), tile((2,)), tile((2,))],
        compiler_params=_cparams(2, 24 << 20),
        name="rwkv_chunk_maps",
    )(pa, pa, pa, pa, pa, pa, ap["mu"], ap["mu"], ap["mu"], ap["mu"], ap["mu"], ap["mu"],
      ap["w0"], ap["w_up"], ap["a0"], ap["a_up"], ap["g_up"], ap["k_k"], ap["k_a"], ap["r_k"], ap["e2"], tri, msk)


def _pair_rows(x):
    lane = lax.broadcasted_iota(jnp.int32, x.shape, 1)
    return jnp.concatenate([jnp.where(lane < HEAD, x, 0.0), jnp.where(lane >= HEAD, x, 0.0)], axis=0)


MSK_STRICT, MSK_INCL, MSK_EYE, MSK_DIAG8, MSK_OFF8, MSK_OFF16, MSK_OFF32 = range(7)


def _chunk_mask_tables():
    n = 2 * CHUNK
    r = np.arange(n)[:, None]
    c = np.arange(n)[None, :]
    same = (r // CHUNK) == (c // CHUNK)
    out = np.zeros((2, 7, n, n), np.float32)
    for d in range(2):
        before = (c > r) if d else (c < r)
        out[d, MSK_STRICT] = same & before
        out[d, MSK_INCL] = same & (before | (r == c))
        out[d, MSK_EYE] = r == c
        out[d, MSK_DIAG8] = ((r // 8) == (c // 8)) & before
        for idx, s in ((MSK_OFF8, 8), (MSK_OFF16, 16), (MSK_OFF32, 32)):
            blk = (r // (2 * s)) == (c // (2 * s))
            rh, ch = (r // s) % 2, (c // s) % 2
            out[d, idx] = blk & ((rh == 0) & (ch == 1) if d else (rh == 1) & (ch == 0))
    return jnp.asarray(out)


def _scan_pre_body(r_ref, v_ref, kk_ref, lw_ref, kd_ref, b_ref, tri_ref, msk_ref, rhat_ref, y0_ref, m_ref, n_ref, *, nchunk):
    c, n = CHUNK, 2 * CHUNK
    chains = [(d, ci) for d in range(2) for ci in range(nchunk)]
    each = lambda f, *cols: [f(*xs) for xs in zip(*cols)]
    msk = lambda d, k: msk_ref[d, k]
    rows = lambda ci: slice(ci * c, (ci + 1) * c)

    lw = [lw_ref[d, rows(ci), :] for d, ci in chains]
    cum = [_dot_exact_lhs(tri_ref[d], x) for (d, _), x in zip(chains, lw)]
    tot = [x[0:1] if d else x[c - 1:c] for (d, _), x in zip(chains, cum)]
    g_inv = each(lambda x: jnp.exp(-x), cum)
    g_tail = each(lambda t, x: jnp.exp(t - x), tot, cum)
    atp = [_pair_rows(-kk_ref[rows(ci), :] * jnp.exp(x - l)) for (_, ci), x, l in zip(chains, cum, lw)]
    rtp = [_pair_rows(r_ref[rows(ci), :] * jnp.exp(x)) for (_, ci), x in zip(chains, cum)]
    btp = [_pair_rows(b_ref[d, rows(ci), :] * g) for (d, ci), g in zip(chains, g_inv)]
    ktp = [_pair_rows(kd_ref[d, rows(ci), :] * g) for (d, ci), g in zip(chains, g_inv)]
    bhp = [_pair_rows(b_ref[d, rows(ci), :] * g) for (d, ci), g in zip(chains, g_tail)]
    khp = [_pair_rows(kd_ref[d, rows(ci), :] * g) for (d, ci), g in zip(chains, g_tail)]
    vp = [_pair_rows(v_ref[rows(ci), :]) for _, ci in chains]

    big = each(lambda a, r, b, k: _dot_nt(jnp.concatenate([a, r], axis=0), jnp.concatenate([b, k], axis=0)),
               atp, rtp, btp, ktp)
    a_ab = [jnp.where(msk(d, MSK_STRICT) > 0.0, x[:n, :n], 0.0) for (d, _), x in zip(chains, big)]
    a_ak = [jnp.where(msk(d, MSK_STRICT) > 0.0, x[:n, n:], 0.0) for (d, _), x in zip(chains, big)]
    a_rb = [jnp.where(msk(d, MSK_INCL) > 0.0, x[n:, :n], 0.0) for (d, _), x in zip(chains, big)]
    a_rk = [jnp.where(msk(d, MSK_INCL) > 0.0, x[n:, n:], 0.0) for (d, _), x in zip(chains, big)]

    n8 = [x * msk(d, MSK_DIAG8) for (d, _), x in zip(chains, a_ab)]
    t = [msk(d, MSK_EYE) + x for (d, _), x in zip(chains, n8)]
    n2 = each(_dot, n8, n8)
    t = each(lambda x, p: x + _dot(p, x), t, n2)
    n4 = each(_dot, n2, n2)
    t = each(lambda x, p: x + _dot(p, x), t, n4)
    for idx in (MSK_OFF8, MSK_OFF16, MSK_OFF32):
        off = [x * msk(d, idx) for (d, _), x in zip(chains, a_ab)]
        tn = each(_dot, t, off)
        t = each(lambda x, p: x + _dot(p, x), t, tn)

    akv = each(_dot, a_ak, vp)
    w_u = each(lambda ti, a, x: _dot(ti, jnp.concatenate([a, x], axis=1)), t, atp, akv)
    ry = each(_dot, a_rb, w_u)
    rkv = each(_dot, a_rk, vp)
    mn = each(_dot_tn, bhp, w_u)
    kv = each(_dot_tn, khp, vp)
    for i, (d, ci) in enumerate(chains):
        rhat_p = rtp[i] + ry[i][:, :n]
        y0_p = ry[i][:, n:] + rkv[i]
        rhat_ref[d, rows(ci), :] = (rhat_p[:c] + rhat_p[c:]).astype(rhat_ref.dtype)
        y0_ref[d, rows(ci), :] = (y0_p[:c] + y0_p[c:]).astype(y0_ref.dtype)
        m_i = mn[i][:, :n] + msk(d, MSK_EYE) * jnp.exp(tot[i])
        n_i = mn[i][:, n:] + kv[i]
        m_ref[d, ci, 0] = m_i[:c] + m_i[c:]
        n_ref[d, ci, 0] = n_i[:c] + n_i[c:]


def _scan_seq_body(s0_ref, mf_ref, mb_ref, nf_ref, nb_ref, rf_ref, rb_ref, yf0_ref, yb0_ref,
                   yf_ref, yb_ref, sfin_ref, s_ref):
    c = pl.program_id(0)

    @pl.when(c == 0)
    def _():
        s_ref[...] = s0_ref[...]

    for d, (m_ref, n_ref, rh_ref, y0_ref, y_ref) in enumerate(((mf_ref, nf_ref, rf_ref, yf0_ref, yf_ref),
                                                                 (mb_ref, nb_ref, rb_ref, yb0_ref, yb_ref))):
        for p in range(PAIRS):
            ln = slice(p * LANES, (p + 1) * LANES)
            s = s_ref[d, p]
            y_ref[:, ln] = _dot(rh_ref[0, :, ln], s) + y0_ref[0, :, ln].astype(F32)
            s_ref[d, p] = _dot(_pair_rows(m_ref[0, 0, p]), s) + _pair_rows(n_ref[0, 0, p])

    @pl.when(c == pl.num_programs(0) - 1)
    def _():
        sfin_ref[...] = s_ref[...]


def scan_seq(s0, rhat, y0, m, n):
    t = rhat.shape[1]
    nc = t // CHUNK
    fwd5 = lambda c: (0, c, 0, 0, 0)
    bwd5 = lambda c: (1, nc - 1 - c, 0, 0, 0)
    mblk = (1, 1, PAIRS, HEAD, LANES)
    rblk = (1, CHUNK, BRANCH_W)
    sblk = pl.BlockSpec((2, PAIRS, LANES, LANES), lambda c: (0, 0, 0, 0))
    ys = jax.ShapeDtypeStruct((t, BRANCH_W), F32)
    return pl.pallas_call(
        _scan_seq_body,
        grid=(nc,),
        in_specs=[sblk,
                  pl.BlockSpec(mblk, fwd5), pl.BlockSpec(mblk, bwd5),
                  pl.BlockSpec(mblk, fwd5), pl.BlockSpec(mblk, bwd5),
                  pl.BlockSpec(rblk, lambda c: (0, c, 0)), pl.BlockSpec(rblk, lambda c: (1, nc - 1 - c, 0)),
                  pl.BlockSpec(rblk, lambda c: (0, c, 0)), pl.BlockSpec(rblk, lambda c: (1, nc - 1 - c, 0))],
        out_specs=[pl.BlockSpec((CHUNK, BRANCH_W), lambda c: (c, 0)),
                   pl.BlockSpec((CHUNK, BRANCH_W), lambda c: (nc - 1 - c, 0)),
                   sblk],
        out_shape=[ys, ys, jax.ShapeDtypeStruct((2, PAIRS, LANES, LANES), F32)],
        scratch_shapes=[pltpu.VMEM((2, PAIRS, LANES, LANES), F32)],
        compiler_params=_cparams(1, 16 << 20),
        name="rwkv_chain",
    )(s0, m, m, n, n, rhat, rhat, y0, y0)


def _areadout_body(yf_ref, yb_ref, bonus_ref, g_ref, lng_ref, lnb_ref, e_ref, o_ref):
    e = e_ref[...]
    for p in range(BRANCH_W // LANES):
        ln = slice(p * LANES, (p + 1) * LANES)
        y = yf_ref[:, ln] + yb_ref[:, ln]
        mu = _dot_split_rhs(y, e) * (1.0 / HEAD)
        yc = y - mu
        var = _dot_split_rhs(yc * yc, e) * (1.0 / HEAD)
        yn = (yc * lax.rsqrt(var + A_GN_EPS)) * lng_ref[:, ln] + lnb_ref[:, ln]
        o_ref[:, ln] = ((yn + bonus_ref[:, ln]) * g_ref[:, ln]).astype(o_ref.dtype)


def a_readout(yf, yb, bonus, g, ap):
    t = yf.shape[0]
    tm = 256
    blk = pl.BlockSpec((tm, BRANCH_W), lambda i: (i, 0))
    prow = pl.BlockSpec((1, BRANCH_W), lambda i: (0, 0))
    return pl.pallas_call(
        _areadout_body,
        grid=(t // tm,),
        in_specs=[blk, blk, blk, blk, prow, prow, pl.BlockSpec((LANES, LANES), lambda i: (0, 0))],
        out_specs=blk,
        out_shape=jax.ShapeDtypeStruct((t, BRANCH_W), BF16),
        compiler_params=_cparams(1, 2 * 5 * tm * BRANCH_W * 4),
        name="rwkv_readout",
    )(yf, yb, bonus, g, ap["ln_g"], ap["ln_b"], ap["e2"])


def _bmix_body(u_ref, v_ref, lng_ref, lnb_ref, ws_ref, bs_ref, o_ref):
    u = _gelu_tanh(u_ref[...].astype(F32))
    v = _layernorm(_gelu_tanh(v_ref[...].astype(F32)), lng_ref[...], lnb_ref[...]).astype(BF16)
    for ci in range(u.shape[0] // B_CHUNK):
        rows = slice(ci * B_CHUNK, (ci + 1) * B_CHUNK)
        for g in range(B_GROUPS):
            ln = slice(g * LANES, (g + 1) * LANES)
            s = jnp.dot(ws_ref[g], v[rows, ln], preferred_element_type=F32) + bs_ref[:, ln]
            o_ref[rows, ln] = (u[rows, ln] * s).astype(o_ref.dtype)


def b_mix(pb, bp):
    t = pb.shape[0]
    tm = 256
    row = pl.BlockSpec((1, BRANCH_W), lambda i: (0, 0))
    return pl.pallas_call(
        _bmix_body,
        grid=(t // tm,),
        in_specs=[pl.BlockSpec((tm, BRANCH_W), lambda i: (i, 0)), pl.BlockSpec((tm, BRANCH_W), lambda i: (i, 1)),
                  row, row, pl.BlockSpec((B_GROUPS, B_CHUNK, B_CHUNK), lambda i: (0, 0, 0)),
                  pl.BlockSpec((B_CHUNK, BRANCH_W), lambda i: (0, 0))],
        out_specs=pl.BlockSpec((tm, BRANCH_W), lambda i: (i, 0)),
        out_shape=jax.ShapeDtypeStruct((t, BRANCH_W), BF16),
        compiler_params=_cparams(1, 12 * tm * BRANCH_W * 4),
        name="gmlp_mix",
    )(pb, pb, bp["ln_g"], bp["ln_b"], bp["ws"], bp["bs"])


def _conv_body(ac_ref, gc_ref, ap_ref, gp_ref, an_ref, gn_ref, dw_ref, dwb_ref, lng_ref, lnb_ref,
               o_ref, zs_ref, cs_ref, *, tm):
    i = pl.program_id(0)
    last = pl.num_programs(0) - 1
    glu = lambda a_ref, g_ref: a_ref[...].astype(F32) * _sigmoid(g_ref[...].astype(F32))
    zs_ref[0:CONV_HALO, :] = jnp.where(i == 0, 0.0, glu(ap_ref, gp_ref))
    zs_ref[CONV_HALO:CONV_HALO + tm, :] = glu(ac_ref, gc_ref)
    zs_ref[CONV_HALO + tm:, :] = jnp.where(i == last, 0.0, glu(an_ref, gn_ref))
    rb, lb = 32, 256
    win = rb + 2 * CONV_HALO
    first = CONV_HALO - C_KERNEL // 2
    for r0 in range(0, tm, rb):
        for l0 in range(0, BRANCH_W, lb):
            w = zs_ref[r0:r0 + win, l0:l0 + lb]
            acc = jnp.zeros((rb, lb), F32) + dwb_ref[:, l0:l0 + lb]
            for s in range(SUBLANES):
                ws = w if s == 0 else pltpu.roll(w, win - s, 0)
                for j in range(C_KERNEL):
                    if (first + j) % SUBLANES == s:
                        a = (first + j) - s
                        acc = acc + ws[a:a + rb] * dw_ref[j:j + 1, l0:l0 + lb]
            cs_ref[r0:r0 + rb, l0:l0 + lb] = acc
    o_ref[...] = _silu(_layernorm(cs_ref[...], lng_ref[...], lnb_ref[...])).astype(o_ref.dtype)


def conv_mix(pc, cp):
    t = pc.shape[0]
    tm = 256
    hb = tm // CONV_HALO
    nh = t // CONV_HALO
    cur = lambda c: pl.BlockSpec((tm, BRANCH_W), functools.partial(lambda i, c: (i, c), c=c))
    prev = lambda c: pl.BlockSpec((CONV_HALO, BRANCH_W), functools.partial(lambda i, c: (jnp.maximum(i * hb - 1, 0), c), c=c))
    nxt = lambda c: pl.BlockSpec((CONV_HALO, BRANCH_W), functools.partial(lambda i, c: (jnp.minimum((i + 1) * hb, nh - 1), c), c=c))
    row = pl.BlockSpec((1, BRANCH_W), lambda i: (0, 0))
    return pl.pallas_call(
        functools.partial(_conv_body, tm=tm),
        grid=(t // tm,),
        in_specs=[cur(0), cur(1), prev(0), prev(1), nxt(0), nxt(1),
                  pl.BlockSpec((C_KERNEL + 1, BRANCH_W), lambda i: (0, 0)), row, row, row],
        out_specs=pl.BlockSpec((tm, BRANCH_W), lambda i: (i, 0)),
        out_shape=jax.ShapeDtypeStruct((t, BRANCH_W), BF16),
        scratch_shapes=[pltpu.VMEM((tm + 2 * CONV_HALO, BRANCH_W), F32), pltpu.VMEM((tm, BRANCH_W), F32)],
        compiler_params=_cparams(1, 16 * tm * BRANCH_W * 4),
        name="conv_mix",
    )(pc, pc, pc, pc, pc, pc, cp["dw"], cp["dw_b"], cp["ln_g"], cp["ln_b"])


FFT_N1, FFT_N2 = 64, 128


def _dft_tables(t):
    two_pi = 2.0 * np.pi
    cidx = np.arange(D_GROUP_CH)
    ph = two_pi * np.outer(cidx, cidx) / D_GROUP_CH
    chan = np.concatenate([np.cos(ph), np.sin(ph)], axis=0)
    if t <= 256:
        n = np.arange(t)
        th = two_pi * np.outer(n, n) / t
        m2 = np.concatenate([np.cos(th), -np.sin(th)], axis=0)
        return None, m2.astype(np.float32), chan.astype(np.float32)
    n1, n2 = FFT_N1, FFT_N2
    assert t == n1 * n2
    k1 = np.arange(n1)
    tok = (n2 * np.arange(n1))[None, None, :] + np.arange(n2)[:, None, None]
    th = two_pi * (k1[None, :, None] * tok) / t
    g1 = np.concatenate([np.cos(th), -np.sin(th)], axis=1)
    q = np.arange(n2)
    th2 = two_pi * np.outer(q, q) / n2
    c2, s2 = np.cos(th2), np.sin(th2)
    m2 = np.block([[c2, s2], [-s2, c2]])
    return g1.astype(np.float32), m2.astype(np.float32), chan.astype(np.float32)


def _fft1_body(x_ref, g_ref, o_ref):
    o_ref[...] = _dot(g_ref[0], x_ref[...])


def _fft2_body(z_ref, m2_ref, ch_ref, o_ref, *, n_out, scale, stacked):
    z = jnp.concatenate([z_ref[0], z_ref[1]], axis=0) if stacked else z_ref[...]
    x = _dot(m2_ref[...], z)
    xr, xi = x[:n_out], x[n_out:]
    for g in range(D_GROUPS):
        ln = slice(g * D_GROUP_CH, (g + 1) * D_GROUP_CH)
        f = _dot(xr[:, ln], ch_ref[0:D_GROUP_CH]) + _dot(xi[:, ln], ch_ref[D_GROUP_CH:])
        o_ref[:, ln] = (f * scale).astype(o_ref.dtype)


def fourier_mix(pd):
    t = pd.shape[0]
    g1, m2, chan = (None if a is None else jnp.asarray(a) for a in _dft_tables(t))
    scale = 1.0 / math.sqrt(t * D_GROUP_CH)
    chspec = pl.BlockSpec((2 * D_GROUP_CH, D_GROUP_CH), lambda i: (0, 0))
    if g1 is None:
        return pl.pallas_call(
            functools.partial(_fft2_body, n_out=t, scale=scale, stacked=False),
            grid=(1,),
            in_specs=[pl.BlockSpec((t, BRANCH_W), lambda i: (0, 0)), pl.BlockSpec((2 * t, t), lambda i: (0, 0)), chspec],
            out_specs=pl.BlockSpec((t, BRANCH_W), lambda i: (0, 0)),
            out_shape=jax.ShapeDtypeStruct((t, BRANCH_W), BF16),
            compiler_params=_cparams(1, 16 << 20),
            name="fourier_small",
        )(pd, m2, chan)
    n1, n2 = FFT_N1, FFT_N2
    z = pl.pallas_call(
        _fft1_body,
        grid=(n2,),
        in_specs=[pl.BlockSpec((n1, BRANCH_W), lambda q: (0, q)), pl.BlockSpec((1, 2 * n1, n1), lambda q: (q, 0, 0))],
        out_specs=pl.BlockSpec((2 * n1, BRANCH_W), lambda q: (0, q)),
        out_shape=jax.ShapeDtypeStruct((2 * n1, n2 * BRANCH_W), F32),
        compiler_params=_cparams(1, 8 << 20),
        name="fourier_stage1",
    )(pd.reshape(n1, n2 * BRANCH_W), g1)
    f = pl.pallas_call(
        functools.partial(_fft2_body, n_out=n2, scale=scale, stacked=True),
        grid=(n1,),
        in_specs=[pl.BlockSpec((2, None, n2, BRANCH_W), lambda k: (0, k, 0, 0)),
                  pl.BlockSpec((2 * n2, 2 * n2), lambda k: (0, 0)), chspec],
        out_specs=pl.BlockSpec((n2, BRANCH_W), lambda k: (0, k)),
        out_shape=jax.ShapeDtypeStruct((n2, n1 * BRANCH_W), BF16),
        compiler_params=_cparams(1, 16 << 20),
        name="fourier_stage2",
    )(z.reshape(2, n1, n2, BRANCH_W), m2, chan)
    return f.reshape(t, BRANCH_W)


def _align_a(x):
    return jnp.pad(x, [(0, 0)] * (x.ndim - 1) + [(0, A_PAD - x.shape[-1])])


def _pad_rows(x, before, rows):
    return jnp.pad(x, [(0, 0)] * (x.ndim - 2) + [(before, rows - before - x.shape[-2]), (0, 0)])


def _layer_params(l, w_in, a_mu, a_w0, a_w_up, a_a0, a_a_up, a_g_up, a_k_k, a_k_a, a_r_k, a_ln, a_w_out,
                  b_ln, b_ws, b_bs, b_w_out, c_dw, c_dw_b, c_ln, c_w_out, d_w_out, gate_w, gate_b, w_o,
                  ffn_w1, ffn_w3, ffn_w2):
    row = lambda v: v.reshape(1, -1)
    hid = np.arange(LANES) // HEAD
    e2 = jnp.asarray((hid[:, None] == hid[None, :]).astype(np.float32)).astype(BF16)
    g_up = jnp.stack([a_g_up[l][:LANES], _pad_rows(a_g_up[l][LANES:], 0, LANES)])
    ap = dict(mu=_align_a(row(a_mu[l])), w0=a_w0[l][:, None, :], w_up=_pad_rows(a_w_up[l], 0, LANES),
              a0=a_a0[l][:, None, :], a_up=_pad_rows(a_a_up[l], LORA_W, LANES), g_up=g_up,
              k_k=row(a_k_k[l]), k_a=row(a_k_a[l]), r_k=row(a_r_k[l]), ln_g=row(a_ln[l][0]), ln_b=row(a_ln[l][1]), e2=e2)
    bs_exp = jnp.repeat(jnp.swapaxes(b_bs[l], 0, 1), LANES, axis=1)
    return dict(
        layer=l, w_a=w_in[:, :, :A_PAD], w_bcd=w_in[l][:, A_PROJ:].astype(BF16), ap=ap,
        bp=dict(ln_g=row(b_ln[l][0]), ln_b=row(b_ln[l][1]), ws=b_ws[l].astype(BF16), bs=bs_exp),
        cp=dict(dw=_pad_rows(c_dw[l], 0, C_KERNEL + 1), dw_b=row(c_dw_b[l]), ln_g=row(c_ln[l][0]), ln_b=row(c_ln[l][1])),
        w_outs=[a_w_out, b_w_out, c_w_out, d_w_out],
        gate_w=gate_w, gate_b=row(gate_b[l]), w_o=w_o,
        w1=ffn_w1, w3=ffn_w3, w2=ffn_w2)


def _tri_tables():
    i = np.arange(CHUNK)
    lower = (i[None, :] <= i[:, None]).astype(np.float32)
    return jnp.asarray(np.stack([lower, lower.T])).astype(BF16)


def _rwkv_scan(pa, s0, ap, row_len):
    rhat, y0, m, n, bonus, g = rwkv_maps(pa, ap, row_len, _tri_tables(), _chunk_mask_tables())
    yf, yb, s_fin = scan_seq(s0, rhat, y0, m, n)
    return yf, yb, bonus, g, s_fin


def _in_proj_a(h, lp):
    return matmul(h, lp["w_a"], A_PAD, 0, 512, F32, "in_proj_a", layer=lp["layer"])


def _in_proj_bcd(h, lp, cast_merge_weights=False):
    tn, l, w = 512, lp["layer"], lp["w_bcd"]
    if not cast_merge_weights:
        return (matmul(h, w, 2 * BRANCH_W, 0, tn, BF16, "in_proj_b"),
                matmul(h, w, 2 * BRANCH_W, 2 * BRANCH_W, tn, BF16, "in_proj_c"),
                matmul(h, w, BRANCH_W, 4 * BRANCH_W, tn, BF16, "in_proj_d")), None
    half = D_MODEL // 2
    pb, (g_top,) = matmul(h, w, 2 * BRANCH_W, 0, tn, BF16, "in_proj_b", sides=[(lp["gate_w"], l, 0, half)])
    pc, (g_bot,) = matmul(h, w, 2 * BRANCH_W, 2 * BRANCH_W, tn, BF16, "in_proj_c", sides=[(lp["gate_w"], l, half, half)])
    pd, w_outs = matmul(h, w, BRANCH_W, 4 * BRANCH_W, tn, BF16, "in_proj_d",
                        sides=[(wo, l, 0, BRANCH_W) for wo in lp["w_outs"]])
    return (pb, pc, pd), ((g_top, g_bot), list(w_outs))


def _token_mix(h, s0, row_len, lp, pbcd, merge_w):
    pa = _in_proj_a(h, lp)
    yf, yb, bonus, g, s_fin = _rwkv_scan(pa, s0, lp["ap"], row_len)
    za = a_readout(yf, yb, bonus, g, lp["ap"])
    zb = b_mix(pbcd[0], lp["bp"])
    zc = conv_mix(pbcd[1], lp["cp"])
    zd = fourier_mix(pbcd[2])
    merged = merge(h, [za, zb, zc, zd], merge_w[0], lp["gate_b"], merge_w[1])
    mix = matmul(merged, lp["w_o"], D_MODEL, 0, 512, BF16, "out_proj", layer=lp["layer"])
    return mix, s_fin


def _ffn(h2, lp, w2_bf16=None):
    if w2_bf16 is None:
        u, w2_bf16 = ffn_up(h2, lp["w1"], lp["w3"], lp["layer"], lp["w2"])
    else:
        u = ffn_up(h2, lp["w1"], lp["w3"], lp["layer"])
    return matmul(u, w2_bf16, D_MODEL, 0, 512, BF16, "ffn_down"), w2_bf16


def kernel(x, c, ctx, c_ctx, mod_w, mod_b, norm_g, w_in, a_mu, a_w0, a_w_up, a_a0, a_a_up, a_g_up, a_k_k, a_k_a,
           a_r_k, a_ln, a_w_out, b_ln, b_ws, b_bs, b_w_out, c_dw, c_dw_b, c_ln, c_w_out, d_w_out, gate_w, gate_b,
           w_o, ffn_w1, ffn_w3, ffn_w2):
    depth = mod_w.shape[0]
    d = D_MODEL
    c_cols = jnp.concatenate([c.reshape(d, 1), c_ctx.reshape(d, 1), jnp.zeros((d, 6), F32)], axis=1)
    mods = modulation(c_cols, mod_w, mod_b.reshape(depth, 1, 6 * d))
    x_lat, x_ctx = x[0], ctx[0]
    zero_state = jnp.zeros((2, PAIRS, LANES, LANES), F32)
    weights = (w_in, a_mu, a_w0, a_w_up, a_a0, a_a_up, a_g_up, a_k_k, a_k_a, a_r_k, a_ln, a_w_out, b_ln, b_ws, b_bs,
               b_w_out, c_dw, c_dw_b, c_ln, c_w_out, d_w_out, gate_w, gate_b, w_o, ffn_w1, ffn_w3, ffn_w2)
    h_lat = h_ctx = None
    for l in range(depth):
        last = l == depth - 1
        lp = _layer_params(l, *weights)
        ng = [norm_g[l, i].reshape(1, d) for i in range(4)]
        ml = [mods[l, 0:1, i * d:(i + 1) * d] for i in range(6)]
        mc = [mods[l, 1:2, i * d:(i + 1) * d] for i in range(6)]
        if l == 0:
            h_lat = normmod(x_lat, ng[0], ml[0], ml[1])
            h_ctx = normmod(x_ctx, ng[0], mc[0], mc[1])

        pbcd_lat, merge_w = _in_proj_bcd(h_lat, lp, cast_merge_weights=True)

        if last:
            ctx_states = _rwkv_scan(_in_proj_a(h_ctx, lp), zero_state, lp["ap"], x_ctx.shape[0])[4]
        else:
            pbcd_ctx, _ = _in_proj_bcd(h_ctx, lp)
            mix_ctx, ctx_states = _token_mix(h_ctx, zero_state, x_ctx.shape[0], lp, pbcd_ctx, merge_w)

        mix_lat, _ = _token_mix(h_lat, ctx_states, GRID_W, lp, pbcd_lat, merge_w)
        x_lat, h2 = resnorm(x_lat, mix_lat, ml[2], ng[1], (ng[2], ml[3], ml[4]))
        y, w2_bf16 = _ffn(h2, lp)
        if last:
            x_lat = resnorm(x_lat, y, ml[5], ng[3])
        else:
            ngn = norm_g[l + 1, 0].reshape(1, d)
            mln = [mods[l + 1, 0:1, i * d:(i + 1) * d] for i in range(2)]
            mcn = [mods[l + 1, 1:2, i * d:(i + 1) * d] for i in range(2)]
            x_lat, h_lat = resnorm(x_lat, y, ml[5], ng[3], (ngn, mln[0], mln[1]))
            x_ctx, h2c = resnorm(x_ctx, mix_ctx, mc[2], ng[1], (ng[2], mc[3], mc[4]))
            yc, _ = _ffn(h2c, lp, w2_bf16)
            x_ctx, h_ctx = resnorm(x_ctx, yc, mc[5], ng[3], (ngn, mcn[0], mcn[1]))
    return x_lat[None]
```

```python
import functools
import math

import numpy as np
import jax
import jax.numpy as jnp
from jax import lax
from jax.experimental import pallas as pl
from jax.experimental.pallas import tpu as pltpu

F32, BF16 = jnp.float32, jnp.bfloat16

D_MODEL = 4096
DEPTH = 2
GRID_W = 64
BRANCH_W = 1024
HEAD = 64
HEADS = BRANCH_W // HEAD
PAIRS = HEADS // 2
LORA_W, LORA_A, LORA_G = 64, 64, 160
A_PROJ = 3 * BRANCH_W + LORA_W + LORA_A + LORA_G
A_GN_EPS = 64e-5
B_CHUNK = 128
B_GROUPS = 8
C_KERNEL = 31
D_GROUPS = 4
D_GROUP_CH = BRANCH_W // D_GROUPS
FFN_HIDDEN = 11008
RMS_EPS = 1e-6
LN_EPS = 1e-5

LANES = 128
SUBLANES = 8
VMEM_BUDGET = 60 * 1024 * 1024

A_LORA = 3 * BRANCH_W
A_GC = A_LORA + LANES
A_PAD = A_GC + 3 * LANES

MERGE_TM = 512
CHUNK = 64
CHAIN_STEP = 2
CONV_HALO = 16


def _cparams(n_axes, vmem_bytes):
    limit = int(min(max(vmem_bytes + (8 << 20), 32 << 20), VMEM_BUDGET))
    return pltpu.CompilerParams(dimension_semantics=("arbitrary",) * n_axes, vmem_limit_bytes=limit)


def _dot(a, b):
    return jnp.dot(a.astype(BF16), b.astype(BF16), preferred_element_type=F32)


def _dot_nt(a, b):
    return lax.dot_general(a.astype(BF16), b.astype(BF16), (((1,), (1,)), ((), ())), preferred_element_type=F32)


def _dot_tn(a, b):
    return lax.dot_general(a.astype(BF16), b.astype(BF16), (((0,), (0,)), ((), ())), preferred_element_type=F32)


def _split3(x):
    hi = x.astype(BF16)
    r1 = x - hi.astype(F32)
    mid = r1.astype(BF16)
    lo = (r1 - mid.astype(F32)).astype(BF16)
    return hi, mid, lo


def _dot_split_rhs(x, e):
    hi = x.astype(BF16)
    lo = (x - hi.astype(F32)).astype(BF16)
    f = lambda p: jnp.dot(p, e, preferred_element_type=F32)
    return f(hi) + f(lo)


def _dot_exact_lhs(e, x):
    hi, mid, lo = _split3(x)
    f = lambda p: jnp.dot(e, p, preferred_element_type=F32)
    return f(hi) + f(mid) + f(lo)


def _sigmoid(x):
    return 1.0 / (1.0 + jnp.exp(-x))


def _silu(x):
    return x * _sigmoid(x)


def _softplus(x):
    return jnp.maximum(x, 0.0) + jnp.log(1.0 + jnp.exp(-jnp.abs(x)))


def _gelu_tanh(x):
    return 0.5 * x * (1.0 + jnp.tanh(math.sqrt(2.0 / math.pi) * (x + 0.044715 * (x * x * x))))


def _rmsnorm(x, g):
    return (x * lax.rsqrt(jnp.mean(x * x, axis=-1, keepdims=True) + RMS_EPS)) * g


def _layernorm(x, g, b):
    mu = jnp.mean(x, axis=-1, keepdims=True)
    xc = x - mu
    var = jnp.mean(xc * xc, axis=-1, keepdims=True)
    return (xc * lax.rsqrt(var + LN_EPS)) * g + b


MOD_ROWS = 32
MOD_COLS = 512


def _mod_body(c_ref, w_ref, b_ref, o_ref, s0_ref, s1_ref):
    d, tn = w_ref.shape[1], w_ref.shape[2]

    @pl.when((pl.program_id(0) == 0) & (pl.program_id(1) == 0))
    def _():
        s = _silu(c_ref[...])
        s0_ref[...] = jnp.broadcast_to(s[:, 0:1], (d, LANES))
        s1_ref[...] = jnp.broadcast_to(s[:, 1:2], (d, LANES))

    for c0 in range(0, tn, MOD_COLS):
        def step(k, acc):
            r0 = pl.multiple_of(k * MOD_ROWS, MOD_ROWS)
            w = w_ref[0, pl.ds(r0, MOD_ROWS), c0:c0 + MOD_COLS]
            s0 = jnp.concatenate([s0_ref[pl.ds(r0, MOD_ROWS), :]] * (MOD_COLS // LANES), axis=1)
            s1 = jnp.concatenate([s1_ref[pl.ds(r0, MOD_ROWS), :]] * (MOD_COLS // LANES), axis=1)
            return acc[0] + w * s0, acc[1] + w * s1

        z = jnp.zeros((MOD_ROWS, MOD_COLS), F32)
        a0, a1 = lax.fori_loop(0, d // MOD_ROWS, step, (z, z), unroll=2)
        bias = b_ref[0, :, c0:c0 + MOD_COLS]
        o_ref[0, :, c0:c0 + MOD_COLS] = jnp.concatenate(
            [jnp.sum(a0, axis=0, keepdims=True) + bias, jnp.sum(a1, axis=0, keepdims=True) + bias,
             jnp.zeros((6, MOD_COLS), F32)], axis=0)


def modulation(c_cols, mod_w, mod_b):
    depth, d, n = mod_w.shape
    tn = 1024
    return pl.pallas_call(
        _mod_body,
        grid=(depth, n // tn),
        in_specs=[pl.BlockSpec((d, 8), lambda l, j: (0, 0)),
                  pl.BlockSpec((1, d, tn), lambda l, j: (l, 0, j)),
                  pl.BlockSpec((1, 1, tn), lambda l, j: (l, 0, j))],
        out_specs=pl.BlockSpec((1, 8, tn), lambda l, j: (l, 0, j)),
        out_shape=jax.ShapeDtypeStruct((depth, 8, n), F32),
        scratch_shapes=[pltpu.VMEM((d, LANES), F32), pltpu.VMEM((d, LANES), F32)],
        compiler_params=_cparams(2, 2 * d * tn * 4 + 5 * d * LANES * 4),
        name="modulation",
    )(c_cols, mod_w, mod_b)


def _normmod_body(x_ref, g_ref, sh_ref, sc_ref, h_ref):
    h = _rmsnorm(x_ref[...], g_ref[...]) * (1.0 + sc_ref[...]) + sh_ref[...]
    h_ref[...] = h.astype(h_ref.dtype)


def normmod(x, g, shift, scale):
    t, d = x.shape
    tm = 256
    row = pl.BlockSpec((1, d), lambda i: (0, 0))
    return pl.pallas_call(
        _normmod_body,
        grid=(t // tm,),
        in_specs=[pl.BlockSpec((tm, d), lambda i: (i, 0)), row, row, row],
        out_specs=pl.BlockSpec((tm, d), lambda i: (i, 0)),
        out_shape=jax.ShapeDtypeStruct((t, d), BF16),
        compiler_params=_cparams(1, 2 * tm * d * 6),
        name="normmod",
    )(x, g, shift, scale)


def _resnorm_body(x_ref, y_ref, gate_ref, gpost_ref, gpre_ref, sh_ref, sc_ref, xo_ref, h_ref):
    xn = x_ref[...] + gate_ref[...] * _rmsnorm(y_ref[...].astype(F32), gpost_ref[...])
    xo_ref[...] = xn
    h = _rmsnorm(xn, gpre_ref[...]) * (1.0 + sc_ref[...]) + sh_ref[...]
    h_ref[...] = h.astype(h_ref.dtype)


def _res_body(x_ref, y_ref, gate_ref, gpost_ref, xo_ref):
    xo_ref[...] = x_ref[...] + gate_ref[...] * _rmsnorm(y_ref[...].astype(F32), gpost_ref[...])


def resnorm(x, y, gate, g_post, nxt=None):
    t, d = x.shape
    tm = 256
    row = pl.BlockSpec((1, d), lambda i: (0, 0))
    tile = pl.BlockSpec((tm, d), lambda i: (i, 0))
    if nxt is None:
        return pl.pallas_call(
            _res_body, grid=(t // tm,), in_specs=[tile, tile, row, row], out_specs=tile,
            out_shape=jax.ShapeDtypeStruct((t, d), F32),
            compiler_params=_cparams(1, 2 * tm * d * 12), name="residual",
        )(x, y, gate, g_post)
    return pl.pallas_call(
        _resnorm_body, grid=(t // tm,), in_specs=[tile, tile, row, row, row, row, row],
        out_specs=[tile, tile],
        out_shape=[jax.ShapeDtypeStruct((t, d), F32), jax.ShapeDtypeStruct((t, d), BF16)],
        compiler_params=_cparams(1, 2 * tm * d * 14), name="residual_norm",
    )(x, y, gate, g_post, *nxt)


def _mm_body(x_ref, w_ref, *rest, n_side, cast):
    side_in, o_ref, side_out = rest[:n_side], rest[n_side], rest[n_side + 1:2 * n_side + 1]
    for s_in, s_out in zip(side_in, side_out):
        s_out[...] = s_in[...].astype(BF16)
    if cast:
        wb_ref = rest[-1]

        @pl.when(pl.program_id(1) == 0)
        def _():
            wb_ref[...] = w_ref[...].astype(BF16)

        w = wb_ref[...]
    else:
        w = w_ref[...]
    o_ref[...] = jnp.dot(x_ref[...], w, preferred_element_type=F32).astype(o_ref.dtype)


def _wspec(w, layer, rows, tn, col_block):
    if w.ndim == 3:
        return pl.BlockSpec((None, rows, tn), lambda j, i: (layer, 0, col_block(j)))
    return pl.BlockSpec((rows, tn), lambda j, i: (0, col_block(j)))


def matmul(x, w, n, col0, tn, out_dtype, name, layer=None, sides=()):
    m, k = x.shape
    tm = min(m, 1024 if k <= 4096 else 512)
    j0 = col0 // tn
    assert col0 % tn == 0 and n % tn == 0 and m % tm == 0
    nj, ni = n // tn, m // tm
    cast = w.dtype == F32
    vm = 2 * (tm * k * 2 + k * tn * w.dtype.itemsize + tm * tn * 4) + (k * tn * 2 if cast else 0)
    side_in, side_out, side_shape, side_args = [], [], [], []
    for arr, lyr, row0, nrows in sides:
        rps, cols = nrows // (nj * ni), arr.shape[2]
        assert rps * nj * ni == nrows and rps % 16 == 0 and row0 % rps == 0
        side_in.append(pl.BlockSpec((None, rps, cols),
                                    functools.partial(lambda j, i, lyr, b0: (lyr, b0 + j * ni + i, 0), lyr=lyr, b0=row0 // rps)))
        side_out.append(pl.BlockSpec((rps, cols), lambda j, i: (j * ni + i, 0)))
        side_shape.append(jax.ShapeDtypeStruct((nrows, cols), BF16))
        side_args.append(arr)
        vm += 2 * rps * cols * 6
    out = pl.pallas_call(
        functools.partial(_mm_body, n_side=len(sides), cast=cast),
        grid=(nj, ni),
        in_specs=[pl.BlockSpec((tm, k), lambda j, i: (i, 0)),
                  _wspec(w, layer, k, tn, lambda j: j + j0)] + side_in,
        out_specs=[pl.BlockSpec((tm, tn), lambda j, i: (i, j))] + side_out,
        out_shape=[jax.ShapeDtypeStruct((m, n), out_dtype)] + side_shape,
        scratch_shapes=[pltpu.VMEM((k, tn), BF16)] if cast else [],
        compiler_params=_cparams(2, vm),
        name=name,
    )(x, w, *side_args)
    return (out[0], out[1:]) if sides else out[0]


def _ffn_up_body(x_ref, w1_ref, w3_ref, *rest):
    if len(rest) == 3:
        o_ref, w1b_ref, w3b_ref = rest
    else:
        w2_ref, o_ref, w2b_ref, w1b_ref, w3b_ref = rest
        w2b_ref[...] = w2_ref[...].astype(BF16)

    @pl.when(pl.program_id(1) == 0)
    def _():
        w1b_ref[...] = w1_ref[...].astype(BF16)
        w3b_ref[...] = w3_ref[...].astype(BF16)

    x = x_ref[...]
    a = jnp.dot(x, w1b_ref[...], preferred_element_type=F32)
    b = jnp.dot(x, w3b_ref[...], preferred_element_type=F32)
    o_ref[...] = (_silu(a) * b).astype(o_ref.dtype)


def ffn_up(h, w1, w3, layer, w2=None):
    m, k = h.shape
    n = w1.shape[-1]
    tm, tn = min(m, 1024), 256
    nj, ni = n // tn, m // tm
    vm = 2 * (tm * k * 2 + 2 * k * tn * 4 + tm * tn * 2) + 2 * k * tn * 2 + 3 * tm * tn * 4
    in_specs = [pl.BlockSpec((tm, k), lambda j, i: (i, 0)),
                _wspec(w1, layer, k, tn, lambda j: j),
                _wspec(w3, layer, k, tn, lambda j: j)]
    out_specs = [pl.BlockSpec((tm, tn), lambda j, i: (i, j))]
    out_shape = [jax.ShapeDtypeStruct((m, n), BF16)]
    args = [h, w1, w3]
    if w2 is not None:
        rows, dm = w2.shape[1] // (nj * ni), w2.shape[2]
        assert rows * nj * ni == w2.shape[1] and rows % 16 == 0
        in_specs.append(pl.BlockSpec((None, rows, dm), lambda j, i: (layer, j * ni + i, 0)))
        out_specs.append(pl.BlockSpec((rows, dm), lambda j, i: (j * ni + i, 0)))
        out_shape.append(jax.ShapeDtypeStruct(w2.shape[1:], BF16))
        args.append(w2)
        vm += 2 * rows * dm * 6
    out = pl.pallas_call(
        _ffn_up_body,
        grid=(nj, ni),
        in_specs=in_specs,
        out_specs=out_specs,
        out_shape=out_shape,
        scratch_shapes=[pltpu.VMEM((k, tn), BF16), pltpu.VMEM((k, tn), BF16)],
        compiler_params=_cparams(2, vm),
        name="ffn_up",
    )(*args)
    return out if w2 is not None else out[0]


def _merge_body(h_ref, *refs):
    z, g_top, g_bot, bias, w_out, o_ref = refs[0:4], refs[4:8], refs[8:12], refs[12:16], refs[16:20], refs[20]
    half = g_top[0].shape[0]
    h_top, h_bot = h_ref[:, :half], h_ref[:, half:]
    acc = None
    for br in range(4):
        logits = (jnp.dot(h_top, g_top[br][...], preferred_element_type=F32)
                  + jnp.dot(h_bot, g_bot[br][...], preferred_element_type=F32) + bias[br][...])
        y = jnp.dot(z[br][...], w_out[br][...], preferred_element_type=F32)
        acc = _sigmoid(logits) * y if acc is None else acc + _sigmoid(logits) * y
    o_ref[...] = acc.astype(o_ref.dtype)


def merge(h, zs, gate_halves, gate_b, w_outs):
    m, d = h.shape
    bw = zs[0].shape[1]
    half = gate_halves[0].shape[0]
    tm, tn = min(m, MERGE_TM), 256
    nj = d // tn
    hspec = pl.BlockSpec((tm, d), lambda j, i: (i, 0))
    zspec = pl.BlockSpec((tm, bw), lambda j, i: (i, 0))
    gspecs = [pl.BlockSpec((half, tn), functools.partial(lambda j, i, br: (0, br * nj + j), br=br)) for br in range(4)]
    bspecs = [pl.BlockSpec((1, tn), functools.partial(lambda j, i, br: (0, br * nj + j), br=br)) for br in range(4)]
    wspec = pl.BlockSpec((bw, tn), lambda j, i: (0, j))
    vm = 2 * (tm * d * 2 + 4 * tm * bw * 2 + tm * tn * 2 + 4 * (d + bw) * tn * 2) + 3 * tm * tn * 4
    return pl.pallas_call(
        _merge_body,
        grid=(nj, m // tm),
        in_specs=[hspec] + [zspec] * 4 + gspecs + gspecs + bspecs + [wspec] * 4,
        out_specs=pl.BlockSpec((tm, tn), lambda j, i: (i, j)),
        out_shape=jax.ShapeDtypeStruct((m, d), BF16),
        compiler_params=_cparams(2, vm),
        name="merge",
    )(h, *zs, *([gate_halves[0]] * 4), *([gate_halves[1]] * 4), gate_b, gate_b, gate_b, gate_b, *w_outs)


def _token_shift(x, mu, row_len):
    tm = x.shape[0]
    pos = lax.broadcasted_iota(jnp.int32, x.shape, 0) & (row_len - 1)
    prev = jnp.where(pos == 0, 0.0, pltpu.roll(x, 1, 0))
    nxt = jnp.where(pos == row_len - 1, 0.0, pltpu.roll(x, tm - 1, 0))
    return x + mu * (0.5 * (prev + nxt) - x)


def _aprep_body(r_ref, k_ref, v_ref, wa_ref, g1_ref, g2_ref,
                mur_ref, muk_ref, muv_ref, muwa_ref, mug1_ref, mug2_ref,
                w0_ref, wup_ref, a0_ref, aup_ref, gup_ref, kk_ref, ka_ref, rk_ref, e_ref,
                ro_ref, vo_ref, kko_ref, lw_ref, kd_ref, b_ref, bonus_ref, g_ref, *, row_len):
    e = e_ref[...]
    xr = _token_shift(r_ref[...], mur_ref[...], row_len)
    xk = _token_shift(k_ref[...], muk_ref[...], row_len)
    xv = _token_shift(v_ref[...], muv_ref[...], row_len)
    xwa = _token_shift(wa_ref[...], muwa_ref[...], row_len)
    xg1 = _token_shift(g1_ref[...], mug1_ref[...], row_len)
    xg2 = _token_shift(g2_ref[...], mug2_ref[...], row_len)
    kk = xk * kk_ref[...]
    kk = kk * lax.rsqrt(jnp.maximum(_dot_split_rhs(kk * kk, e), 1e-12))
    tw = jnp.tanh(xwa)
    bonus = None
    for d in range(2):
        w = -_softplus(-(w0_ref[d] + _dot(tw, wup_ref[d]))) - 0.5
        a = _sigmoid(a0_ref[d] + _dot(xwa, aup_ref[d]))
        kd = xk * (1.0 + (a - 1.0) * ka_ref[...])
        lw_ref[d] = -jnp.exp(w)
        kd_ref[d] = kd
        b_ref[d] = kk * a
        bn = _dot_split_rhs(xr * kd * rk_ref[...], e) * xv
        bonus = bn if bonus is None else bonus + bn
    ro_ref[...] = xr
    vo_ref[...] = xv
    kko_ref[...] = kk
    bonus_ref[...] = bonus
    g_ref[...] = _dot(_sigmoid(xg1), gup_ref[0]) + _dot(_sigmoid(xg2), gup_ref[1])


N_PREP_IN = 21


def _rwkv_maps_body(*refs, row_len, nchunk):
    prep_in, (tri_ref, msk_ref) = refs[:N_PREP_IN], refs[N_PREP_IN:N_PREP_IN + 2]
    rhat_ref, y0_ref, m_ref, n_ref, bonus_ref, g_ref = refs[N_PREP_IN + 2:N_PREP_IN + 8]
    r_s, v_s, kk_s, lw_s, kd_s, b_s = refs[N_PREP_IN + 8:]
    _aprep_body(*prep_in, r_s, v_s, kk_s, lw_s, kd_s, b_s, bonus_ref, g_ref, row_len=row_len)
    _scan_pre_body(r_s, v_s, kk_s, lw_s, kd_s, b_s, tri_ref, msk_ref, rhat_ref, y0_ref, m_ref, n_ref, nchunk=nchunk)


def rwkv_maps(pa, ap, row_len, tri, msk):
    t = pa.shape[0]
    nchunk = 4
    tm = nchunk * CHUNK
    nc = t // CHUNK
    nb = BRANCH_W // LANES
    col = lambda c: pl.BlockSpec((tm, LANES), functools.partial(lambda i, p, c: (i, c + p), c=c))
    fix = lambda c: pl.BlockSpec((tm, LANES), functools.partial(lambda i, p, c: (i, c), c=c))
    mcol = lambda c: pl.BlockSpec((1, LANES), functools.partial(lambda i, p, c: (0, c + p), c=c))
    mfix = lambda c: pl.BlockSpec((1, LANES), functools.partial(lambda i, p, c: (0, c), c=c))
    prow = pl.BlockSpec((1, LANES), lambda i, p: (0, p))
    p2 = pl.BlockSpec((2, 1, LANES), lambda i, p: (0, 0, p))
    up2 = pl.BlockSpec((2, LANES, LANES), lambda i, p: (0, 0, p))
    out1 = pl.BlockSpec((tm, LANES), lambda i, p: (i, p))
    out2 = pl.BlockSpec((2, tm, LANES), lambda i, p: (0, i, p))
    mspec = pl.BlockSpec((2, nchunk, 1, HEAD, LANES), lambda i, p: (0, i, p, 0, 0))
    s1 = jax.ShapeDtypeStruct((t, BRANCH_W), F32)
    s2 = jax.ShapeDtypeStruct((2, t, BRANCH_W), BF16)
    sm = jax.ShapeDtypeStruct((2, nc, PAIRS, HEAD, LANES), F32)
    lora, gc = A_LORA // LANES, A_GC // LANES
    tile = lambda lead=(): pltpu.VMEM(lead + (tm, LANES), F32)
    return pl.pallas_call(
        functools.partial(_rwkv_maps_body, row_len=row_len, nchunk=nchunk),
        grid=(t // tm, nb),
        in_specs=[col(0), col(nb), col(2 * nb), fix(lora), fix(gc), fix(gc + 1),
                  mcol(0), mcol(nb), mcol(2 * nb), mfix(lora), mfix(gc), mfix(gc + 1),
                  p2, up2, p2, up2, up2, prow, prow, prow,
                  pl.BlockSpec((LANES, LANES), lambda i, p: (0, 0)),
                  pl.BlockSpec((2, CHUNK, CHUNK), lambda i, p: (0, 0, 0)),
                  pl.BlockSpec(msk.shape, lambda i, p: (0, 0, 0, 0))],
        out_specs=[out2, out2, mspec, mspec, out1, out1],
        out_shape=[s2, s2, sm, sm, s1, s1],
        scratch_shapes=[tile(), tile(), tile(), tile((2,)), tile((2,)), tile((2,))],
        compiler_params=_cparams(2, 24 << 20),
        name="rwkv_chunk_maps",
    )(pa, pa, pa, pa, pa, pa, ap["mu"], ap["mu"], ap["mu"], ap["mu"], ap["mu"], ap["mu"],
      ap["w0"], ap["w_up"], ap["a0"], ap["a_up"], ap["g_up"], ap["k_k"], ap["k_a"], ap["r_k"], ap["e2"], tri, msk)


def _pair_rows(x):
    lane = lax.broadcasted_iota(jnp.int32, x.shape, 1)
    return jnp.concatenate([jnp.where(lane < HEAD, x, 0.0), jnp.where(lane >= HEAD, x, 0.0)], axis=0)


MSK_STRICT, MSK_INCL, MSK_EYE, MSK_DIAG8, MSK_OFF8, MSK_OFF16, MSK_OFF32 = range(7)


def _chunk_mask_tables():
    n = 2 * CHUNK
    r = np.arange(n)[:, None]
    c = np.arange(n)[None, :]
    same = (r // CHUNK) == (c // CHUNK)
    out = np.zeros((2, 7, n, n), np.float32)
    for d in range(2):
        before = (c > r) if d else (c < r)
        out[d, MSK_STRICT] = same & before
        out[d, MSK_INCL] = same & (before | (r == c))
        out[d, MSK_EYE] = r == c
        out[d, MSK_DIAG8] = ((r // 8) == (c // 8)) & before
        for idx, s in ((MSK_OFF8, 8), (MSK_OFF16, 16), (MSK_OFF32, 32)):
            blk = (r // (2 * s)) == (c // (2 * s))
            rh, ch = (r // s) % 2, (c // s) % 2
            out[d, idx] = blk & ((rh == 0) & (ch == 1) if d else (rh == 1) & (ch == 0))
    return jnp.asarray(out)


def _scan_pre_body(r_ref, v_ref, kk_ref, lw_ref, kd_ref, b_ref, tri_ref, msk_ref, rhat_ref, y0_ref, m_ref, n_ref, *, nchunk):
    c, n = CHUNK, 2 * CHUNK
    chains = [(d, ci) for d in range(2) for ci in range(nchunk)]
    each = lambda f, *cols: [f(*xs) for xs in zip(*cols)]
    msk = lambda d, k: msk_ref[d, k]
    rows = lambda ci: slice(ci * c, (ci + 1) * c)

    lw = [lw_ref[d, rows(ci), :] for d, ci in chains]
    cum = [_dot_exact_lhs(tri_ref[d], x) for (d, _), x in zip(chains, lw)]
    tot = [x[0:1] if d else x[c - 1:c] for (d, _), x in zip(chains, cum)]
    g_inv = each(lambda x: jnp.exp(-x), cum)
    g_tail = each(lambda t, x: jnp.exp(t - x), tot, cum)
    atp = [_pair_rows(-kk_ref[rows(ci), :] * jnp.exp(x - l)) for (_, ci), x, l in zip(chains, cum, lw)]
    rtp = [_pair_rows(r_ref[rows(ci), :] * jnp.exp(x)) for (_, ci), x in zip(chains, cum)]
    btp = [_pair_rows(b_ref[d, rows(ci), :] * g) for (d, ci), g in zip(chains, g_inv)]
    ktp = [_pair_rows(kd_ref[d, rows(ci), :] * g) for (d, ci), g in zip(chains, g_inv)]
    bhp = [_pair_rows(b_ref[d, rows(ci), :] * g) for (d, ci), g in zip(chains, g_tail)]
    khp = [_pair_rows(kd_ref[d, rows(ci), :] * g) for (d, ci), g in zip(chains, g_tail)]
    vp = [_pair_rows(v_ref[rows(ci), :]) for _, ci in chains]

    big = each(lambda a, r, b, k: _dot_nt(jnp.concatenate([a, r], axis=0), jnp.concatenate([b, k], axis=0)),
               atp, rtp, btp, ktp)
    a_ab = [jnp.where(msk(d, MSK_STRICT) > 0.0, x[:n, :n], 0.0) for (d, _), x in zip(chains, big)]
    a_ak = [jnp.where(msk(d, MSK_STRICT) > 0.0, x[:n, n:], 0.0) for (d, _), x in zip(chains, big)]
    a_rb = [jnp.where(msk(d, MSK_INCL) > 0.0, x[n:, :n], 0.0) for (d, _), x in zip(chains, big)]
    a_rk = [jnp.where(msk(d, MSK_INCL) > 0.0, x[n:, n:], 0.0) for (d, _), x in zip(chains, big)]

    n8 = [x * msk(d, MSK_DIAG8) for (d, _), x in zip(chains, a_ab)]
    t = [msk(d, MSK_EYE) + x for (d, _), x in zip(chains, n8)]
    n2 = each(_dot, n8, n8)
    t = each(lambda x, p: x + _dot(p, x), t, n2)
    n4 = each(_dot, n2, n2)
    t = each(lambda x, p: x + _dot(p, x), t, n4)
    for idx in (MSK_OFF8, MSK_OFF16, MSK_OFF32):
        off = [x * msk(d, idx) for (d, _), x in zip(chains, a_ab)]
        tn = each(_dot, t, off)
        t = each(lambda x, p: x + _dot(p, x), t, tn)

    akv = each(_dot, a_ak, vp)
    w_u = each(lambda ti, a, x: _dot(ti, jnp.concatenate([a, x], axis=1)), t, atp, akv)
    ry = each(_dot, a_rb, w_u)
    rkv = each(_dot, a_rk, vp)
    mn = each(_dot_tn, bhp, w_u)
    kv = each(_dot_tn, khp, vp)
    for i, (d, ci) in enumerate(chains):
        rhat_p = rtp[i] + ry[i][:, :n]
        y0_p = ry[i][:, n:] + rkv[i]
        rhat_ref[d, rows(ci), :] = (rhat_p[:c] + rhat_p[c:]).astype(rhat_ref.dtype)
        y0_ref[d, rows(ci), :] = (y0_p[:c] + y0_p[c:]).astype(y0_ref.dtype)
        m_i = mn[i][:, :n] + msk(d, MSK_EYE) * jnp.exp(tot[i])
        n_i = mn[i][:, n:] + kv[i]
        m_ref[d, ci, 0] = m_i[:c] + m_i[c:]
        n_ref[d, ci, 0] = n_i[:c] + n_i[c:]


def _scan_seq_body(s0_ref, mf_ref, mb_ref, nf_ref, nb_ref, rf_ref, rb_ref, yf0_ref, yb0_ref,
                   yf_ref, yb_ref, sfin_ref, s_ref):
    c = pl.program_id(0)

    @pl.when(c == 0)
    def _():
        s_ref[...] = s0_ref[...]

    for sub in range(CHAIN_STEP):
        for d, (m_ref, n_ref, rh_ref, y0_ref, y_ref) in enumerate(((mf_ref, nf_ref, rf_ref, yf0_ref, yf_ref),
                                                                     (mb_ref, nb_ref, rb_ref, yb0_ref, yb_ref))):
            lc = sub if d == 0 else CHAIN_STEP - 1 - sub
            rows = slice(lc * CHUNK, (lc + 1) * CHUNK)
            for p in range(PAIRS):
                ln = slice(p * LANES, (p + 1) * LANES)
                s = s_ref[d, p]
                y_ref[rows, ln] = _dot(rh_ref[0, rows, ln], s) + y0_ref[0, rows, ln].astype(F32)
                s_ref[d, p] = _dot(_pair_rows(m_ref[0, lc, p]), s) + _pair_rows(n_ref[0, lc, p])

    @pl.when(c == pl.num_programs(0) - 1)
    def _():
        sfin_ref[...] = s_ref[...]


def scan_seq(s0, rhat, y0, m, n):
    t = rhat.shape[1]
    nc = t // (CHUNK * CHAIN_STEP)
    fwd5 = lambda c: (0, c, 0, 0, 0)
    bwd5 = lambda c: (1, nc - 1 - c, 0, 0, 0)
    mblk = (1, CHAIN_STEP, PAIRS, HEAD, LANES)
    rblk = (1, CHAIN_STEP * CHUNK, BRANCH_W)
    sblk = pl.BlockSpec((2, PAIRS, LANES, LANES), lambda c: (0, 0, 0, 0))
    ys = jax.ShapeDtypeStruct((t, BRANCH_W), F32)
    return pl.pallas_call(
        _scan_seq_body,
        grid=(nc,),
        in_specs=[sblk,
                  pl.BlockSpec(mblk, fwd5), pl.BlockSpec(mblk, bwd5),
                  pl.BlockSpec(mblk, fwd5), pl.BlockSpec(mblk, bwd5),
                  pl.BlockSpec(rblk, lambda c: (0, c, 0)), pl.BlockSpec(rblk, lambda c: (1, nc - 1 - c, 0)),
                  pl.BlockSpec(rblk, lambda c: (0, c, 0)), pl.BlockSpec(rblk, lambda c: (1, nc - 1 - c, 0))],
        out_specs=[pl.BlockSpec((CHAIN_STEP * CHUNK, BRANCH_W), lambda c: (c, 0)),
                   pl.BlockSpec((CHAIN_STEP * CHUNK, BRANCH_W), lambda c: (nc - 1 - c, 0)),
                   sblk],
        out_shape=[ys, ys, jax.ShapeDtypeStruct((2, PAIRS, LANES, LANES), F32)],
        scratch_shapes=[pltpu.VMEM((2, PAIRS, LANES, LANES), F32)],
        compiler_params=_cparams(1, 16 << 20),
        name="rwkv_chain",
    )(s0, m, m, n, n, rhat, rhat, y0, y0)


def _areadout_body(yf_ref, yb_ref, bonus_ref, g_ref, lng_ref, lnb_ref, e_ref, o_ref):
    e = e_ref[...]
    for p in range(BRANCH_W // LANES):
        ln = slice(p * LANES, (p + 1) * LANES)
        y = yf_ref[:, ln] + yb_ref[:, ln]
        mu = _dot_split_rhs(y, e) * (1.0 / HEAD)
        yc = y - mu
        var = _dot_split_rhs(yc * yc, e) * (1.0 / HEAD)
        yn = (yc * lax.rsqrt(var + A_GN_EPS)) * lng_ref[:, ln] + lnb_ref[:, ln]
        o_ref[:, ln] = ((yn + bonus_ref[:, ln]) * g_ref[:, ln]).astype(o_ref.dtype)


def a_readout(yf, yb, bonus, g, ap):
    t = yf.shape[0]
    tm = 256
    blk = pl.BlockSpec((tm, BRANCH_W), lambda i: (i, 0))
    prow = pl.BlockSpec((1, BRANCH_W), lambda i: (0, 0))
    return pl.pallas_call(
        _areadout_body,
        grid=(t // tm,),
        in_specs=[blk, blk, blk, blk, prow, prow, pl.BlockSpec((LANES, LANES), lambda i: (0, 0))],
        out_specs=blk,
        out_shape=jax.ShapeDtypeStruct((t, BRANCH_W), BF16),
        compiler_params=_cparams(1, 2 * 5 * tm * BRANCH_W * 4),
        name="rwkv_readout",
    )(yf, yb, bonus, g, ap["ln_g"], ap["ln_b"], ap["e2"])


def _bmix_body(u_ref, v_ref, lng_ref, lnb_ref, ws_ref, bs_ref, o_ref):
    u = _gelu_tanh(u_ref[...].astype(F32))
    v = _layernorm(_gelu_tanh(v_ref[...].astype(F32)), lng_ref[...], lnb_ref[...]).astype(BF16)
    for ci in range(u.shape[0] // B_CHUNK):
        rows = slice(ci * B_CHUNK, (ci + 1) * B_CHUNK)
        for g in range(B_GROUPS):
            ln = slice(g * LANES, (g + 1) * LANES)
            s = jnp.dot(ws_ref[g], v[rows, ln], preferred_element_type=F32) + bs_ref[:, ln]
            o_ref[rows, ln] = (u[rows, ln] * s).astype(o_ref.dtype)


def b_mix(pb, bp):
    t = pb.shape[0]
    tm = 256
    row = pl.BlockSpec((1, BRANCH_W), lambda i: (0, 0))
    return pl.pallas_call(
        _bmix_body,
        grid=(t // tm,),
        in_specs=[pl.BlockSpec((tm, BRANCH_W), lambda i: (i, 0)), pl.BlockSpec((tm, BRANCH_W), lambda i: (i, 1)),
                  row, row, pl.BlockSpec((B_GROUPS, B_CHUNK, B_CHUNK), lambda i: (0, 0, 0)),
                  pl.BlockSpec((B_CHUNK, BRANCH_W), lambda i: (0, 0))],
        out_specs=pl.BlockSpec((tm, BRANCH_W), lambda i: (i, 0)),
        out_shape=jax.ShapeDtypeStruct((t, BRANCH_W), BF16),
        compiler_params=_cparams(1, 12 * tm * BRANCH_W * 4),
        name="gmlp_mix",
    )(pb, pb, bp["ln_g"], bp["ln_b"], bp["ws"], bp["bs"])


def _conv_body(ac_ref, gc_ref, ap_ref, gp_ref, an_ref, gn_ref, dw_ref, dwb_ref, lng_ref, lnb_ref,
               o_ref, zs_ref, cs_ref, *, tm):
    i = pl.program_id(0)
    last = pl.num_programs(0) - 1
    glu = lambda a_ref, g_ref: a_ref[...].astype(F32) * _sigmoid(g_ref[...].astype(F32))
    zs_ref[0:CONV_HALO, :] = jnp.where(i == 0, 0.0, glu(ap_ref, gp_ref))
    zs_ref[CONV_HALO:CONV_HALO + tm, :] = glu(ac_ref, gc_ref)
    zs_ref[CONV_HALO + tm:, :] = jnp.where(i == last, 0.0, glu(an_ref, gn_ref))
    rb, lb = 32, 256
    win = rb + 2 * CONV_HALO
    first = CONV_HALO - C_KERNEL // 2
    for r0 in range(0, tm, rb):
        for l0 in range(0, BRANCH_W, lb):
            w = zs_ref[r0:r0 + win, l0:l0 + lb]
            acc = jnp.zeros((rb, lb), F32) + dwb_ref[:, l0:l0 + lb]
            for s in range(SUBLANES):
                ws = w if s == 0 else pltpu.roll(w, win - s, 0)
                for j in range(C_KERNEL):
                    if (first + j) % SUBLANES == s:
                        a = (first + j) - s
                        acc = acc + ws[a:a + rb] * dw_ref[j:j + 1, l0:l0 + lb]
            cs_ref[r0:r0 + rb, l0:l0 + lb] = acc
    o_ref[...] = _silu(_layernorm(cs_ref[...], lng_ref[...], lnb_ref[...])).astype(o_ref.dtype)


def conv_mix(pc, cp):
    t = pc.shape[0]
    tm = 256
    hb = tm // CONV_HALO
    nh = t // CONV_HALO
    cur = lambda c: pl.BlockSpec((tm, BRANCH_W), functools.partial(lambda i, c: (i, c), c=c))
    prev = lambda c: pl.BlockSpec((CONV_HALO, BRANCH_W), functools.partial(lambda i, c: (jnp.maximum(i * hb - 1, 0), c), c=c))
    nxt = lambda c: pl.BlockSpec((CONV_HALO, BRANCH_W), functools.partial(lambda i, c: (jnp.minimum((i + 1) * hb, nh - 1), c), c=c))
    row = pl.BlockSpec((1, BRANCH_W), lambda i: (0, 0))
    return pl.pallas_call(
        functools.partial(_conv_body, tm=tm),
        grid=(t // tm,),
        in_specs=[cur(0), cur(1), prev(0), prev(1), nxt(0), nxt(1),
                  pl.BlockSpec((C_KERNEL + 1, BRANCH_W), lambda i: (0, 0)), row, row, row],
        out_specs=pl.BlockSpec((tm, BRANCH_W), lambda i: (i, 0)),
        out_shape=jax.ShapeDtypeStruct((t, BRANCH_W), BF16),
        scratch_shapes=[pltpu.VMEM((tm + 2 * CONV_HALO, BRANCH_W), F32), pltpu.VMEM((tm, BRANCH_W), F32)],
        compiler_params=_cparams(1, 16 * tm * BRANCH_W * 4),
        name="conv_mix",
    )(pc, pc, pc, pc, pc, pc, cp["dw"], cp["dw_b"], cp["ln_g"], cp["ln_b"])


FFT_N1, FFT_N2 = 64, 128


def _dft_tables(t):
    two_pi = 2.0 * np.pi
    cidx = np.arange(D_GROUP_CH)
    ph = two_pi * np.outer(cidx, cidx) / D_GROUP_CH
    chan = np.concatenate([np.cos(ph), np.sin(ph)], axis=0)
    if t <= 256:
        n = np.arange(t)
        th = two_pi * np.outer(n, n) / t
        m2 = np.concatenate([np.cos(th), -np.sin(th)], axis=0)
        return None, m2.astype(np.float32), chan.astype(np.float32)
    n1, n2 = FFT_N1, FFT_N2
    assert t == n1 * n2
    k1 = np.arange(n1)
    tok = (n2 * np.arange(n1))[None, None, :] + np.arange(n2)[:, None, None]
    th = two_pi * (k1[None, :, None] * tok) / t
    g1 = np.concatenate([np.cos(th), -np.sin(th)], axis=1)
    q = np.arange(n2)
    th2 = two_pi * np.outer(q, q) / n2
    c2, s2 = np.cos(th2), np.sin(th2)
    m2 = np.block([[c2, s2], [-s2, c2]])
    return g1.astype(np.float32), m2.astype(np.float32), chan.astype(np.float32)


FFT_STEP = 4


def _fft1_body(x_ref, g_ref, o_ref):
    for q in range(FFT_STEP):
        ln = slice(q * BRANCH_W, (q + 1) * BRANCH_W)
        o_ref[:, ln] = _dot(g_ref[q], x_ref[:, ln])


def _fft2_body(z_ref, m2_ref, ch_ref, o_ref, *, n_out, scale, stacked):
    z = jnp.concatenate([z_ref[0], z_ref[1]], axis=0) if stacked else z_ref[...]
    x = _dot(m2_ref[...], z)
    xr, xi = x[:n_out], x[n_out:]
    for g in range(D_GROUPS):
        ln = slice(g * D_GROUP_CH, (g + 1) * D_GROUP_CH)
        f = _dot(xr[:, ln], ch_ref[0:D_GROUP_CH]) + _dot(xi[:, ln], ch_ref[D_GROUP_CH:])
        o_ref[:, ln] = (f * scale).astype(o_ref.dtype)


def fourier_mix(pd):
    t = pd.shape[0]
    g1, m2, chan = (None if a is None else jnp.asarray(a) for a in _dft_tables(t))
    scale = 1.0 / math.sqrt(t * D_GROUP_CH)
    chspec = pl.BlockSpec((2 * D_GROUP_CH, D_GROUP_CH), lambda i: (0, 0))
    if g1 is None:
        return pl.pallas_call(
            functools.partial(_fft2_body, n_out=t, scale=scale, stacked=False),
            grid=(1,),
            in_specs=[pl.BlockSpec((t, BRANCH_W), lambda i: (0, 0)), pl.BlockSpec((2 * t, t), lambda i: (0, 0)), chspec],
            out_specs=pl.BlockSpec((t, BRANCH_W), lambda i: (0, 0)),
            out_shape=jax.ShapeDtypeStruct((t, BRANCH_W), BF16),
            compiler_params=_cparams(1, 16 << 20),
            name="fourier_small",
        )(pd, m2, chan)
    n1, n2 = FFT_N1, FFT_N2
    z = pl.pallas_call(
        _fft1_body,
        grid=(n2 // FFT_STEP,),
        in_specs=[pl.BlockSpec((n1, FFT_STEP * BRANCH_W), lambda q: (0, q)),
                  pl.BlockSpec((FFT_STEP, 2 * n1, n1), lambda q: (q, 0, 0))],
        out_specs=pl.BlockSpec((2 * n1, FFT_STEP * BRANCH_W), lambda q: (0, q)),
        out_shape=jax.ShapeDtypeStruct((2 * n1, n2 * BRANCH_W), F32),
        compiler_params=_cparams(1, 8 << 20),
        name="fourier_stage1",
    )(pd.reshape(n1, n2 * BRANCH_W), g1)
    f = pl.pallas_call(
        functools.partial(_fft2_body, n_out=n2, scale=scale, stacked=True),
        grid=(n1,),
        in_specs=[pl.BlockSpec((2, None, n2, BRANCH_W), lambda k: (0, k, 0, 0)),
                  pl.BlockSpec((2 * n2, 2 * n2), lambda k: (0, 0)), chspec],
        out_specs=pl.BlockSpec((n2, BRANCH_W), lambda k: (0, k)),
        out_shape=jax.ShapeDtypeStruct((n2, n1 * BRANCH_W), BF16),
        compiler_params=_cparams(1, 16 << 20),
        name="fourier_stage2",
    )(z.reshape(2, n1, n2, BRANCH_W), m2, chan)
    return f.reshape(t, BRANCH_W)


def _align_a(x):
    return jnp.pad(x, [(0, 0)] * (x.ndim - 1) + [(0, A_PAD - x.shape[-1])])


def _pad_rows(x, before, rows):
    return jnp.pad(x, [(0, 0)] * (x.ndim - 2) + [(before, rows - before - x.shape[-2]), (0, 0)])


def _layer_params(l, w_in, a_mu, a_w0, a_w_up, a_a0, a_a_up, a_g_up, a_k_k, a_k_a, a_r_k, a_ln, a_w_out,
                  b_ln, b_ws, b_bs, b_w_out, c_dw, c_dw_b, c_ln, c_w_out, d_w_out, gate_w, gate_b, w_o,
                  ffn_w1, ffn_w3, ffn_w2):
    row = lambda v: v.reshape(1, -1)
    hid = np.arange(LANES) // HEAD
    e2 = jnp.asarray((hid[:, None] == hid[None, :]).astype(np.float32)).astype(BF16)
    g_up = jnp.stack([a_g_up[l][:LANES], _pad_rows(a_g_up[l][LANES:], 0, LANES)])
    ap = dict(mu=_align_a(row(a_mu[l])), w0=a_w0[l][:, None, :], w_up=_pad_rows(a_w_up[l], 0, LANES),
              a0=a_a0[l][:, None, :], a_up=_pad_rows(a_a_up[l], LORA_W, LANES), g_up=g_up,
              k_k=row(a_k_k[l]), k_a=row(a_k_a[l]), r_k=row(a_r_k[l]), ln_g=row(a_ln[l][0]), ln_b=row(a_ln[l][1]), e2=e2)
    bs_exp = jnp.repeat(jnp.swapaxes(b_bs[l], 0, 1), LANES, axis=1)
    return dict(
        layer=l, w_a=w_in, w_bcd=w_in[l][:, A_PROJ:], ap=ap,
        bp=dict(ln_g=row(b_ln[l][0]), ln_b=row(b_ln[l][1]), ws=b_ws[l].astype(BF16), bs=bs_exp),
        cp=dict(dw=_pad_rows(c_dw[l], 0, C_KERNEL + 1), dw_b=row(c_dw_b[l]), ln_g=row(c_ln[l][0]), ln_b=row(c_ln[l][1])),
        w_outs=[a_w_out, b_w_out, c_w_out, d_w_out],
        gate_w=gate_w, gate_b=row(gate_b[l]), w_o=w_o,
        w1=ffn_w1, w3=ffn_w3, w2=ffn_w2)


def _tri_tables():
    i = np.arange(CHUNK)
    lower = (i[None, :] <= i[:, None]).astype(np.float32)
    return jnp.asarray(np.stack([lower, lower.T])).astype(BF16)


def _rwkv_scan(pa, s0, ap, row_len):
    rhat, y0, m, n, bonus, g = rwkv_maps(pa, ap, row_len, _tri_tables(), _chunk_mask_tables())
    yf, yb, s_fin = scan_seq(s0, rhat, y0, m, n)
    return yf, yb, bonus, g, s_fin


def _in_proj_a(h, lp):
    return matmul(h, lp["w_a"], A_PAD, 0, 512, F32, "in_proj_a", layer=lp["layer"])


def _in_proj_bcd(h, lp, cast_merge_weights=False):
    tn, l, w = 512, lp["layer"], lp["w_bcd"]
    if not cast_merge_weights:
        return (matmul(h, w, 2 * BRANCH_W, 0, tn, BF16, "in_proj_b"),
                matmul(h, w, 2 * BRANCH_W, 2 * BRANCH_W, tn, BF16, "in_proj_c"),
                matmul(h, w, BRANCH_W, 4 * BRANCH_W, tn, BF16, "in_proj_d")), None
    half = D_MODEL // 2
    pb, (g_top,) = matmul(h, w, 2 * BRANCH_W, 0, tn, BF16, "in_proj_b", sides=[(lp["gate_w"], l, 0, half)])
    pc, (g_bot,) = matmul(h, w, 2 * BRANCH_W, 2 * BRANCH_W, tn, BF16, "in_proj_c", sides=[(lp["gate_w"], l, half, half)])
    pd, w_outs = matmul(h, w, BRANCH_W, 4 * BRANCH_W, tn, BF16, "in_proj_d",
                        sides=[(wo, l, 0, BRANCH_W) for wo in lp["w_outs"]])
    return (pb, pc, pd), ((g_top, g_bot), list(w_outs))


def _token_mix(h, s0, row_len, lp, pbcd, merge_w):
    pa = _in_proj_a(h, lp)
    yf, yb, bonus, g, s_fin = _rwkv_scan(pa, s0, lp["ap"], row_len)
    za = a_readout(yf, yb, bonus, g, lp["ap"])
    zb = b_mix(pbcd[0], lp["bp"])
    zc = conv_mix(pbcd[1], lp["cp"])
    zd = fourier_mix(pbcd[2])
    merged = merge(h, [za, zb, zc, zd], merge_w[0], lp["gate_b"], merge_w[1])
    mix = matmul(merged, lp["w_o"], D_MODEL, 0, 512, BF16, "out_proj", layer=lp["layer"])
    return mix, s_fin


def _ffn(h2, lp, w2_bf16=None):
    if w2_bf16 is None:
        u, w2_bf16 = ffn_up(h2, lp["w1"], lp["w3"], lp["layer"], lp["w2"])
    else:
        u = ffn_up(h2, lp["w1"], lp["w3"], lp["layer"])
    return matmul(u, w2_bf16, D_MODEL, 0, 512, BF16, "ffn_down"), w2_bf16


def kernel(x, c, ctx, c_ctx, mod_w, mod_b, norm_g, w_in, a_mu, a_w0, a_w_up, a_a0, a_a_up, a_g_up, a_k_k, a_k_a,
           a_r_k, a_ln, a_w_out, b_ln, b_ws, b_bs, b_w_out, c_dw, c_dw_b, c_ln, c_w_out, d_w_out, gate_w, gate_b,
           w_o, ffn_w1, ffn_w3, ffn_w2):
    depth = mod_w.shape[0]
    d = D_MODEL
    c_cols = jnp.concatenate([c.reshape(d, 1), c_ctx.reshape(d, 1), jnp.zeros((d, 6), F32)], axis=1)
    mods = modulation(c_cols, mod_w, mod_b.reshape(depth, 1, 6 * d))
    x_lat, x_ctx = x[0], ctx[0]
    zero_state = jnp.zeros((2, PAIRS, LANES, LANES), F32)
    weights = (w_in.astype(BF16), a_mu, a_w0, a_w_up, a_a0, a_a_up, a_g_up, a_k_k, a_k_a, a_r_k, a_ln, a_w_out, b_ln, b_ws, b_bs,
               b_w_out, c_dw, c_dw_b, c_ln, c_w_out, d_w_out, gate_w, gate_b, w_o, ffn_w1, ffn_w3, ffn_w2)
    h_lat = h_ctx = None
    for l in range(depth):
        last = l == depth - 1
        lp = _layer_params(l, *weights)
        ng = [norm_g[l, i].reshape(1, d) for i in range(4)]
        ml = [mods[l, 0:1, i * d:(i + 1) * d] for i in range(6)]
        mc = [mods[l, 1:2, i * d:(i + 1) * d] for i in range(6)]
        if l == 0:
            h_lat = normmod(x_lat, ng[0], ml[0], ml[1])
            h_ctx = normmod(x_ctx, ng[0], mc[0], mc[1])

        pbcd_lat, merge_w = _in_proj_bcd(h_lat, lp, cast_merge_weights=True)

        if last:
            ctx_states = _rwkv_scan(_in_proj_a(h_ctx, lp), zero_state, lp["ap"], x_ctx.shape[0])[4]
        else:
            pbcd_ctx, _ = _in_proj_bcd(h_ctx, lp)
            mix_ctx, ctx_states = _token_mix(h_ctx, zero_state, x_ctx.shape[0], lp, pbcd_ctx, merge_w)

        mix_lat, _ = _token_mix(h_lat, ctx_states, GRID_W, lp, pbcd_lat, merge_w)
        x_lat, h2 = resnorm(x_lat, mix_lat, ml[2], ng[1], (ng[2], ml[3], ml[4]))
        y, w2_bf16 = _ffn(h2, lp)
        if last:
            x_lat = resnorm(x_lat, y, ml[5], ng[3])
        else:
            ngn = norm_g[l + 1, 0].reshape(1, d)
            mln = [mods[l + 1, 0:1, i * d:(i + 1) * d] for i in range(2)]
            mcn = [mods[l + 1, 1:2, i * d:(i + 1) * d] for i in range(2)]
            x_lat, h_lat = resnorm(x_lat, y, ml[5], ng[3], (ngn, mln[0], mln[1]))
            x_ctx, h2c = resnorm(x_ctx, mix_ctx, mc[2], ng[1], (ng[2], mc[3], mc[4]))
            yc, _ = _ffn(h2c, lp, w2_bf16)
            x_ctx, h_ctx = resnorm(x_ctx, yc, mc[5], ng[3], (ngn, mcn[0], mcn[1]))
    return x_lat[None]
```

```python
import functools
import math

import numpy as np
import jax
import jax.numpy as jnp
from jax import lax
from jax.experimental import pallas as pl
from jax.experimental.pallas import tpu as pltpu

F32, BF16 = jnp.float32, jnp.bfloat16

D_MODEL = 4096
DEPTH = 2
GRID_W = 64
BRANCH_W = 1024
HEAD = 64
HEADS = BRANCH_W // HEAD
PAIRS = HEADS // 2
LORA_W, LORA_A, LORA_G = 64, 64, 160
A_PROJ = 3 * BRANCH_W + LORA_W + LORA_A + LORA_G
A_GN_EPS = 64e-5
B_CHUNK = 128
B_GROUPS = 8
C_KERNEL = 31
D_GROUPS = 4
D_GROUP_CH = BRANCH_W // D_GROUPS
FFN_HIDDEN = 11008
RMS_EPS = 1e-6
LN_EPS = 1e-5

LANES = 128
SUBLANES = 8
VMEM_BUDGET = 60 * 1024 * 1024

A_LORA = 3 * BRANCH_W
A_GC = A_LORA + LANES
A_PAD = A_GC + 3 * LANES

MERGE_TM = 512
FFN_UP_TM = 2048
CHUNK = 64
CHAIN_STEP = 2
CONV_HALO = 16


def _cparams(n_axes, vmem_bytes):
    limit = int(min(max(vmem_bytes + (8 << 20), 32 << 20), VMEM_BUDGET))
    return pltpu.CompilerParams(dimension_semantics=("arbitrary",) * n_axes, vmem_limit_bytes=limit)


def _dot(a, b):
    return jnp.dot(a.astype(BF16), b.astype(BF16), preferred_element_type=F32)


def _dot_nt(a, b):
    return lax.dot_general(a.astype(BF16), b.astype(BF16), (((1,), (1,)), ((), ())), preferred_element_type=F32)


def _dot_tn(a, b):
    return lax.dot_general(a.astype(BF16), b.astype(BF16), (((0,), (0,)), ((), ())), preferred_element_type=F32)


def _split3(x):
    hi = x.astype(BF16)
    r1 = x - hi.astype(F32)
    mid = r1.astype(BF16)
    lo = (r1 - mid.astype(F32)).astype(BF16)
    return hi, mid, lo


def _dot_split_rhs(x, e):
    hi = x.astype(BF16)
    lo = (x - hi.astype(F32)).astype(BF16)
    f = lambda p: jnp.dot(p, e, preferred_element_type=F32)
    return f(hi) + f(lo)


def _dot_exact_lhs(e, x):
    hi, mid, lo = _split3(x)
    f = lambda p: jnp.dot(e, p, preferred_element_type=F32)
    return f(hi) + f(mid) + f(lo)


def _sigmoid(x):
    return 1.0 / (1.0 + jnp.exp(-x))


def _silu(x):
    return x * _sigmoid(x)


def _softplus(x):
    return jnp.maximum(x, 0.0) + jnp.log(1.0 + jnp.exp(-jnp.abs(x)))


def _gelu_tanh(x):
    return 0.5 * x * (1.0 + jnp.tanh(math.sqrt(2.0 / math.pi) * (x + 0.044715 * (x * x * x))))


def _rmsnorm(x, g):
    return (x * lax.rsqrt(jnp.mean(x * x, axis=-1, keepdims=True) + RMS_EPS)) * g


def _layernorm(x, g, b):
    mu = jnp.mean(x, axis=-1, keepdims=True)
    xc = x - mu
    var = jnp.mean(xc * xc, axis=-1, keepdims=True)
    return (xc * lax.rsqrt(var + LN_EPS)) * g + b


MOD_ROWS = 32
MOD_COLS = 512


def _mod_body(c_ref, w_ref, b_ref, o_ref, s0_ref, s1_ref):
    d, tn = w_ref.shape[1], w_ref.shape[2]

    @pl.when((pl.program_id(0) == 0) & (pl.program_id(1) == 0))
    def _():
        s = _silu(c_ref[...])
        s0_ref[...] = jnp.broadcast_to(s[:, 0:1], (d, LANES))
        s1_ref[...] = jnp.broadcast_to(s[:, 1:2], (d, LANES))

    for c0 in range(0, tn, MOD_COLS):
        def step(k, acc):
            r0 = pl.multiple_of(k * MOD_ROWS, MOD_ROWS)
            w = w_ref[0, pl.ds(r0, MOD_ROWS), c0:c0 + MOD_COLS]
            s0 = jnp.concatenate([s0_ref[pl.ds(r0, MOD_ROWS), :]] * (MOD_COLS // LANES), axis=1)
            s1 = jnp.concatenate([s1_ref[pl.ds(r0, MOD_ROWS), :]] * (MOD_COLS // LANES), axis=1)
            return acc[0] + w * s0, acc[1] + w * s1

        z = jnp.zeros((MOD_ROWS, MOD_COLS), F32)
        a0, a1 = lax.fori_loop(0, d // MOD_ROWS, step, (z, z), unroll=2)
        bias = b_ref[0, :, c0:c0 + MOD_COLS]
        o_ref[0, :, c0:c0 + MOD_COLS] = jnp.concatenate(
            [jnp.sum(a0, axis=0, keepdims=True) + bias, jnp.sum(a1, axis=0, keepdims=True) + bias,
             jnp.zeros((6, MOD_COLS), F32)], axis=0)


def modulation(c_cols, mod_w, mod_b):
    depth, d, n = mod_w.shape
    tn = 1024
    return pl.pallas_call(
        _mod_body,
        grid=(depth, n // tn),
        in_specs=[pl.BlockSpec((d, 8), lambda l, j: (0, 0)),
                  pl.BlockSpec((1, d, tn), lambda l, j: (l, 0, j)),
                  pl.BlockSpec((1, 1, tn), lambda l, j: (l, 0, j))],
        out_specs=pl.BlockSpec((1, 8, tn), lambda l, j: (l, 0, j)),
        out_shape=jax.ShapeDtypeStruct((depth, 8, n), F32),
        scratch_shapes=[pltpu.VMEM((d, LANES), F32), pltpu.VMEM((d, LANES), F32)],
        compiler_params=_cparams(2, 2 * d * tn * 4 + 5 * d * LANES * 4),
        name="modulation",
    )(c_cols, mod_w, mod_b)


def _normmod_body(x_ref, g_ref, sh_ref, sc_ref, h_ref):
    h = _rmsnorm(x_ref[...], g_ref[...]) * (1.0 + sc_ref[...]) + sh_ref[...]
    h_ref[...] = h.astype(h_ref.dtype)


def normmod(x, g, shift, scale):
    t, d = x.shape
    tm = 256
    row = pl.BlockSpec((1, d), lambda i: (0, 0))
    return pl.pallas_call(
        _normmod_body,
        grid=(t // tm,),
        in_specs=[pl.BlockSpec((tm, d), lambda i: (i, 0)), row, row, row],
        out_specs=pl.BlockSpec((tm, d), lambda i: (i, 0)),
        out_shape=jax.ShapeDtypeStruct((t, d), BF16),
        compiler_params=_cparams(1, 2 * tm * d * 6),
        name="normmod",
    )(x, g, shift, scale)


def _resnorm_body(x_ref, y_ref, gate_ref, gpost_ref, gpre_ref, sh_ref, sc_ref, xo_ref, h_ref):
    xn = x_ref[...] + gate_ref[...] * _rmsnorm(y_ref[...].astype(F32), gpost_ref[...])
    xo_ref[...] = xn
    h = _rmsnorm(xn, gpre_ref[...]) * (1.0 + sc_ref[...]) + sh_ref[...]
    h_ref[...] = h.astype(h_ref.dtype)


def _res_body(x_ref, y_ref, gate_ref, gpost_ref, xo_ref):
    xo_ref[...] = x_ref[...] + gate_ref[...] * _rmsnorm(y_ref[...].astype(F32), gpost_ref[...])


def resnorm(x, y, gate, g_post, nxt=None):
    t, d = x.shape
    tm = 256
    row = pl.BlockSpec((1, d), lambda i: (0, 0))
    tile = pl.BlockSpec((tm, d), lambda i: (i, 0))
    if nxt is None:
        return pl.pallas_call(
            _res_body, grid=(t // tm,), in_specs=[tile, tile, row, row], out_specs=tile,
            out_shape=jax.ShapeDtypeStruct((t, d), F32),
            compiler_params=_cparams(1, 2 * tm * d * 12), name="residual",
        )(x, y, gate, g_post)
    return pl.pallas_call(
        _resnorm_body, grid=(t // tm,), in_specs=[tile, tile, row, row, row, row, row],
        out_specs=[tile, tile],
        out_shape=[jax.ShapeDtypeStruct((t, d), F32), jax.ShapeDtypeStruct((t, d), BF16)],
        compiler_params=_cparams(1, 2 * tm * d * 14), name="residual_norm",
    )(x, y, gate, g_post, *nxt)


def _mm_body(x_ref, w_ref, *rest, n_side, cast):
    side_in, o_ref, side_out = rest[:n_side], rest[n_side], rest[n_side + 1:2 * n_side + 1]
    for s_in, s_out in zip(side_in, side_out):
        s_out[...] = s_in[...].astype(BF16)
    if cast:
        wb_ref = rest[-1]

        @pl.when(pl.program_id(1) == 0)
        def _():
            wb_ref[...] = w_ref[...].astype(BF16)

        w = wb_ref[...]
    else:
        w = w_ref[...]
    o_ref[...] = jnp.dot(x_ref[...], w, preferred_element_type=F32).astype(o_ref.dtype)


def _wspec(w, layer, rows, tn, col_block):
    if w.ndim == 3:
        return pl.BlockSpec((None, rows, tn), lambda j, i: (layer, 0, col_block(j)))
    return pl.BlockSpec((rows, tn), lambda j, i: (0, col_block(j)))


def _side_specs(sides, nj, ni):
    side_in, side_out, side_shape, side_args, vm = [], [], [], [], 0
    for arr, lyr, row0, nrows in sides:
        rps, cols = nrows // (nj * ni), arr.shape[2]
        assert rps * nj * ni == nrows and rps % 16 == 0 and row0 % rps == 0
        side_in.append(pl.BlockSpec((None, rps, cols),
                                    functools.partial(lambda j, i, lyr, b0: (lyr, b0 + j * ni + i, 0), lyr=lyr, b0=row0 // rps)))
        side_out.append(pl.BlockSpec((rps, cols), lambda j, i: (j * ni + i, 0)))
        side_shape.append(jax.ShapeDtypeStruct((nrows, cols), BF16))
        side_args.append(arr)
        vm += 2 * rps * cols * 6
    return side_in, side_out, side_shape, side_args, vm


def matmul(x, w, n, col0, tn, out_dtype, name, layer=None, sides=()):
    m, k = x.shape
    tm = min(m, 1024 if k <= 4096 else 512)
    j0 = col0 // tn
    assert col0 % tn == 0 and n % tn == 0 and m % tm == 0
    nj, ni = n // tn, m // tm
    cast = w.dtype == F32
    vm = 2 * (tm * k * 2 + k * tn * w.dtype.itemsize + tm * tn * 4) + (k * tn * 2 if cast else 0)
    side_in, side_out, side_shape, side_args, side_vm = _side_specs(sides, nj, ni)
    vm += side_vm
    out = pl.pallas_call(
        functools.partial(_mm_body, n_side=len(sides), cast=cast),
        grid=(nj, ni),
        in_specs=[pl.BlockSpec((tm, k), lambda j, i: (i, 0)),
                  _wspec(w, layer, k, tn, lambda j: j + j0)] + side_in,
        out_specs=[pl.BlockSpec((tm, tn), lambda j, i: (i, j))] + side_out,
        out_shape=[jax.ShapeDtypeStruct((m, n), out_dtype)] + side_shape,
        scratch_shapes=[pltpu.VMEM((k, tn), BF16)] if cast else [],
        compiler_params=_cparams(2, vm),
        name=name,
    )(x, w, *side_args)
    return (out[0], out[1:]) if sides else out[0]


def _ffn_up_body(x_ref, w1_ref, w3_ref, *rest, n_side):
    side_in, o_ref, side_out = rest[:n_side], rest[n_side], rest[n_side + 1:]
    for s_in, s_out in zip(side_in, side_out):
        s_out[...] = s_in[...].astype(BF16)
    x = x_ref[...]
    a = jnp.dot(x, w1_ref[...], preferred_element_type=F32)
    b = jnp.dot(x, w3_ref[...], preferred_element_type=F32)
    o_ref[...] = (_silu(a) * b).astype(o_ref.dtype)


def ffn_up(h, w1, w3, sides=()):
    m, k = h.shape
    n = w1.shape[-1]
    tm, tn = min(m, FFN_UP_TM), 256
    nj, ni = n // tn, m // tm
    vm = 2 * (tm * k * 2 + 2 * k * tn * 2 + tm * tn * 2) + 5 * tm * tn * 4
    side_in, side_out, side_shape, side_args, side_vm = _side_specs(sides, nj, ni)
    out = pl.pallas_call(
        functools.partial(_ffn_up_body, n_side=len(sides)),
        grid=(nj, ni),
        in_specs=[pl.BlockSpec((tm, k), lambda j, i: (i, 0)),
                  pl.BlockSpec((k, tn), lambda j, i: (0, j)),
                  pl.BlockSpec((k, tn), lambda j, i: (0, j))] + side_in,
        out_specs=[pl.BlockSpec((tm, tn), lambda j, i: (i, j))] + side_out,
        out_shape=[jax.ShapeDtypeStruct((m, n), BF16)] + side_shape,
        compiler_params=_cparams(2, vm + side_vm),
        name="ffn_up",
    )(h, w1, w3, *side_args)
    return (out[0], out[1:]) if sides else out[0]


def _merge_body(h_ref, *refs, n_side):
    z, g_top, g_bot, bias, w_out = refs[0:4], refs[4:8], refs[8:12], refs[12:16], refs[16:20]
    side_in, o_ref, side_out = refs[20:20 + n_side], refs[20 + n_side], refs[21 + n_side:]
    for s_in, s_out in zip(side_in, side_out):
        s_out[...] = s_in[...].astype(BF16)
    half = g_top[0].shape[0]
    h_top, h_bot = h_ref[:, :half], h_ref[:, half:]
    acc = None
    for br in range(4):
        logits = (jnp.dot(h_top, g_top[br][...], preferred_element_type=F32)
                  + jnp.dot(h_bot, g_bot[br][...], preferred_element_type=F32) + bias[br][...])
        y = jnp.dot(z[br][...], w_out[br][...], preferred_element_type=F32)
        acc = _sigmoid(logits) * y if acc is None else acc + _sigmoid(logits) * y
    o_ref[...] = acc.astype(o_ref.dtype)


def merge(h, zs, gate_halves, gate_b, w_outs, sides=()):
    m, d = h.shape
    bw = zs[0].shape[1]
    half = gate_halves[0].shape[0]
    tm, tn = min(m, MERGE_TM), 256
    nj = d // tn
    hspec = pl.BlockSpec((tm, d), lambda j, i: (i, 0))
    zspec = pl.BlockSpec((tm, bw), lambda j, i: (i, 0))
    gspecs = [pl.BlockSpec((half, tn), functools.partial(lambda j, i, br: (0, br * nj + j), br=br)) for br in range(4)]
    bspecs = [pl.BlockSpec((1, tn), functools.partial(lambda j, i, br: (0, br * nj + j), br=br)) for br in range(4)]
    wspec = pl.BlockSpec((bw, tn), lambda j, i: (0, j))
    vm = 2 * (tm * d * 2 + 4 * tm * bw * 2 + tm * tn * 2 + 4 * (d + bw) * tn * 2) + 3 * tm * tn * 4
    side_in, side_out, side_shape, side_args, side_vm = _side_specs(sides, nj, m // tm)
    out = pl.pallas_call(
        functools.partial(_merge_body, n_side=len(sides)),
        grid=(nj, m // tm),
        in_specs=[hspec] + [zspec] * 4 + gspecs + gspecs + bspecs + [wspec] * 4 + side_in,
        out_specs=[pl.BlockSpec((tm, tn), lambda j, i: (i, j))] + side_out,
        out_shape=[jax.ShapeDtypeStruct((m, d), BF16)] + side_shape,
        compiler_params=_cparams(2, vm + side_vm),
        name="merge",
    )(h, *zs, *([gate_halves[0]] * 4), *([gate_halves[1]] * 4), gate_b, gate_b, gate_b, gate_b, *w_outs, *side_args)
    return (out[0], out[1:]) if sides else out[0]


def _token_shift(x, mu, row_len):
    tm = x.shape[0]
    pos = lax.broadcasted_iota(jnp.int32, x.shape, 0) & (row_len - 1)
    prev = jnp.where(pos == 0, 0.0, pltpu.roll(x, 1, 0))
    nxt = jnp.where(pos == row_len - 1, 0.0, pltpu.roll(x, tm - 1, 0))
    return x + mu * (0.5 * (prev + nxt) - x)


def _aprep_body(r_ref, k_ref, v_ref, wa_ref, g1_ref, g2_ref,
                mur_ref, muk_ref, muv_ref, muwa_ref, mug1_ref, mug2_ref,
                w0_ref, wup_ref, a0_ref, aup_ref, gup_ref, kk_ref, ka_ref, rk_ref, e_ref,
                ro_ref, vo_ref, kko_ref, lw_ref, kd_ref, b_ref, bonus_ref, g_ref, *, row_len):
    e = e_ref[...]
    xr = _token_shift(r_ref[...], mur_ref[...], row_len)
    xk = _token_shift(k_ref[...], muk_ref[...], row_len)
    xv = _token_shift(v_ref[...], muv_ref[...], row_len)
    xwa = _token_shift(wa_ref[...], muwa_ref[...], row_len)
    xg1 = _token_shift(g1_ref[...], mug1_ref[...], row_len)
    xg2 = _token_shift(g2_ref[...], mug2_ref[...], row_len)
    kk = xk * kk_ref[...]
    kk = kk * lax.rsqrt(jnp.maximum(_dot_split_rhs(kk * kk, e), 1e-12))
    tw = jnp.tanh(xwa)
    bonus = None
    for d in range(2):
        w = -_softplus(-(w0_ref[d] + _dot(tw, wup_ref[d]))) - 0.5
        a = _sigmoid(a0_ref[d] + _dot(xwa, aup_ref[d]))
        kd = xk * (1.0 + (a - 1.0) * ka_ref[...])
        lw_ref[d] = -jnp.exp(w)
        kd_ref[d] = kd
        b_ref[d] = kk * a
        bn = _dot_split_rhs(xr * kd * rk_ref[...], e) * xv
        bonus = bn if bonus is None else bonus + bn
    ro_ref[...] = xr
    vo_ref[...] = xv
    kko_ref[...] = kk
    bonus_ref[...] = bonus
    g_ref[...] = _dot(_sigmoid(xg1), gup_ref[0]) + _dot(_sigmoid(xg2), gup_ref[1])


N_PREP_IN = 21


def _rwkv_maps_body(*refs, row_len, nchunk):
    prep_in, (tri_ref, msk_ref) = refs[:N_PREP_IN], refs[N_PREP_IN:N_PREP_IN + 2]
    rhat_ref, y0_ref, m_ref, n_ref, bonus_ref, g_ref = refs[N_PREP_IN + 2:N_PREP_IN + 8]
    r_s, v_s, kk_s, lw_s, kd_s, b_s = refs[N_PREP_IN + 8:]
    _aprep_body(*prep_in, r_s, v_s, kk_s, lw_s, kd_s, b_s, bonus_ref, g_ref, row_len=row_len)
    _scan_pre_body(r_s, v_s, kk_s, lw_s, kd_s, b_s, tri_ref, msk_ref, rhat_ref, y0_ref, m_ref, n_ref, nchunk=nchunk)


def rwkv_maps(pa, ap, row_len, tri, msk):
    t = pa.shape[0]
    nchunk = 4
    tm = nchunk * CHUNK
    nc = t // CHUNK
    nb = BRANCH_W // LANES
    col = lambda c: pl.BlockSpec((tm, LANES), functools.partial(lambda i, p, c: (i, c + p), c=c))
    fix = lambda c: pl.BlockSpec((tm, LANES), functools.partial(lambda i, p, c: (i, c), c=c))
    mcol = lambda c: pl.BlockSpec((1, LANES), functools.partial(lambda i, p, c: (0, c + p), c=c))
    mfix = lambda c: pl.BlockSpec((1, LANES), functools.partial(lambda i, p, c: (0, c), c=c))
    prow = pl.BlockSpec((1, LANES), lambda i, p: (0, p))
    p2 = pl.BlockSpec((2, 1, LANES), lambda i, p: (0, 0, p))
    up2 = pl.BlockSpec((2, LANES, LANES), lambda i, p: (0, 0, p))
    out1 = pl.BlockSpec((tm, LANES), lambda i, p: (i, p))
    out2 = pl.BlockSpec((2, tm, LANES), lambda i, p: (0, i, p))
    mspec = pl.BlockSpec((2, nchunk, 1, HEAD, LANES), lambda i, p: (0, i, p, 0, 0))
    s1 = jax.ShapeDtypeStruct((t, BRANCH_W), F32)
    s2 = jax.ShapeDtypeStruct((2, t, BRANCH_W), BF16)
    sm = jax.ShapeDtypeStruct((2, nc, PAIRS, HEAD, LANES), F32)
    lora, gc = A_LORA // LANES, A_GC // LANES
    tile = lambda lead=(): pltpu.VMEM(lead + (tm, LANES), F32)
    return pl.pallas_call(
        functools.partial(_rwkv_maps_body, row_len=row_len, nchunk=nchunk),
        grid=(t // tm, nb),
        in_specs=[col(0), col(nb), col(2 * nb), fix(lora), fix(gc), fix(gc + 1),
                  mcol(0), mcol(nb), mcol(2 * nb), mfix(lora), mfix(gc), mfix(gc + 1),
                  p2, up2, p2, up2, up2, prow, prow, prow,
                  pl.BlockSpec((LANES, LANES), lambda i, p: (0, 0)),
                  pl.BlockSpec((2, CHUNK, CHUNK), lambda i, p: (0, 0, 0)),
                  pl.BlockSpec(msk.shape, lambda i, p: (0, 0, 0, 0))],
        out_specs=[out2, out2, mspec, mspec, out1, out1],
        out_shape=[s2, s2, sm, sm, s1, s1],
        scratch_shapes=[tile(), tile(), tile(), tile((2,)), tile((2,)), tile((2,))],
        compiler_params=_cparams(2, 24 << 20),
        name="rwkv_chunk_maps",
    )(pa, pa, pa, pa, pa, pa, ap["mu"], ap["mu"], ap["mu"], ap["mu"], ap["mu"], ap["mu"],
      ap["w0"], ap["w_up"], ap["a0"], ap["a_up"], ap["g_up"], ap["k_k"], ap["k_a"], ap["r_k"], ap["e2"], tri, msk)


def _pair_rows(x):
    lane = lax.broadcasted_iota(jnp.int32, x.shape, 1)
    return jnp.concatenate([jnp.where(lane < HEAD, x, 0.0), jnp.where(lane >= HEAD, x, 0.0)], axis=0)


MSK_STRICT, MSK_INCL, MSK_EYE, MSK_DIAG8, MSK_OFF8, MSK_OFF16, MSK_OFF32 = range(7)


def _chunk_mask_tables():
    n = 2 * CHUNK
    r = np.arange(n)[:, None]
    c = np.arange(n)[None, :]
    same = (r // CHUNK) == (c // CHUNK)
    out = np.zeros((2, 7, n, n), np.float32)
    for d in range(2):
        before = (c > r) if d else (c < r)
        out[d, MSK_STRICT] = same & before
        out[d, MSK_INCL] = same & (before | (r == c))
        out[d, MSK_EYE] = r == c
        out[d, MSK_DIAG8] = ((r // 8) == (c // 8)) & before
        for idx, s in ((MSK_OFF8, 8), (MSK_OFF16, 16), (MSK_OFF32, 32)):
            blk = (r // (2 * s)) == (c // (2 * s))
            rh, ch = (r // s) % 2, (c // s) % 2
            out[d, idx] = blk & ((rh == 0) & (ch == 1) if d else (rh == 1) & (ch == 0))
    return jnp.asarray(out)


def _scan_pre_body(r_ref, v_ref, kk_ref, lw_ref, kd_ref, b_ref, tri_ref, msk_ref, rhat_ref, y0_ref, m_ref, n_ref, *, nchunk):
    c, n = CHUNK, 2 * CHUNK
    chains = [(d, ci) for d in range(2) for ci in range(nchunk)]
    each = lambda f, *cols: [f(*xs) for xs in zip(*cols)]
    msk = lambda d, k: msk_ref[d, k]
    rows = lambda ci: slice(ci * c, (ci + 1) * c)

    lw = [lw_ref[d, rows(ci), :] for d, ci in chains]
    cum = [_dot_exact_lhs(tri_ref[d], x) for (d, _), x in zip(chains, lw)]
    tot = [x[0:1] if d else x[c - 1:c] for (d, _), x in zip(chains, cum)]
    g_inv = each(lambda x: jnp.exp(-x), cum)
    g_tail = each(lambda t, x: jnp.exp(t - x), tot, cum)
    atp = [_pair_rows(-kk_ref[rows(ci), :] * jnp.exp(x - l)) for (_, ci), x, l in zip(chains, cum, lw)]
    rtp = [_pair_rows(r_ref[rows(ci), :] * jnp.exp(x)) for (_, ci), x in zip(chains, cum)]
    btp = [_pair_rows(b_ref[d, rows(ci), :] * g) for (d, ci), g in zip(chains, g_inv)]
    ktp = [_pair_rows(kd_ref[d, rows(ci), :] * g) for (d, ci), g in zip(chains, g_inv)]
    bhp = [_pair_rows(b_ref[d, rows(ci), :] * g) for (d, ci), g in zip(chains, g_tail)]
    khp = [_pair_rows(kd_ref[d, rows(ci), :] * g) for (d, ci), g in zip(chains, g_tail)]
    vp = [_pair_rows(v_ref[rows(ci), :]) for _, ci in chains]

    big = each(lambda a, r, b, k: _dot_nt(jnp.concatenate([a, r], axis=0), jnp.concatenate([b, k], axis=0)),
               atp, rtp, btp, ktp)
    a_ab = [jnp.where(msk(d, MSK_STRICT) > 0.0, x[:n, :n], 0.0) for (d, _), x in zip(chains, big)]
    a_ak = [jnp.where(msk(d, MSK_STRICT) > 0.0, x[:n, n:], 0.0) for (d, _), x in zip(chains, big)]
    a_rb = [jnp.where(msk(d, MSK_INCL) > 0.0, x[n:, :n], 0.0) for (d, _), x in zip(chains, big)]
    a_rk = [jnp.where(msk(d, MSK_INCL) > 0.0, x[n:, n:], 0.0) for (d, _), x in zip(chains, big)]

    n8 = [x * msk(d, MSK_DIAG8) for (d, _), x in zip(chains, a_ab)]
    t = [msk(d, MSK_EYE) + x for (d, _), x in zip(chains, n8)]
    n2 = each(_dot, n8, n8)
    t = each(lambda x, p: x + _dot(p, x), t, n2)
    n4 = each(_dot, n2, n2)
    t = each(lambda x, p: x + _dot(p, x), t, n4)
    for idx in (MSK_OFF8, MSK_OFF16, MSK_OFF32):
        off = [x * msk(d, idx) for (d, _), x in zip(chains, a_ab)]
        tn = each(_dot, t, off)
        t = each(lambda x, p: x + _dot(p, x), t, tn)

    akv = each(_dot, a_ak, vp)
    w_u = each(lambda ti, a, x: _dot(ti, jnp.concatenate([a, x], axis=1)), t, atp, akv)
    ry = each(_dot, a_rb, w_u)
    rkv = each(_dot, a_rk, vp)
    mn = each(_dot_tn, bhp, w_u)
    kv = each(_dot_tn, khp, vp)
    for i, (d, ci) in enumerate(chains):
        rhat_p = rtp[i] + ry[i][:, :n]
        y0_p = ry[i][:, n:] + rkv[i]
        rhat_ref[d, rows(ci), :] = (rhat_p[:c] + rhat_p[c:]).astype(rhat_ref.dtype)
        y0_ref[d, rows(ci), :] = (y0_p[:c] + y0_p[c:]).astype(y0_ref.dtype)
        m_i = mn[i][:, :n] + msk(d, MSK_EYE) * jnp.exp(tot[i])
        n_i = mn[i][:, n:] + kv[i]
        m_ref[d, ci, 0] = m_i[:c] + m_i[c:]
        n_ref[d, ci, 0] = n_i[:c] + n_i[c:]


def _scan_seq_body(s0_ref, mf_ref, mb_ref, nf_ref, nb_ref, rf_ref, rb_ref, yf0_ref, yb0_ref,
                   yf_ref, yb_ref, sfin_ref, s_ref):
    c = pl.program_id(0)

    @pl.when(c == 0)
    def _():
        s_ref[...] = s0_ref[...]

    for sub in range(CHAIN_STEP):
        for d, (m_ref, n_ref, rh_ref, y0_ref, y_ref) in enumerate(((mf_ref, nf_ref, rf_ref, yf0_ref, yf_ref),
                                                                     (mb_ref, nb_ref, rb_ref, yb0_ref, yb_ref))):
            lc = sub if d == 0 else CHAIN_STEP - 1 - sub
            rows = slice(lc * CHUNK, (lc + 1) * CHUNK)
            for p in range(PAIRS):
                ln = slice(p * LANES, (p + 1) * LANES)
                s = s_ref[d, p]
                y_ref[rows, ln] = _dot(rh_ref[0, rows, ln], s) + y0_ref[0, rows, ln].astype(F32)
                s_ref[d, p] = _dot(_pair_rows(m_ref[0, lc, p]), s) + _pair_rows(n_ref[0, lc, p])

    @pl.when(c == pl.num_programs(0) - 1)
    def _():
        sfin_ref[...] = s_ref[...]


def scan_seq(s0, rhat, y0, m, n):
    t = rhat.shape[1]
    nc = t // (CHUNK * CHAIN_STEP)
    fwd5 = lambda c: (0, c, 0, 0, 0)
    bwd5 = lambda c: (1, nc - 1 - c, 0, 0, 0)
    mblk = (1, CHAIN_STEP, PAIRS, HEAD, LANES)
    rblk = (1, CHAIN_STEP * CHUNK, BRANCH_W)
    sblk = pl.BlockSpec((2, PAIRS, LANES, LANES), lambda c: (0, 0, 0, 0))
    ys = jax.ShapeDtypeStruct((t, BRANCH_W), F32)
    return pl.pallas_call(
        _scan_seq_body,
        grid=(nc,),
        in_specs=[sblk,
                  pl.BlockSpec(mblk, fwd5), pl.BlockSpec(mblk, bwd5),
                  pl.BlockSpec(mblk, fwd5), pl.BlockSpec(mblk, bwd5),
                  pl.BlockSpec(rblk, lambda c: (0, c, 0)), pl.BlockSpec(rblk, lambda c: (1, nc - 1 - c, 0)),
                  pl.BlockSpec(rblk, lambda c: (0, c, 0)), pl.BlockSpec(rblk, lambda c: (1, nc - 1 - c, 0))],
        out_specs=[pl.BlockSpec((CHAIN_STEP * CHUNK, BRANCH_W), lambda c: (c, 0)),
                   pl.BlockSpec((CHAIN_STEP * CHUNK, BRANCH_W), lambda c: (nc - 1 - c, 0)),
                   sblk],
        out_shape=[ys, ys, jax.ShapeDtypeStruct((2, PAIRS, LANES, LANES), F32)],
        scratch_shapes=[pltpu.VMEM((2, PAIRS, LANES, LANES), F32)],
        compiler_params=_cparams(1, 16 << 20),
        name="rwkv_chain",
    )(s0, m, m, n, n, rhat, rhat, y0, y0)


def _areadout_body(yf_ref, yb_ref, bonus_ref, g_ref, lng_ref, lnb_ref, e_ref, o_ref):
    e = e_ref[...]
    for p in range(BRANCH_W // LANES):
        ln = slice(p * LANES, (p + 1) * LANES)
        y = yf_ref[:, ln] + yb_ref[:, ln]
        mu = _dot_split_rhs(y, e) * (1.0 / HEAD)
        yc = y - mu
        var = _dot_split_rhs(yc * yc, e) * (1.0 / HEAD)
        yn = (yc * lax.rsqrt(var + A_GN_EPS)) * lng_ref[:, ln] + lnb_ref[:, ln]
        o_ref[:, ln] = ((yn + bonus_ref[:, ln]) * g_ref[:, ln]).astype(o_ref.dtype)


def a_readout(yf, yb, bonus, g, ap):
    t = yf.shape[0]
    tm = 256
    blk = pl.BlockSpec((tm, BRANCH_W), lambda i: (i, 0))
    prow = pl.BlockSpec((1, BRANCH_W), lambda i: (0, 0))
    return pl.pallas_call(
        _areadout_body,
        grid=(t // tm,),
        in_specs=[blk, blk, blk, blk, prow, prow, pl.BlockSpec((LANES, LANES), lambda i: (0, 0))],
        out_specs=blk,
        out_shape=jax.ShapeDtypeStruct((t, BRANCH_W), BF16),
        compiler_params=_cparams(1, 2 * 5 * tm * BRANCH_W * 4),
        name="rwkv_readout",
    )(yf, yb, bonus, g, ap["ln_g"], ap["ln_b"], ap["e2"])


def _bmix_body(u_ref, v_ref, lng_ref, lnb_ref, ws_ref, bs_ref, o_ref):
    u = _gelu_tanh(u_ref[...].astype(F32))
    v = _layernorm(_gelu_tanh(v_ref[...].astype(F32)), lng_ref[...], lnb_ref[...]).astype(BF16)
    for ci in range(u.shape[0] // B_CHUNK):
        rows = slice(ci * B_CHUNK, (ci + 1) * B_CHUNK)
        for g in range(B_GROUPS):
            ln = slice(g * LANES, (g + 1) * LANES)
            s = jnp.dot(ws_ref[g], v[rows, ln], preferred_element_type=F32) + bs_ref[:, ln]
            o_ref[rows, ln] = (u[rows, ln] * s).astype(o_ref.dtype)


def b_mix(pb, bp):
    t = pb.shape[0]
    tm = 256
    row = pl.BlockSpec((1, BRANCH_W), lambda i: (0, 0))
    return pl.pallas_call(
        _bmix_body,
        grid=(t // tm,),
        in_specs=[pl.BlockSpec((tm, BRANCH_W), lambda i: (i, 0)), pl.BlockSpec((tm, BRANCH_W), lambda i: (i, 1)),
                  row, row, pl.BlockSpec((B_GROUPS, B_CHUNK, B_CHUNK), lambda i: (0, 0, 0)),
                  pl.BlockSpec((B_CHUNK, BRANCH_W), lambda i: (0, 0))],
        out_specs=pl.BlockSpec((tm, BRANCH_W), lambda i: (i, 0)),
        out_shape=jax.ShapeDtypeStruct((t, BRANCH_W), BF16),
        compiler_params=_cparams(1, 12 * tm * BRANCH_W * 4),
        name="gmlp_mix",
    )(pb, pb, bp["ln_g"], bp["ln_b"], bp["ws"], bp["bs"])


def _conv_body(ac_ref, gc_ref, ap_ref, gp_ref, an_ref, gn_ref, dw_ref, dwb_ref, lng_ref, lnb_ref,
               o_ref, zs_ref, cs_ref, *, tm):
    i = pl.program_id(0)
    last = pl.num_programs(0) - 1
    glu = lambda a_ref, g_ref: a_ref[...].astype(F32) * _sigmoid(g_ref[...].astype(F32))
    zs_ref[0:CONV_HALO, :] = jnp.where(i == 0, 0.0, glu(ap_ref, gp_ref))
    zs_ref[CONV_HALO:CONV_HALO + tm, :] = glu(ac_ref, gc_ref)
    zs_ref[CONV_HALO + tm:, :] = jnp.where(i == last, 0.0, glu(an_ref, gn_ref))
    rb, lb = 32, 256
    win = rb + 2 * CONV_HALO
    first = CONV_HALO - C_KERNEL // 2
    for r0 in range(0, tm, rb):
        for l0 in range(0, BRANCH_W, lb):
            w = zs_ref[r0:r0 + win, l0:l0 + lb]
            acc = jnp.zeros((rb, lb), F32) + dwb_ref[:, l0:l0 + lb]
            for s in range(SUBLANES):
                ws = w if s == 0 else pltpu.roll(w, win - s, 0)
                for j in range(C_KERNEL):
                    if (first + j) % SUBLANES == s:
                        a = (first + j) - s
                        acc = acc + ws[a:a + rb] * dw_ref[j:j + 1, l0:l0 + lb]
            cs_ref[r0:r0 + rb, l0:l0 + lb] = acc
    o_ref[...] = _silu(_layernorm(cs_ref[...], lng_ref[...], lnb_ref[...])).astype(o_ref.dtype)


def conv_mix(pc, cp):
    t = pc.shape[0]
    tm = 256
    hb = tm // CONV_HALO
    nh = t // CONV_HALO
    cur = lambda c: pl.BlockSpec((tm, BRANCH_W), functools.partial(lambda i, c: (i, c), c=c))
    prev = lambda c: pl.BlockSpec((CONV_HALO, BRANCH_W), functools.partial(lambda i, c: (jnp.maximum(i * hb - 1, 0), c), c=c))
    nxt = lambda c: pl.BlockSpec((CONV_HALO, BRANCH_W), functools.partial(lambda i, c: (jnp.minimum((i + 1) * hb, nh - 1), c), c=c))
    row = pl.BlockSpec((1, BRANCH_W), lambda i: (0, 0))
    return pl.pallas_call(
        functools.partial(_conv_body, tm=tm),
        grid=(t // tm,),
        in_specs=[cur(0), cur(1), prev(0), prev(1), nxt(0), nxt(1),
                  pl.BlockSpec((C_KERNEL + 1, BRANCH_W), lambda i: (0, 0)), row, row, row],
        out_specs=pl.BlockSpec((tm, BRANCH_W), lambda i: (i, 0)),
        out_shape=jax.ShapeDtypeStruct((t, BRANCH_W), BF16),
        scratch_shapes=[pltpu.VMEM((tm + 2 * CONV_HALO, BRANCH_W), F32), pltpu.VMEM((tm, BRANCH_W), F32)],
        compiler_params=_cparams(1, 16 * tm * BRANCH_W * 4),
        name="conv_mix",
    )(pc, pc, pc, pc, pc, pc, cp["dw"], cp["dw_b"], cp["ln_g"], cp["ln_b"])


FFT_N1, FFT_N2 = 64, 128


def _dft_tables(t):
    two_pi = 2.0 * np.pi
    cidx = np.arange(D_GROUP_CH)
    ph = two_pi * np.outer(cidx, cidx) / D_GROUP_CH
    chan = np.concatenate([np.cos(ph), np.sin(ph)], axis=0)
    if t <= 256:
        n = np.arange(t)
        th = two_pi * np.outer(n, n) / t
        m2 = np.concatenate([np.cos(th), -np.sin(th)], axis=0)
        return None, m2.astype(np.float32), chan.astype(np.float32)
    n1, n2 = FFT_N1, FFT_N2
    assert t == n1 * n2
    k1 = np.arange(n1)
    tok = (n2 * np.arange(n1))[None, None, :] + np.arange(n2)[:, None, None]
    th = two_pi * (k1[None, :, None] * tok) / t
    g1 = np.concatenate([np.cos(th), -np.sin(th)], axis=1)
    q = np.arange(n2)
    th2 = two_pi * np.outer(q, q) / n2
    c2, s2 = np.cos(th2), np.sin(th2)
    m2 = np.block([[c2, s2], [-s2, c2]])
    return g1.astype(np.float32), m2.astype(np.float32), chan.astype(np.float32)


FFT_STEP = 4


def _fft1_body(x_ref, g_ref, o_ref):
    for q in range(FFT_STEP):
        ln = slice(q * BRANCH_W, (q + 1) * BRANCH_W)
        o_ref[:, ln] = _dot(g_ref[q], x_ref[:, ln])


def _fft2_body(z_ref, m2_ref, ch_ref, o_ref, *, n_out, scale, stacked):
    z = jnp.concatenate([z_ref[0], z_ref[1]], axis=0) if stacked else z_ref[...]
    x = _dot(m2_ref[...], z)
    xr, xi = x[:n_out], x[n_out:]
    for g in range(D_GROUPS):
        ln = slice(g * D_GROUP_CH, (g + 1) * D_GROUP_CH)
        f = _dot(xr[:, ln], ch_ref[0:D_GROUP_CH]) + _dot(xi[:, ln], ch_ref[D_GROUP_CH:])
        o_ref[:, ln] = (f * scale).astype(o_ref.dtype)


def fourier_mix(pd):
    t = pd.shape[0]
    g1, m2, chan = (None if a is None else jnp.asarray(a) for a in _dft_tables(t))
    scale = 1.0 / math.sqrt(t * D_GROUP_CH)
    chspec = pl.BlockSpec((2 * D_GROUP_CH, D_GROUP_CH), lambda i: (0, 0))
    if g1 is None:
        return pl.pallas_call(
            functools.partial(_fft2_body, n_out=t, scale=scale, stacked=False),
            grid=(1,),
            in_specs=[pl.BlockSpec((t, BRANCH_W), lambda i: (0, 0)), pl.BlockSpec((2 * t, t), lambda i: (0, 0)), chspec],
            out_specs=pl.BlockSpec((t, BRANCH_W), lambda i: (0, 0)),
            out_shape=jax.ShapeDtypeStruct((t, BRANCH_W), BF16),
            compiler_params=_cparams(1, 16 << 20),
            name="fourier_small",
        )(pd, m2, chan)
    n1, n2 = FFT_N1, FFT_N2
    z = pl.pallas_call(
        _fft1_body,
        grid=(n2 // FFT_STEP,),
        in_specs=[pl.BlockSpec((n1, FFT_STEP * BRANCH_W), lambda q: (0, q)),
                  pl.BlockSpec((FFT_STEP, 2 * n1, n1), lambda q: (q, 0, 0))],
        out_specs=pl.BlockSpec((2 * n1, FFT_STEP * BRANCH_W), lambda q: (0, q)),
        out_shape=jax.ShapeDtypeStruct((2 * n1, n2 * BRANCH_W), F32),
        compiler_params=_cparams(1, 8 << 20),
        name="fourier_stage1",
    )(pd.reshape(n1, n2 * BRANCH_W), g1)
    f = pl.pallas_call(
        functools.partial(_fft2_body, n_out=n2, scale=scale, stacked=True),
        grid=(n1,),
        in_specs=[pl.BlockSpec((2, None, n2, BRANCH_W), lambda k: (0, k, 0, 0)),
                  pl.BlockSpec((2 * n2, 2 * n2), lambda k: (0, 0)), chspec],
        out_specs=pl.BlockSpec((n2, BRANCH_W), lambda k: (0, k)),
        out_shape=jax.ShapeDtypeStruct((n2, n1 * BRANCH_W), BF16),
        compiler_params=_cparams(1, 16 << 20),
        name="fourier_stage2",
    )(z.reshape(2, n1, n2, BRANCH_W), m2, chan)
    return f.reshape(t, BRANCH_W)


def _align_a(x):
    return jnp.pad(x, [(0, 0)] * (x.ndim - 1) + [(0, A_PAD - x.shape[-1])])


def _pad_rows(x, before, rows):
    return jnp.pad(x, [(0, 0)] * (x.ndim - 2) + [(before, rows - before - x.shape[-2]), (0, 0)])


def _layer_params(l, w_in, a_mu, a_w0, a_w_up, a_a0, a_a_up, a_g_up, a_k_k, a_k_a, a_r_k, a_ln, a_w_out,
                  b_ln, b_ws, b_bs, b_w_out, c_dw, c_dw_b, c_ln, c_w_out, d_w_out, gate_w, gate_b, w_o,
                  ffn_w1, ffn_w3, ffn_w2):
    row = lambda v: v.reshape(1, -1)
    hid = np.arange(LANES) // HEAD
    e2 = jnp.asarray((hid[:, None] == hid[None, :]).astype(np.float32)).astype(BF16)
    g_up = jnp.stack([a_g_up[l][:LANES], _pad_rows(a_g_up[l][LANES:], 0, LANES)])
    ap = dict(mu=_align_a(row(a_mu[l])), w0=a_w0[l][:, None, :], w_up=_pad_rows(a_w_up[l], 0, LANES),
              a0=a_a0[l][:, None, :], a_up=_pad_rows(a_a_up[l], LORA_W, LANES), g_up=g_up,
              k_k=row(a_k_k[l]), k_a=row(a_k_a[l]), r_k=row(a_r_k[l]), ln_g=row(a_ln[l][0]), ln_b=row(a_ln[l][1]), e2=e2)
    bs_exp = jnp.repeat(jnp.swapaxes(b_bs[l], 0, 1), LANES, axis=1)
    return dict(
        layer=l, w_a=w_in, w_bcd=w_in[l][:, A_PROJ:], ap=ap,
        bp=dict(ln_g=row(b_ln[l][0]), ln_b=row(b_ln[l][1]), ws=b_ws[l].astype(BF16), bs=bs_exp),
        cp=dict(dw=_pad_rows(c_dw[l], 0, C_KERNEL + 1), dw_b=row(c_dw_b[l]), ln_g=row(c_ln[l][0]), ln_b=row(c_ln[l][1])),
        w_outs=[a_w_out, b_w_out, c_w_out, d_w_out],
        gate_w=gate_w, gate_b=row(gate_b[l]), w_o=w_o,
        w1=ffn_w1, w3=ffn_w3, w2=ffn_w2)


def _tri_tables():
    i = np.arange(CHUNK)
    lower = (i[None, :] <= i[:, None]).astype(np.float32)
    return jnp.asarray(np.stack([lower, lower.T])).astype(BF16)


def _rwkv_scan(pa, s0, ap, row_len):
    rhat, y0, m, n, bonus, g = rwkv_maps(pa, ap, row_len, _tri_tables(), _chunk_mask_tables())
    yf, yb, s_fin = scan_seq(s0, rhat, y0, m, n)
    return yf, yb, bonus, g, s_fin


def _in_proj_a(h, lp):
    return matmul(h, lp["w_a"], A_PAD, 0, 512, F32, "in_proj_a", layer=lp["layer"])


def _in_proj_bcd(h, lp, cast_merge_weights=False):
    tn, l, w = 512, lp["layer"], lp["w_bcd"]
    if not cast_merge_weights:
        return (matmul(h, w, 2 * BRANCH_W, 0, tn, BF16, "in_proj_b"),
                matmul(h, w, 2 * BRANCH_W, 2 * BRANCH_W, tn, BF16, "in_proj_c"),
                matmul(h, w, BRANCH_W, 4 * BRANCH_W, tn, BF16, "in_proj_d")), None
    half = D_MODEL // 2
    pb, (g_top,) = matmul(h, w, 2 * BRANCH_W, 0, tn, BF16, "in_proj_b", sides=[(lp["gate_w"], l, 0, half)])
    pc, (g_bot,) = matmul(h, w, 2 * BRANCH_W, 2 * BRANCH_W, tn, BF16, "in_proj_c", sides=[(lp["gate_w"], l, half, half)])
    pd, w_outs = matmul(h, w, BRANCH_W, 4 * BRANCH_W, tn, BF16, "in_proj_d",
                        sides=[(wo, l, 0, BRANCH_W) for wo in lp["w_outs"]])
    return (pb, pc, pd), ((g_top, g_bot), list(w_outs))


def _token_mix(h, s0, row_len, lp, pbcd, merge_w, cast_ffn_weights=False):
    pa = _in_proj_a(h, lp)
    yf, yb, bonus, g, s_fin = _rwkv_scan(pa, s0, lp["ap"], row_len)
    za = a_readout(yf, yb, bonus, g, lp["ap"])
    zb = b_mix(pbcd[0], lp["bp"])
    zc = conv_mix(pbcd[1], lp["cp"])
    zd = fourier_mix(pbcd[2])
    zs, w13 = [za, zb, zc, zd], None
    if cast_ffn_weights:
        sides = [(w, lp["layer"], 0, w.shape[1]) for w in (lp["w1"], lp["w3"])]
        merged, w13 = merge(h, zs, merge_w[0], lp["gate_b"], merge_w[1], sides=sides)
    else:
        merged = merge(h, zs, merge_w[0], lp["gate_b"], merge_w[1])
    mix = matmul(merged, lp["w_o"], D_MODEL, 0, 512, BF16, "out_proj", layer=lp["layer"])
    return mix, s_fin, w13


def _ffn(h2, lp, w13, w2_bf16=None):
    if w2_bf16 is None:
        u, (w2_bf16,) = ffn_up(h2, w13[0], w13[1], sides=[(lp["w2"], lp["layer"], 0, lp["w2"].shape[1])])
    else:
        u = ffn_up(h2, w13[0], w13[1])
    return matmul(u, w2_bf16, D_MODEL, 0, 512, BF16, "ffn_down"), w2_bf16


def kernel(x, c, ctx, c_ctx, mod_w, mod_b, norm_g, w_in, a_mu, a_w0, a_w_up, a_a0, a_a_up, a_g_up, a_k_k, a_k_a,
           a_r_k, a_ln, a_w_out, b_ln, b_ws, b_bs, b_w_out, c_dw, c_dw_b, c_ln, c_w_out, d_w_out, gate_w, gate_b,
           w_o, ffn_w1, ffn_w3, ffn_w2):
    depth = mod_w.shape[0]
    d = D_MODEL
    c_cols = jnp.concatenate([c.reshape(d, 1), c_ctx.reshape(d, 1), jnp.zeros((d, 6), F32)], axis=1)
    mods = modulation(c_cols, mod_w, mod_b.reshape(depth, 1, 6 * d))
    x_lat, x_ctx = x[0], ctx[0]
    zero_state = jnp.zeros((2, PAIRS, LANES, LANES), F32)
    weights = (w_in.astype(BF16), a_mu, a_w0, a_w_up, a_a0, a_a_up, a_g_up, a_k_k, a_k_a, a_r_k, a_ln, a_w_out, b_ln, b_ws, b_bs,
               b_w_out, c_dw, c_dw_b, c_ln, c_w_out, d_w_out, gate_w, gate_b, w_o, ffn_w1, ffn_w3, ffn_w2)
    h_lat = h_ctx = None
    for l in range(depth):
        last = l == depth - 1
        lp = _layer_params(l, *weights)
        ng = [norm_g[l, i].reshape(1, d) for i in range(4)]
        ml = [mods[l, 0:1, i * d:(i + 1) * d] for i in range(6)]
        mc = [mods[l, 1:2, i * d:(i + 1) * d] for i in range(6)]
        if l == 0:
            h_lat = normmod(x_lat, ng[0], ml[0], ml[1])
            h_ctx = normmod(x_ctx, ng[0], mc[0], mc[1])

        pbcd_lat, merge_w = _in_proj_bcd(h_lat, lp, cast_merge_weights=True)

        if last:
            ctx_states = _rwkv_scan(_in_proj_a(h_ctx, lp), zero_state, lp["ap"], x_ctx.shape[0])[4]
        else:
            pbcd_ctx, _ = _in_proj_bcd(h_ctx, lp)
            mix_ctx, ctx_states, _ = _token_mix(h_ctx, zero_state, x_ctx.shape[0], lp, pbcd_ctx, merge_w)

        mix_lat, _, w13 = _token_mix(h_lat, ctx_states, GRID_W, lp, pbcd_lat, merge_w, cast_ffn_weights=True)
        x_lat, h2 = resnorm(x_lat, mix_lat, ml[2], ng[1], (ng[2], ml[3], ml[4]))
        y, w2_bf16 = _ffn(h2, lp, w13)
        if last:
            x_lat = resnorm(x_lat, y, ml[5], ng[3])
        else:
            ngn = norm_g[l + 1, 0].reshape(1, d)
            mln = [mods[l + 1, 0:1, i * d:(i + 1) * d] for i in range(2)]
            mcn = [mods[l + 1, 1:2, i * d:(i + 1) * d] for i in range(2)]
            x_lat, h_lat = resnorm(x_lat, y, ml[5], ng[3], (ngn, mln[0], mln[1]))
            x_ctx, h2c = resnorm(x_ctx, mix_ctx, mc[2], ng[1], (ng[2], mc[3], mc[4]))
            yc, _ = _ffn(h2c, lp, w13, w2_bf16)
            x_ctx, h_ctx = resnorm(x_ctx, yc, mc[5], ng[3], (ngn, mcn[0], mcn[1]))
    return x_lat[None]
```

```python
import functools
import math

import numpy as np
import jax
import jax.numpy as jnp
from jax import lax
from jax.experimental import pallas as pl
from jax.experimental.pallas import tpu as pltpu

F32, BF16 = jnp.float32, jnp.bfloat16

D_MODEL = 4096
DEPTH = 2
GRID_W = 64
BRANCH_W = 1024
HEAD = 64
HEADS = BRANCH_W // HEAD
PAIRS = HEADS // 2
LORA_W, LORA_A, LORA_G = 64, 64, 160
A_PROJ = 3 * BRANCH_W + LORA_W + LORA_A + LORA_G
A_GN_EPS = 64e-5
B_CHUNK = 128
B_GROUPS = 8
C_KERNEL = 31
D_GROUPS = 4
D_GROUP_CH = BRANCH_W // D_GROUPS
FFN_HIDDEN = 11008
RMS_EPS = 1e-6
LN_EPS = 1e-5

LANES = 128
SUBLANES = 8
VMEM_BUDGET = 60 * 1024 * 1024

A_LORA = 3 * BRANCH_W
A_GC = A_LORA + LANES
A_PAD = A_GC + 3 * LANES

MERGE_TM = 512
FFN_UP_TM = 2048
CHUNK = 64
CHAIN_STEP = 2
CONV_HALO = 16


def _cparams(n_axes, vmem_bytes):
    limit = int(min(max(vmem_bytes + (8 << 20), 32 << 20), VMEM_BUDGET))
    return pltpu.CompilerParams(dimension_semantics=("arbitrary",) * n_axes, vmem_limit_bytes=limit)


def _dot(a, b):
    return jnp.dot(a.astype(BF16), b.astype(BF16), preferred_element_type=F32)


def _dot_nt(a, b):
    return lax.dot_general(a.astype(BF16), b.astype(BF16), (((1,), (1,)), ((), ())), preferred_element_type=F32)


def _dot_tn(a, b):
    return lax.dot_general(a.astype(BF16), b.astype(BF16), (((0,), (0,)), ((), ())), preferred_element_type=F32)


def _split3(x):
    hi = x.astype(BF16)
    r1 = x - hi.astype(F32)
    mid = r1.astype(BF16)
    lo = (r1 - mid.astype(F32)).astype(BF16)
    return hi, mid, lo


def _dot_split_rhs(x, e):
    hi = x.astype(BF16)
    lo = (x - hi.astype(F32)).astype(BF16)
    f = lambda p: jnp.dot(p, e, preferred_element_type=F32)
    return f(hi) + f(lo)


def _dot_exact_lhs(e, x):
    hi, mid, lo = _split3(x)
    f = lambda p: jnp.dot(e, p, preferred_element_type=F32)
    return f(hi) + f(mid) + f(lo)


def _sigmoid(x):
    return 1.0 / (1.0 + jnp.exp(-x))


def _silu(x):
    return x * _sigmoid(x)


def _softplus(x):
    return jnp.maximum(x, 0.0) + jnp.log(1.0 + jnp.exp(-jnp.abs(x)))


def _gelu_tanh(x):
    return 0.5 * x * (1.0 + jnp.tanh(math.sqrt(2.0 / math.pi) * (x + 0.044715 * (x * x * x))))


def _rmsnorm(x, g):
    return (x * lax.rsqrt(jnp.mean(x * x, axis=-1, keepdims=True) + RMS_EPS)) * g


def _layernorm(x, g, b):
    mu = jnp.mean(x, axis=-1, keepdims=True)
    xc = x - mu
    var = jnp.mean(xc * xc, axis=-1, keepdims=True)
    return (xc * lax.rsqrt(var + LN_EPS)) * g + b


MOD_ROWS = 32
MOD_COLS = 512


def _mod_body(c_ref, w_ref, b_ref, o_ref, s0_ref, s1_ref):
    d, tn = w_ref.shape[1], w_ref.shape[2]

    @pl.when((pl.program_id(0) == 0) & (pl.program_id(1) == 0))
    def _():
        s = _silu(c_ref[...])
        s0_ref[...] = jnp.broadcast_to(s[:, 0:1], (d, LANES))
        s1_ref[...] = jnp.broadcast_to(s[:, 1:2], (d, LANES))

    for c0 in range(0, tn, MOD_COLS):
        def step(k, acc):
            r0 = pl.multiple_of(k * MOD_ROWS, MOD_ROWS)
            w = w_ref[0, pl.ds(r0, MOD_ROWS), c0:c0 + MOD_COLS]
            s0 = jnp.concatenate([s0_ref[pl.ds(r0, MOD_ROWS), :]] * (MOD_COLS // LANES), axis=1)
            s1 = jnp.concatenate([s1_ref[pl.ds(r0, MOD_ROWS), :]] * (MOD_COLS // LANES), axis=1)
            return acc[0] + w * s0, acc[1] + w * s1

        z = jnp.zeros((MOD_ROWS, MOD_COLS), F32)
        a0, a1 = lax.fori_loop(0, d // MOD_ROWS, step, (z, z), unroll=2)
        bias = b_ref[0, :, c0:c0 + MOD_COLS]
        o_ref[0, :, c0:c0 + MOD_COLS] = jnp.concatenate(
            [jnp.sum(a0, axis=0, keepdims=True) + bias, jnp.sum(a1, axis=0, keepdims=True) + bias,
             jnp.zeros((6, MOD_COLS), F32)], axis=0)


def modulation(c_cols, mod_w, mod_b):
    depth, d, n = mod_w.shape
    tn = 1024
    return pl.pallas_call(
        _mod_body,
        grid=(depth, n // tn),
        in_specs=[pl.BlockSpec((d, 8), lambda l, j: (0, 0)),
                  pl.BlockSpec((1, d, tn), lambda l, j: (l, 0, j)),
                  pl.BlockSpec((1, 1, tn), lambda l, j: (l, 0, j))],
        out_specs=pl.BlockSpec((1, 8, tn), lambda l, j: (l, 0, j)),
        out_shape=jax.ShapeDtypeStruct((depth, 8, n), F32),
        scratch_shapes=[pltpu.VMEM((d, LANES), F32), pltpu.VMEM((d, LANES), F32)],
        compiler_params=_cparams(2, 2 * d * tn * 4 + 5 * d * LANES * 4),
        name="modulation",
    )(c_cols, mod_w, mod_b)


def _normmod_body(x_ref, g_ref, sh_ref, sc_ref, h_ref):
    h = _rmsnorm(x_ref[...], g_ref[...]) * (1.0 + sc_ref[...]) + sh_ref[...]
    h_ref[...] = h.astype(h_ref.dtype)


def normmod(x, g, shift, scale):
    t, d = x.shape
    tm = 256
    row = pl.BlockSpec((1, d), lambda i: (0, 0))
    return pl.pallas_call(
        _normmod_body,
        grid=(t // tm,),
        in_specs=[pl.BlockSpec((tm, d), lambda i: (i, 0)), row, row, row],
        out_specs=pl.BlockSpec((tm, d), lambda i: (i, 0)),
        out_shape=jax.ShapeDtypeStruct((t, d), BF16),
        compiler_params=_cparams(1, 2 * tm * d * 6),
        name="normmod",
    )(x, g, shift, scale)


def _resnorm_body(x_ref, y_ref, gate_ref, gpost_ref, gpre_ref, sh_ref, sc_ref, xo_ref, h_ref):
    xn = x_ref[...] + gate_ref[...] * _rmsnorm(y_ref[...].astype(F32), gpost_ref[...])
    xo_ref[...] = xn
    h = _rmsnorm(xn, gpre_ref[...]) * (1.0 + sc_ref[...]) + sh_ref[...]
    h_ref[...] = h.astype(h_ref.dtype)


def _res_body(x_ref, y_ref, gate_ref, gpost_ref, xo_ref):
    xo_ref[...] = x_ref[...] + gate_ref[...] * _rmsnorm(y_ref[...].astype(F32), gpost_ref[...])


def resnorm(x, y, gate, g_post, nxt=None):
    t, d = x.shape
    tm = 256
    row = pl.BlockSpec((1, d), lambda i: (0, 0))
    tile = pl.BlockSpec((tm, d), lambda i: (i, 0))
    if nxt is None:
        return pl.pallas_call(
            _res_body, grid=(t // tm,), in_specs=[tile, tile, row, row], out_specs=tile,
            out_shape=jax.ShapeDtypeStruct((t, d), F32),
            compiler_params=_cparams(1, 2 * tm * d * 12), name="residual",
        )(x, y, gate, g_post)
    return pl.pallas_call(
        _resnorm_body, grid=(t // tm,), in_specs=[tile, tile, row, row, row, row, row],
        out_specs=[tile, tile],
        out_shape=[jax.ShapeDtypeStruct((t, d), F32), jax.ShapeDtypeStruct((t, d), BF16)],
        compiler_params=_cparams(1, 2 * tm * d * 14), name="residual_norm",
    )(x, y, gate, g_post, *nxt)


def _mm_body(x_ref, w_ref, *rest, n_side, cast, w_t):
    side_in, o_ref, side_out = rest[:n_side], rest[n_side], rest[n_side + 1:2 * n_side + 1]
    for s_in, s_out in zip(side_in, side_out):
        s_out[...] = s_in[...].astype(BF16)
    if cast:
        wb_ref = rest[-1]

        @pl.when(pl.program_id(1) == 0)
        def _():
            wb_ref[...] = w_ref[...].astype(BF16)

        w = wb_ref[...]
    else:
        w = w_ref[...]
    if w_t:
        acc = lax.dot_general(x_ref[...], w, (((1,), (1,)), ((), ())), preferred_element_type=F32)
    else:
        acc = jnp.dot(x_ref[...], w, preferred_element_type=F32)
    o_ref[...] = acc.astype(o_ref.dtype)


def _wspec(w, layer, rows, tn, col_block, w_t=False):
    if w_t:
        return pl.BlockSpec((None, tn, rows), lambda j, i: (layer, col_block(j), 0))
    if w.ndim == 3:
        return pl.BlockSpec((None, rows, tn), lambda j, i: (layer, 0, col_block(j)))
    return pl.BlockSpec((rows, tn), lambda j, i: (0, col_block(j)))


def _side_specs(sides, nj, ni):
    side_in, side_out, side_shape, side_args, vm = [], [], [], [], 0
    for arr, lyr, row0, nrows in sides:
        rps, cols = nrows // (nj * ni), arr.shape[2]
        assert rps * nj * ni == nrows and rps % 16 == 0 and row0 % rps == 0
        side_in.append(pl.BlockSpec((None, rps, cols),
                                    functools.partial(lambda j, i, lyr, b0: (lyr, b0 + j * ni + i, 0), lyr=lyr, b0=row0 // rps)))
        side_out.append(pl.BlockSpec((rps, cols), lambda j, i: (j * ni + i, 0)))
        side_shape.append(jax.ShapeDtypeStruct((nrows, cols), BF16))
        side_args.append(arr)
        vm += 2 * rps * cols * 6
    return side_in, side_out, side_shape, side_args, vm


def matmul(x, w, n, col0, tn, out_dtype, name, layer=None, sides=(), w_t=False):
    m, k = x.shape
    tm = min(m, 1024 if k <= 4096 else 512)
    j0 = col0 // tn
    assert col0 % tn == 0 and n % tn == 0 and m % tm == 0
    nj, ni = n // tn, m // tm
    cast = w.dtype == F32
    assert not (cast and w_t)
    vm = 2 * (tm * k * 2 + k * tn * w.dtype.itemsize + tm * tn * 4) + (k * tn * 2 if cast else 0)
    side_in, side_out, side_shape, side_args, side_vm = _side_specs(sides, nj, ni)
    vm += side_vm
    out = pl.pallas_call(
        functools.partial(_mm_body, n_side=len(sides), cast=cast, w_t=w_t),
        grid=(nj, ni),
        in_specs=[pl.BlockSpec((tm, k), lambda j, i: (i, 0)),
                  _wspec(w, layer, k, tn, lambda j: j + j0, w_t)] + side_in,
        out_specs=[pl.BlockSpec((tm, tn), lambda j, i: (i, j))] + side_out,
        out_shape=[jax.ShapeDtypeStruct((m, n), out_dtype)] + side_shape,
        scratch_shapes=[pltpu.VMEM((k, tn), BF16)] if cast else [],
        compiler_params=_cparams(2, vm),
        name=name,
    )(x, w, *side_args)
    return (out[0], out[1:]) if sides else out[0]


def _round_rows_body(w_ref, o_ref):
    o_ref[...] = w_ref[...].astype(BF16)


def round_rows(w, row0, nrows, blk):
    depth, _, cols = w.shape
    assert row0 % blk == 0 and nrows % blk == 0 and blk % 16 == 0
    b0 = row0 // blk
    return pl.pallas_call(
        _round_rows_body,
        grid=(depth, nrows // blk),
        in_specs=[pl.BlockSpec((None, blk, cols), lambda l, i: (l, b0 + i, 0))],
        out_specs=pl.BlockSpec((None, blk, cols), lambda l, i: (l, i, 0)),
        out_shape=jax.ShapeDtypeStruct((depth, nrows, cols), BF16),
        compiler_params=_cparams(2, 2 * blk * cols * 6),
        name="round_rows",
    )(w)


def _ffn_up_body(x_ref, w1_ref, w3_ref, *rest, n_side):
    side_in, o_ref, side_out = rest[:n_side], rest[n_side], rest[n_side + 1:]
    for s_in, s_out in zip(side_in, side_out):
        s_out[...] = s_in[...].astype(BF16)
    x = x_ref[...]
    a = jnp.dot(x, w1_ref[...], preferred_element_type=F32)
    b = jnp.dot(x, w3_ref[...], preferred_element_type=F32)
    o_ref[...] = (_silu(a) * b).astype(o_ref.dtype)


def ffn_up(h, w1, w3, sides=()):
    m, k = h.shape
    n = w1.shape[-1]
    tm, tn = min(m, FFN_UP_TM), 256
    nj, ni = n // tn, m // tm
    vm = 2 * (tm * k * 2 + 2 * k * tn * 2 + tm * tn * 2) + 5 * tm * tn * 4
    side_in, side_out, side_shape, side_args, side_vm = _side_specs(sides, nj, ni)
    out = pl.pallas_call(
        functools.partial(_ffn_up_body, n_side=len(sides)),
        grid=(nj, ni),
        in_specs=[pl.BlockSpec((tm, k), lambda j, i: (i, 0)),
                  pl.BlockSpec((k, tn), lambda j, i: (0, j)),
                  pl.BlockSpec((k, tn), lambda j, i: (0, j))] + side_in,
        out_specs=[pl.BlockSpec((tm, tn), lambda j, i: (i, j))] + side_out,
        out_shape=[jax.ShapeDtypeStruct((m, n), BF16)] + side_shape,
        compiler_params=_cparams(2, vm + side_vm),
        name="ffn_up",
    )(h, w1, w3, *side_args)
    return (out[0], out[1:]) if sides else out[0]


def _merge_body(h_ref, *refs, n_side):
    z, g_top, g_bot, bias, w_out = refs[0:4], refs[4:8], refs[8:12], refs[12:16], refs[16:20]
    side_in, o_ref, side_out = refs[20:20 + n_side], refs[20 + n_side], refs[21 + n_side:]
    for s_in, s_out in zip(side_in, side_out):
        s_out[...] = s_in[...].astype(BF16)
    half = g_top[0].shape[0]
    h_top, h_bot = h_ref[:, :half], h_ref[:, half:]
    acc = None
    for br in range(4):
        logits = (jnp.dot(h_top, g_top[br][...], preferred_element_type=F32)
                  + jnp.dot(h_bot, g_bot[br][...], preferred_element_type=F32) + bias[br][...])
        y = jnp.dot(z[br][...], w_out[br][...], preferred_element_type=F32)
        acc = _sigmoid(logits) * y if acc is None else acc + _sigmoid(logits) * y
    o_ref[...] = acc.astype(o_ref.dtype)


def merge(h, zs, gate_halves, gate_b, w_outs, sides=()):
    m, d = h.shape
    bw = zs[0].shape[1]
    half = gate_halves[0].shape[0]
    tm, tn = min(m, MERGE_TM), 256
    nj = d // tn
    hspec = pl.BlockSpec((tm, d), lambda j, i: (i, 0))
    zspec = pl.BlockSpec((tm, bw), lambda j, i: (i, 0))
    gspecs = [pl.BlockSpec((half, tn), functools.partial(lambda j, i, br: (0, br * nj + j), br=br)) for br in range(4)]
    bspecs = [pl.BlockSpec((1, tn), functools.partial(lambda j, i, br: (0, br * nj + j), br=br)) for br in range(4)]
    wspec = pl.BlockSpec((bw, tn), lambda j, i: (0, j))
    vm = 2 * (tm * d * 2 + 4 * tm * bw * 2 + tm * tn * 2 + 4 * (d + bw) * tn * 2) + 3 * tm * tn * 4
    side_in, side_out, side_shape, side_args, side_vm = _side_specs(sides, nj, m // tm)
    out = pl.pallas_call(
        functools.partial(_merge_body, n_side=len(sides)),
        grid=(nj, m // tm),
        in_specs=[hspec] + [zspec] * 4 + gspecs + gspecs + bspecs + [wspec] * 4 + side_in,
        out_specs=[pl.BlockSpec((tm, tn), lambda j, i: (i, j))] + side_out,
        out_shape=[jax.ShapeDtypeStruct((m, d), BF16)] + side_shape,
        compiler_params=_cparams(2, vm + side_vm),
        name="merge",
    )(h, *zs, *([gate_halves[0]] * 4), *([gate_halves[1]] * 4), gate_b, gate_b, gate_b, gate_b, *w_outs, *side_args)
    return (out[0], out[1:]) if sides else out[0]


def _token_shift(x, mu, row_len):
    tm = x.shape[0]
    pos = lax.broadcasted_iota(jnp.int32, x.shape, 0) & (row_len - 1)
    prev = jnp.where(pos == 0, 0.0, pltpu.roll(x, 1, 0))
    nxt = jnp.where(pos == row_len - 1, 0.0, pltpu.roll(x, tm - 1, 0))
    return x + mu * (0.5 * (prev + nxt) - x)


def _aprep_body(r_ref, k_ref, v_ref, wa_ref, g1_ref, g2_ref,
                mur_ref, muk_ref, muv_ref, muwa_ref, mug1_ref, mug2_ref,
                w0_ref, wup_ref, a0_ref, aup_ref, gup_ref, kk_ref, ka_ref, rk_ref, e_ref,
                ro_ref, vo_ref, kko_ref, lw_ref, kd_ref, b_ref, bonus_ref, g_ref, *, row_len):
    e = e_ref[...]
    xr = _token_shift(r_ref[...], mur_ref[...], row_len)
    xk = _token_shift(k_ref[...], muk_ref[...], row_len)
    xv = _token_shift(v_ref[...], muv_ref[...], row_len)
    xwa = _token_shift(wa_ref[...], muwa_ref[...], row_len)
    xg1 = _token_shift(g1_ref[...], mug1_ref[...], row_len)
    xg2 = _token_shift(g2_ref[...], mug2_ref[...], row_len)
    kk = xk * kk_ref[...]
    kk = kk * lax.rsqrt(jnp.maximum(_dot_split_rhs(kk * kk, e), 1e-12))
    tw = jnp.tanh(xwa)
    bonus = None
    for d in range(2):
        w = -_softplus(-(w0_ref[d] + _dot(tw, wup_ref[d]))) - 0.5
        a = _sigmoid(a0_ref[d] + _dot(xwa, aup_ref[d]))
        kd = xk * (1.0 + (a - 1.0) * ka_ref[...])
        lw_ref[d] = -jnp.exp(w)
        kd_ref[d] = kd
        b_ref[d] = kk * a
        bn = _dot_split_rhs(xr * kd * rk_ref[...], e) * xv
        bonus = bn if bonus is None else bonus + bn
    ro_ref[...] = xr
    vo_ref[...] = xv
    kko_ref[...] = kk
    bonus_ref[...] = bonus
    g_ref[...] = _dot(_sigmoid(xg1), gup_ref[0]) + _dot(_sigmoid(xg2), gup_ref[1])


N_PREP_IN = 21


def _rwkv_maps_body(*refs, row_len, nchunk):
    prep_in, (tri_ref, msk_ref) = refs[:N_PREP_IN], refs[N_PREP_IN:N_PREP_IN + 2]
    rhat_ref, y0_ref, m_ref, n_ref, bonus_ref, g_ref = refs[N_PREP_IN + 2:N_PREP_IN + 8]
    r_s, v_s, kk_s, lw_s, kd_s, b_s = refs[N_PREP_IN + 8:]
    _aprep_body(*prep_in, r_s, v_s, kk_s, lw_s, kd_s, b_s, bonus_ref, g_ref, row_len=row_len)
    _scan_pre_body(r_s, v_s, kk_s, lw_s, kd_s, b_s, tri_ref, msk_ref, rhat_ref, y0_ref, m_ref, n_ref, nchunk=nchunk)


def rwkv_maps(pa, ap, row_len, tri, msk):
    t = pa.shape[0]
    nchunk = 4
    tm = nchunk * CHUNK
    nc = t // CHUNK
    nb = BRANCH_W // LANES
    col = lambda c: pl.BlockSpec((tm, LANES), functools.partial(lambda i, p, c: (i, c + p), c=c))
    fix = lambda c: pl.BlockSpec((tm, LANES), functools.partial(lambda i, p, c: (i, c), c=c))
    mcol = lambda c: pl.BlockSpec((1, LANES), functools.partial(lambda i, p, c: (0, c + p), c=c))
    mfix = lambda c: pl.BlockSpec((1, LANES), functools.partial(lambda i, p, c: (0, c), c=c))
    prow = pl.BlockSpec((1, LANES), lambda i, p: (0, p))
    p2 = pl.BlockSpec((2, 1, LANES), lambda i, p: (0, 0, p))
    up2 = pl.BlockSpec((2, LANES, LANES), lambda i, p: (0, 0, p))
    out1 = pl.BlockSpec((tm, LANES), lambda i, p: (i, p))
    out2 = pl.BlockSpec((2, tm, LANES), lambda i, p: (0, i, p))
    mspec = pl.BlockSpec((2, nchunk, 1, HEAD, LANES), lambda i, p: (0, i, p, 0, 0))
    s1 = jax.ShapeDtypeStruct((t, BRANCH_W), F32)
    s2 = jax.ShapeDtypeStruct((2, t, BRANCH_W), BF16)
    sm = jax.ShapeDtypeStruct((2, nc, PAIRS, HEAD, LANES), F32)
    lora, gc = A_LORA // LANES, A_GC // LANES
    tile = lambda lead=(): pltpu.VMEM(lead + (tm, LANES), F32)
    return pl.pallas_call(
        functools.partial(_rwkv_maps_body, row_len=row_len, nchunk=nchunk),
        grid=(t // tm, nb),
        in_specs=[col(0), col(nb), col(2 * nb), fix(lora), fix(gc), fix(gc + 1),
                  mcol(0), mcol(nb), mcol(2 * nb), mfix(lora), mfix(gc), mfix(gc + 1),
                  p2, up2, p2, up2, up2, prow, prow, prow,
                  pl.BlockSpec((LANES, LANES), lambda i, p: (0, 0)),
                  pl.BlockSpec((2, CHUNK, CHUNK), lambda i, p: (0, 0, 0)),
                  pl.BlockSpec(msk.shape, lambda i, p: (0, 0, 0, 0))],
        out_specs=[out2, out2, mspec, mspec, out1, out1],
        out_shape=[s2, s2, sm, sm, s1, s1],
        scratch_shapes=[tile(), tile(), tile(), tile((2,)), tile((2,)), tile((2,))],
        compiler_params=_cparams(2, 24 << 20),
        name="rwkv_chunk_maps",
    )(pa, pa, pa, pa, pa, pa, ap["mu"], ap["mu"], ap["mu"], ap["mu"], ap["mu"], ap["mu"],
      ap["w0"], ap["w_up"], ap["a0"], ap["a_up"], ap["g_up"], ap["k_k"], ap["k_a"], ap["r_k"], ap["e2"], tri, msk)


def _pair_rows(x):
    lane = lax.broadcasted_iota(jnp.int32, x.shape, 1)
    return jnp.concatenate([jnp.where(lane < HEAD, x, 0.0), jnp.where(lane >= HEAD, x, 0.0)], axis=0)


MSK_STRICT, MSK_INCL, MSK_EYE, MSK_DIAG8, MSK_OFF8, MSK_OFF16, MSK_OFF32 = range(7)


def _chunk_mask_tables():
    n = 2 * CHUNK
    r = np.arange(n)[:, None]
    c = np.arange(n)[None, :]
    same = (r // CHUNK) == (c // CHUNK)
    out = np.zeros((2, 7, n, n), np.float32)
    for d in range(2):
        before = (c > r) if d else (c < r)
        out[d, MSK_STRICT] = same & before
        out[d, MSK_INCL] = same & (before | (r == c))
        out[d, MSK_EYE] = r == c
        out[d, MSK_DIAG8] = ((r // 8) == (c // 8)) & before
        for idx, s in ((MSK_OFF8, 8), (MSK_OFF16, 16), (MSK_OFF32, 32)):
            blk = (r // (2 * s)) == (c // (2 * s))
            rh, ch = (r // s) % 2, (c // s) % 2
            out[d, idx] = blk & ((rh == 0) & (ch == 1) if d else (rh == 1) & (ch == 0))
    return jnp.asarray(out)


def _scan_pre_body(r_ref, v_ref, kk_ref, lw_ref, kd_ref, b_ref, tri_ref, msk_ref, rhat_ref, y0_ref, m_ref, n_ref, *, nchunk):
    c, n = CHUNK, 2 * CHUNK
    chains = [(d, ci) for d in range(2) for ci in range(nchunk)]
    each = lambda f, *cols: [f(*xs) for xs in zip(*cols)]
    msk = lambda d, k: msk_ref[d, k]
    rows = lambda ci: slice(ci * c, (ci + 1) * c)

    lw = [lw_ref[d, rows(ci), :] for d, ci in chains]
    cum = [_dot_exact_lhs(tri_ref[d], x) for (d, _), x in zip(chains, lw)]
    tot = [x[0:1] if d else x[c - 1:c] for (d, _), x in zip(chains, cum)]
    g_inv = each(lambda x: jnp.exp(-x), cum)
    g_tail = each(lambda t, x: jnp.exp(t - x), tot, cum)
    atp = [_pair_rows(-kk_ref[rows(ci), :] * jnp.exp(x - l)) for (_, ci), x, l in zip(chains, cum, lw)]
    rtp = [_pair_rows(r_ref[rows(ci), :] * jnp.exp(x)) for (_, ci), x in zip(chains, cum)]
    btp = [_pair_rows(b_ref[d, rows(ci), :] * g) for (d, ci), g in zip(chains, g_inv)]
    ktp = [_pair_rows(kd_ref[d, rows(ci), :] * g) for (d, ci), g in zip(chains, g_inv)]
    bhp = [_pair_rows(b_ref[d, rows(ci), :] * g) for (d, ci), g in zip(chains, g_tail)]
    khp = [_pair_rows(kd_ref[d, rows(ci), :] * g) for (d, ci), g in zip(chains, g_tail)]
    vp = [_pair_rows(v_ref[rows(ci), :]) for _, ci in chains]

    big = each(lambda a, r, b, k: _dot_nt(jnp.concatenate([a, r], axis=0), jnp.concatenate([b, k], axis=0)),
               atp, rtp, btp, ktp)
    a_ab = [jnp.where(msk(d, MSK_STRICT) > 0.0, x[:n, :n], 0.0) for (d, _), x in zip(chains, big)]
    a_ak = [jnp.where(msk(d, MSK_STRICT) > 0.0, x[:n, n:], 0.0) for (d, _), x in zip(chains, big)]
    a_rb = [jnp.where(msk(d, MSK_INCL) > 0.0, x[n:, :n], 0.0) for (d, _), x in zip(chains, big)]
    a_rk = [jnp.where(msk(d, MSK_INCL) > 0.0, x[n:, n:], 0.0) for (d, _), x in zip(chains, big)]

    n8 = [x * msk(d, MSK_DIAG8) for (d, _), x in zip(chains, a_ab)]
    t = [msk(d, MSK_EYE) + x for (d, _), x in zip(chains, n8)]
    n2 = each(_dot, n8, n8)
    t = each(lambda x, p: x + _dot(p, x), t, n2)
    n4 = each(_dot, n2, n2)
    t = each(lambda x, p: x + _dot(p, x), t, n4)
    for idx in (MSK_OFF8, MSK_OFF16, MSK_OFF32):
        off = [x * msk(d, idx) for (d, _), x in zip(chains, a_ab)]
        tn = each(_dot, t, off)
        t = each(lambda x, p: x + _dot(p, x), t, tn)

    akv = each(_dot, a_ak, vp)
    w_u = each(lambda ti, a, x: _dot(ti, jnp.concatenate([a, x], axis=1)), t, atp, akv)
    ry = each(_dot, a_rb, w_u)
    rkv = each(_dot, a_rk, vp)
    mn = each(_dot_tn, bhp, w_u)
    kv = each(_dot_tn, khp, vp)
    for i, (d, ci) in enumerate(chains):
        rhat_p = rtp[i] + ry[i][:, :n]
        y0_p = ry[i][:, n:] + rkv[i]
        rhat_ref[d, rows(ci), :] = (rhat_p[:c] + rhat_p[c:]).astype(rhat_ref.dtype)
        y0_ref[d, rows(ci), :] = (y0_p[:c] + y0_p[c:]).astype(y0_ref.dtype)
        m_i = mn[i][:, :n] + msk(d, MSK_EYE) * jnp.exp(tot[i])
        n_i = mn[i][:, n:] + kv[i]
        m_ref[d, ci, 0] = m_i[:c] + m_i[c:]
        n_ref[d, ci, 0] = n_i[:c] + n_i[c:]


def _scan_seq_body(s0_ref, mf_ref, mb_ref, nf_ref, nb_ref, rf_ref, rb_ref, yf0_ref, yb0_ref,
                   yf_ref, yb_ref, sfin_ref, s_ref):
    c = pl.program_id(0)

    @pl.when(c == 0)
    def _():
        s_ref[...] = s0_ref[...]

    for sub in range(CHAIN_STEP):
        for d, (m_ref, n_ref, rh_ref, y0_ref, y_ref) in enumerate(((mf_ref, nf_ref, rf_ref, yf0_ref, yf_ref),
                                                                     (mb_ref, nb_ref, rb_ref, yb0_ref, yb_ref))):
            lc = sub if d == 0 else CHAIN_STEP - 1 - sub
            rows = slice(lc * CHUNK, (lc + 1) * CHUNK)
            for p in range(PAIRS):
                ln = slice(p * LANES, (p + 1) * LANES)
                s = s_ref[d, p]
                y_ref[rows, ln] = _dot(rh_ref[0, rows, ln], s) + y0_ref[0, rows, ln].astype(F32)
                s_ref[d, p] = _dot(_pair_rows(m_ref[0, lc, p]), s) + _pair_rows(n_ref[0, lc, p])

    @pl.when(c == pl.num_programs(0) - 1)
    def _():
        sfin_ref[...] = s_ref[...]


def scan_seq(s0, rhat, y0, m, n):
    t = rhat.shape[1]
    nc = t // (CHUNK * CHAIN_STEP)
    fwd5 = lambda c: (0, c, 0, 0, 0)
    bwd5 = lambda c: (1, nc - 1 - c, 0, 0, 0)
    mblk = (1, CHAIN_STEP, PAIRS, HEAD, LANES)
    rblk = (1, CHAIN_STEP * CHUNK, BRANCH_W)
    sblk = pl.BlockSpec((2, PAIRS, LANES, LANES), lambda c: (0, 0, 0, 0))
    ys = jax.ShapeDtypeStruct((t, BRANCH_W), F32)
    return pl.pallas_call(
        _scan_seq_body,
        grid=(nc,),
        in_specs=[sblk,
                  pl.BlockSpec(mblk, fwd5), pl.BlockSpec(mblk, bwd5),
                  pl.BlockSpec(mblk, fwd5), pl.BlockSpec(mblk, bwd5),
                  pl.BlockSpec(rblk, lambda c: (0, c, 0)), pl.BlockSpec(rblk, lambda c: (1, nc - 1 - c, 0)),
                  pl.BlockSpec(rblk, lambda c: (0, c, 0)), pl.BlockSpec(rblk, lambda c: (1, nc - 1 - c, 0))],
        out_specs=[pl.BlockSpec((CHAIN_STEP * CHUNK, BRANCH_W), lambda c: (c, 0)),
                   pl.BlockSpec((CHAIN_STEP * CHUNK, BRANCH_W), lambda c: (nc - 1 - c, 0)),
                   sblk],
        out_shape=[ys, ys, jax.ShapeDtypeStruct((2, PAIRS, LANES, LANES), F32)],
        scratch_shapes=[pltpu.VMEM((2, PAIRS, LANES, LANES), F32)],
        compiler_params=_cparams(1, 16 << 20),
        name="rwkv_chain",
    )(s0, m, m, n, n, rhat, rhat, y0, y0)


def _areadout_body(yf_ref, yb_ref, bonus_ref, g_ref, lng_ref, lnb_ref, e_ref, o_ref):
    e = e_ref[...]
    for p in range(BRANCH_W // LANES):
        ln = slice(p * LANES, (p + 1) * LANES)
        y = yf_ref[:, ln] + yb_ref[:, ln]
        mu = _dot_split_rhs(y, e) * (1.0 / HEAD)
        yc = y - mu
        var = _dot_split_rhs(yc * yc, e) * (1.0 / HEAD)
        yn = (yc * lax.rsqrt(var + A_GN_EPS)) * lng_ref[:, ln] + lnb_ref[:, ln]
        o_ref[:, ln] = ((yn + bonus_ref[:, ln]) * g_ref[:, ln]).astype(o_ref.dtype)


def a_readout(yf, yb, bonus, g, ap):
    t = yf.shape[0]
    tm = 256
    blk = pl.BlockSpec((tm, BRANCH_W), lambda i: (i, 0))
    prow = pl.BlockSpec((1, BRANCH_W), lambda i: (0, 0))
    return pl.pallas_call(
        _areadout_body,
        grid=(t // tm,),
        in_specs=[blk, blk, blk, blk, prow, prow, pl.BlockSpec((LANES, LANES), lambda i: (0, 0))],
        out_specs=blk,
        out_shape=jax.ShapeDtypeStruct((t, BRANCH_W), BF16),
        compiler_params=_cparams(1, 2 * 5 * tm * BRANCH_W * 4),
        name="rwkv_readout",
    )(yf, yb, bonus, g, ap["ln_g"], ap["ln_b"], ap["e2"])


def _bmix_body(u_ref, v_ref, lng_ref, lnb_ref, ws_ref, bs_ref, o_ref):
    u = _gelu_tanh(u_ref[...].astype(F32))
    v = _layernorm(_gelu_tanh(v_ref[...].astype(F32)), lng_ref[...], lnb_ref[...]).astype(BF16)
    for ci in range(u.shape[0] // B_CHUNK):
        rows = slice(ci * B_CHUNK, (ci + 1) * B_CHUNK)
        for g in range(B_GROUPS):
            ln = slice(g * LANES, (g + 1) * LANES)
            s = jnp.dot(ws_ref[g], v[rows, ln], preferred_element_type=F32) + bs_ref[:, ln]
            o_ref[rows, ln] = (u[rows, ln] * s).astype(o_ref.dtype)


def b_mix(pb, bp):
    t = pb.shape[0]
    tm = 256
    row = pl.BlockSpec((1, BRANCH_W), lambda i: (0, 0))
    return pl.pallas_call(
        _bmix_body,
        grid=(t // tm,),
        in_specs=[pl.BlockSpec((tm, BRANCH_W), lambda i: (i, 0)), pl.BlockSpec((tm, BRANCH_W), lambda i: (i, 1)),
                  row, row, pl.BlockSpec((B_GROUPS, B_CHUNK, B_CHUNK), lambda i: (0, 0, 0)),
                  pl.BlockSpec((B_CHUNK, BRANCH_W), lambda i: (0, 0))],
        out_specs=pl.BlockSpec((tm, BRANCH_W), lambda i: (i, 0)),
        out_shape=jax.ShapeDtypeStruct((t, BRANCH_W), BF16),
        compiler_params=_cparams(1, 12 * tm * BRANCH_W * 4),
        name="gmlp_mix",
    )(pb, pb, bp["ln_g"], bp["ln_b"], bp["ws"], bp["bs"])


def _conv_body(ac_ref, gc_ref, ap_ref, gp_ref, an_ref, gn_ref, dw_ref, dwb_ref, lng_ref, lnb_ref,
               o_ref, zs_ref, cs_ref, *, tm):
    i = pl.program_id(0)
    last = pl.num_programs(0) - 1
    glu = lambda a_ref, g_ref: a_ref[...].astype(F32) * _sigmoid(g_ref[...].astype(F32))
    zs_ref[0:CONV_HALO, :] = jnp.where(i == 0, 0.0, glu(ap_ref, gp_ref))
    zs_ref[CONV_HALO:CONV_HALO + tm, :] = glu(ac_ref, gc_ref)
    zs_ref[CONV_HALO + tm:, :] = jnp.where(i == last, 0.0, glu(an_ref, gn_ref))
    rb, lb = 32, 256
    win = rb + 2 * CONV_HALO
    first = CONV_HALO - C_KERNEL // 2
    for r0 in range(0, tm, rb):
        for l0 in range(0, BRANCH_W, lb):
            w = zs_ref[r0:r0 + win, l0:l0 + lb]
            acc = jnp.zeros((rb, lb), F32) + dwb_ref[:, l0:l0 + lb]
            for s in range(SUBLANES):
                ws = w if s == 0 else pltpu.roll(w, win - s, 0)
                for j in range(C_KERNEL):
                    if (first + j) % SUBLANES == s:
                        a = (first + j) - s
                        acc = acc + ws[a:a + rb] * dw_ref[j:j + 1, l0:l0 + lb]
            cs_ref[r0:r0 + rb, l0:l0 + lb] = acc
    o_ref[...] = _silu(_layernorm(cs_ref[...], lng_ref[...], lnb_ref[...])).astype(o_ref.dtype)


def conv_mix(pc, cp):
    t = pc.shape[0]
    tm = 256
    hb = tm // CONV_HALO
    nh = t // CONV_HALO
    cur = lambda c: pl.BlockSpec((tm, BRANCH_W), functools.partial(lambda i, c: (i, c), c=c))
    prev = lambda c: pl.BlockSpec((CONV_HALO, BRANCH_W), functools.partial(lambda i, c: (jnp.maximum(i * hb - 1, 0), c), c=c))
    nxt = lambda c: pl.BlockSpec((CONV_HALO, BRANCH_W), functools.partial(lambda i, c: (jnp.minimum((i + 1) * hb, nh - 1), c), c=c))
    row = pl.BlockSpec((1, BRANCH_W), lambda i: (0, 0))
    return pl.pallas_call(
        functools.partial(_conv_body, tm=tm),
        grid=(t // tm,),
        in_specs=[cur(0), cur(1), prev(0), prev(1), nxt(0), nxt(1),
                  pl.BlockSpec((C_KERNEL + 1, BRANCH_W), lambda i: (0, 0)), row, row, row],
        out_specs=pl.BlockSpec((tm, BRANCH_W), lambda i: (i, 0)),
        out_shape=jax.ShapeDtypeStruct((t, BRANCH_W), BF16),
        scratch_shapes=[pltpu.VMEM((tm + 2 * CONV_HALO, BRANCH_W), F32), pltpu.VMEM((tm, BRANCH_W), F32)],
        compiler_params=_cparams(1, 16 * tm * BRANCH_W * 4),
        name="conv_mix",
    )(pc, pc, pc, pc, pc, pc, cp["dw"], cp["dw_b"], cp["ln_g"], cp["ln_b"])


FFT_N1, FFT_N2 = 64, 128


def _dft_tables(t):
    two_pi = 2.0 * np.pi
    cidx = np.arange(D_GROUP_CH)
    ph = two_pi * np.outer(cidx, cidx) / D_GROUP_CH
    chan = np.concatenate([np.cos(ph), np.sin(ph)], axis=0)
    if t <= 256:
        n = np.arange(t)
        th = two_pi * np.outer(n, n) / t
        m2 = np.concatenate([np.cos(th), -np.sin(th)], axis=0)
        return None, m2.astype(np.float32), chan.astype(np.float32)
    n1, n2 = FFT_N1, FFT_N2
    assert t == n1 * n2
    k1 = np.arange(n1)
    tok = (n2 * np.arange(n1))[None, None, :] + np.arange(n2)[:, None, None]
    th = two_pi * (k1[None, :, None] * tok) / t
    g1 = np.concatenate([np.cos(th), -np.sin(th)], axis=1)
    q = np.arange(n2)
    th2 = two_pi * np.outer(q, q) / n2
    c2, s2 = np.cos(th2), np.sin(th2)
    m2 = np.block([[c2, s2], [-s2, c2]])
    return g1.astype(np.float32), m2.astype(np.float32), chan.astype(np.float32)


FFT_STEP = 4


def _fft1_body(x_ref, g_ref, o_ref):
    for q in range(FFT_STEP):
        ln = slice(q * BRANCH_W, (q + 1) * BRANCH_W)
        o_ref[:, ln] = _dot(g_ref[q], x_ref[:, ln])


def _fft2_body(z_ref, m2_ref, ch_ref, o_ref, *, n_out, scale, stacked):
    z = jnp.concatenate([z_ref[0], z_ref[1]], axis=0) if stacked else z_ref[...]
    x = _dot(m2_ref[...], z)
    xr, xi = x[:n_out], x[n_out:]
    for g in range(D_GROUPS):
        ln = slice(g * D_GROUP_CH, (g + 1) * D_GROUP_CH)
        f = _dot(xr[:, ln], ch_ref[0:D_GROUP_CH]) + _dot(xi[:, ln], ch_ref[D_GROUP_CH:])
        o_ref[:, ln] = (f * scale).astype(o_ref.dtype)


def fourier_mix(pd):
    t = pd.shape[0]
    g1, m2, chan = (None if a is None else jnp.asarray(a) for a in _dft_tables(t))
    scale = 1.0 / math.sqrt(t * D_GROUP_CH)
    chspec = pl.BlockSpec((2 * D_GROUP_CH, D_GROUP_CH), lambda i: (0, 0))
    if g1 is None:
        return pl.pallas_call(
            functools.partial(_fft2_body, n_out=t, scale=scale, stacked=False),
            grid=(1,),
            in_specs=[pl.BlockSpec((t, BRANCH_W), lambda i: (0, 0)), pl.BlockSpec((2 * t, t), lambda i: (0, 0)), chspec],
            out_specs=pl.BlockSpec((t, BRANCH_W), lambda i: (0, 0)),
            out_shape=jax.ShapeDtypeStruct((t, BRANCH_W), BF16),
            compiler_params=_cparams(1, 16 << 20),
            name="fourier_small",
        )(pd, m2, chan)
    n1, n2 = FFT_N1, FFT_N2
    z = pl.pallas_call(
        _fft1_body,
        grid=(n2 // FFT_STEP,),
        in_specs=[pl.BlockSpec((n1, FFT_STEP * BRANCH_W), lambda q: (0, q)),
                  pl.BlockSpec((FFT_STEP, 2 * n1, n1), lambda q: (q, 0, 0))],
        out_specs=pl.BlockSpec((2 * n1, FFT_STEP * BRANCH_W), lambda q: (0, q)),
        out_shape=jax.ShapeDtypeStruct((2 * n1, n2 * BRANCH_W), F32),
        compiler_params=_cparams(1, 8 << 20),
        name="fourier_stage1",
    )(pd.reshape(n1, n2 * BRANCH_W), g1)
    f = pl.pallas_call(
        functools.partial(_fft2_body, n_out=n2, scale=scale, stacked=True),
        grid=(n1,),
        in_specs=[pl.BlockSpec((2, None, n2, BRANCH_W), lambda k: (0, k, 0, 0)),
                  pl.BlockSpec((2 * n2, 2 * n2), lambda k: (0, 0)), chspec],
        out_specs=pl.BlockSpec((n2, BRANCH_W), lambda k: (0, k)),
        out_shape=jax.ShapeDtypeStruct((n2, n1 * BRANCH_W), BF16),
        compiler_params=_cparams(1, 16 << 20),
        name="fourier_stage2",
    )(z.reshape(2, n1, n2, BRANCH_W), m2, chan)
    return f.reshape(t, BRANCH_W)


def _align_a(x):
    return jnp.pad(x, [(0, 0)] * (x.ndim - 1) + [(0, A_PAD - x.shape[-1])])


def _pad_rows(x, before, rows):
    return jnp.pad(x, [(0, 0)] * (x.ndim - 2) + [(before, rows - before - x.shape[-2]), (0, 0)])


def _layer_params(l, w_in, a_mu, a_w0, a_w_up, a_a0, a_a_up, a_g_up, a_k_k, a_k_a, a_r_k, a_ln, a_w_out,
                  b_ln, b_ws, b_bs, b_w_out, c_dw, c_dw_b, c_ln, c_w_out, d_w_out, gate_w, gate_b, w_o,
                  ffn_w1, ffn_w3, ffn_w2):
    row = lambda v: v.reshape(1, -1)
    hid = np.arange(LANES) // HEAD
    e2 = jnp.asarray((hid[:, None] == hid[None, :]).astype(np.float32)).astype(BF16)
    g_up = jnp.stack([a_g_up[l][:LANES], _pad_rows(a_g_up[l][LANES:], 0, LANES)])
    ap = dict(mu=_align_a(row(a_mu[l])), w0=a_w0[l][:, None, :], w_up=_pad_rows(a_w_up[l], 0, LANES),
              a0=a_a0[l][:, None, :], a_up=_pad_rows(a_a_up[l], LORA_W, LANES), g_up=g_up,
              k_k=row(a_k_k[l]), k_a=row(a_k_a[l]), r_k=row(a_r_k[l]), ln_g=row(a_ln[l][0]), ln_b=row(a_ln[l][1]), e2=e2)
    bs_exp = jnp.repeat(jnp.swapaxes(b_bs[l], 0, 1), LANES, axis=1)
    return dict(
        layer=l, w_a=w_in[0], w_bcd=w_in[1], ap=ap,
        bp=dict(ln_g=row(b_ln[l][0]), ln_b=row(b_ln[l][1]), ws=b_ws[l].astype(BF16), bs=bs_exp),
        cp=dict(dw=_pad_rows(c_dw[l], 0, C_KERNEL + 1), dw_b=row(c_dw_b[l]), ln_g=row(c_ln[l][0]), ln_b=row(c_ln[l][1])),
        w_outs=[a_w_out, b_w_out, c_w_out, d_w_out],
        gate_w=gate_w, gate_b=row(gate_b[l]), w_o=w_o,
        w1=ffn_w1, w3=ffn_w3, w2=ffn_w2)


def _tri_tables():
    i = np.arange(CHUNK)
    lower = (i[None, :] <= i[:, None]).astype(np.float32)
    return jnp.asarray(np.stack([lower, lower.T])).astype(BF16)


def _rwkv_scan(pa, s0, ap, row_len):
    rhat, y0, m, n, bonus, g = rwkv_maps(pa, ap, row_len, _tri_tables(), _chunk_mask_tables())
    yf, yb, s_fin = scan_seq(s0, rhat, y0, m, n)
    return yf, yb, bonus, g, s_fin


def _in_proj_a(h, lp):
    return matmul(h, lp["w_a"], A_PAD, 0, 512, F32, "in_proj_a", layer=lp["layer"], w_t=True)


def _in_proj_bcd(h, lp, cast_merge_weights=False):
    l = lp["layer"]
    proj = functools.partial(matmul, h, lp["w_bcd"], tn=512, out_dtype=BF16, layer=l, w_t=True)
    if not cast_merge_weights:
        return (proj(n=2 * BRANCH_W, col0=0, name="in_proj_b"),
                proj(n=2 * BRANCH_W, col0=2 * BRANCH_W, name="in_proj_c"),
                proj(n=BRANCH_W, col0=4 * BRANCH_W, name="in_proj_d")), None
    half = D_MODEL // 2
    pb, (g_top,) = proj(n=2 * BRANCH_W, col0=0, name="in_proj_b", sides=[(lp["gate_w"], l, 0, half)])
    pc, (g_bot,) = proj(n=2 * BRANCH_W, col0=2 * BRANCH_W, name="in_proj_c", sides=[(lp["gate_w"], l, half, half)])
    pd, w_outs = proj(n=BRANCH_W, col0=4 * BRANCH_W, name="in_proj_d", sides=[(wo, l, 0, BRANCH_W) for wo in lp["w_outs"]])
    return (pb, pc, pd), ((g_top, g_bot), list(w_outs))


def _token_mix(h, s0, row_len, lp, pbcd, merge_w, cast_ffn_weights=False):
    pa = _in_proj_a(h, lp)
    yf, yb, bonus, g, s_fin = _rwkv_scan(pa, s0, lp["ap"], row_len)
    za = a_readout(yf, yb, bonus, g, lp["ap"])
    zb = b_mix(pbcd[0], lp["bp"])
    zc = conv_mix(pbcd[1], lp["cp"])
    zd = fourier_mix(pbcd[2])
    zs, w13 = [za, zb, zc, zd], None
    if cast_ffn_weights:
        sides = [(w, lp["layer"], 0, w.shape[1]) for w in (lp["w1"], lp["w3"])]
        merged, w13 = merge(h, zs, merge_w[0], lp["gate_b"], merge_w[1], sides=sides)
    else:
        merged = merge(h, zs, merge_w[0], lp["gate_b"], merge_w[1])
    mix = matmul(merged, lp["w_o"], D_MODEL, 0, 512, BF16, "out_proj", layer=lp["layer"])
    return mix, s_fin, w13


def _ffn(h2, lp, w13, w2_bf16=None):
    if w2_bf16 is None:
        u, (w2_bf16,) = ffn_up(h2, w13[0], w13[1], sides=[(lp["w2"], lp["layer"], 0, lp["w2"].shape[1])])
    else:
        u = ffn_up(h2, w13[0], w13[1])
    return matmul(u, w2_bf16, D_MODEL, 0, 512, BF16, "ffn_down"), w2_bf16


def kernel(x, c, ctx, c_ctx, mod_w, mod_b, norm_g, w_in, a_mu, a_w0, a_w_up, a_a0, a_a_up, a_g_up, a_k_k, a_k_a,
           a_r_k, a_ln, a_w_out, b_ln, b_ws, b_bs, b_w_out, c_dw, c_dw_b, c_ln, c_w_out, d_w_out, gate_w, gate_b,
           w_o, ffn_w1, ffn_w3, ffn_w2):
    depth = mod_w.shape[0]
    d = D_MODEL
    c_cols = jnp.concatenate([c.reshape(d, 1), c_ctx.reshape(d, 1), jnp.zeros((d, 6), F32)], axis=1)
    mods = modulation(c_cols, mod_w, mod_b.reshape(depth, 1, 6 * d))
    x_lat, x_ctx = x[0], ctx[0]
    zero_state = jnp.zeros((2, PAIRS, LANES, LANES), F32)
    w_in_t = jnp.swapaxes(w_in, 1, 2)
    w_in_packed = (round_rows(w_in_t, 0, A_PAD, 256), round_rows(w_in_t, A_PROJ, w_in.shape[2] - A_PROJ, 160))
    weights = (w_in_packed, a_mu, a_w0, a_w_up, a_a0, a_a_up, a_g_up, a_k_k, a_k_a, a_r_k, a_ln, a_w_out, b_ln, b_ws, b_bs,
               b_w_out, c_dw, c_dw_b, c_ln, c_w_out, d_w_out, gate_w, gate_b, w_o, ffn_w1, ffn_w3, ffn_w2)
    h_lat = h_ctx = None
    for l in range(depth):
        last = l == depth - 1
        lp = _layer_params(l, *weights)
        ng = [norm_g[l, i].reshape(1, d) for i in range(4)]
        ml = [mods[l, 0:1, i * d:(i + 1) * d] for i in range(6)]
        mc = [mods[l, 1:2, i * d:(i + 1) * d] for i in range(6)]
        if l == 0:
            h_lat = normmod(x_lat, ng[0], ml[0], ml[1])
            h_ctx = normmod(x_ctx, ng[0], mc[0], mc[1])

        pbcd_lat, merge_w = _in_proj_bcd(h_lat, lp, cast_merge_weights=True)

        if last:
            ctx_states = _rwkv_scan(_in_proj_a(h_ctx, lp), zero_state, lp["ap"], x_ctx.shape[0])[4]
        else:
            pbcd_ctx, _ = _in_proj_bcd(h_ctx, lp)
            mix_ctx, ctx_states, _ = _token_mix(h_ctx, zero_state, x_ctx.shape[0], lp, pbcd_ctx, merge_w)

        mix_lat, _, w13 = _token_mix(h_lat, ctx_states, GRID_W, lp, pbcd_lat, merge_w, cast_ffn_weights=True)
        x_lat, h2 = resnorm(x_lat, mix_lat, ml[2], ng[1], (ng[2], ml[3], ml[4]))
        y, w2_bf16 = _ffn(h2, lp, w13)
        if last:
            x_lat = resnorm(x_lat, y, ml[5], ng[3])
        else:
            ngn = norm_g[l + 1, 0].reshape(1, d)
            mln = [mods[l + 1, 0:1, i * d:(i + 1) * d] for i in range(2)]
            mcn = [mods[l + 1, 1:2, i * d:(i + 1) * d] for i in range(2)]
            x_lat, h_lat = resnorm(x_lat, y, ml[5], ng[3], (ngn, mln[0], mln[1]))
            x_ctx, h2c = resnorm(x_ctx, mix_ctx, mc[2], ng[1], (ng[2], mc[3], mc[4]))
            yc, _ = _ffn(h2c, lp, w13, w2_bf16)
            x_ctx, h_ctx = resnorm(x_ctx, yc, mc[5], ng[3], (ngn, mcn[0], mcn[1]))
    return x_lat[None]
```

```python
import functools
import math

import numpy as np
import jax
import jax.numpy as jnp
from jax import lax
from jax.experimental import pallas as pl
from jax.experimental.pallas import tpu as pltpu

F32, BF16 = jnp.float32, jnp.bfloat16

D_MODEL = 4096
DEPTH = 2
GRID_W = 64
BRANCH_W = 1024
HEAD = 64
HEADS = BRANCH_W // HEAD
PAIRS = HEADS // 2
LORA_W, LORA_A, LORA_G = 64, 64, 160
A_PROJ = 3 * BRANCH_W + LORA_W + LORA_A + LORA_G
A_GN_EPS = 64e-5
B_CHUNK = 128
B_GROUPS = 8
C_KERNEL = 31
D_GROUPS = 4
D_GROUP_CH = BRANCH_W // D_GROUPS
FFN_HIDDEN = 11008
RMS_EPS = 1e-6
LN_EPS = 1e-5

LANES = 128
SUBLANES = 8
VMEM_BUDGET = 60 * 1024 * 1024

A_LORA = 3 * BRANCH_W
A_GC = A_LORA + LANES
A_PAD = A_GC + 3 * LANES

MERGE_TM = 512
FFN_UP_TM = 2048
CHUNK = 64
CHAIN_STEP = 4
CONV_HALO = 16


def _cparams(n_axes, vmem_bytes):
    limit = int(min(max(vmem_bytes + (8 << 20), 32 << 20), VMEM_BUDGET))
    return pltpu.CompilerParams(dimension_semantics=("arbitrary",) * n_axes, vmem_limit_bytes=limit)


def _dot(a, b):
    return jnp.dot(a.astype(BF16), b.astype(BF16), preferred_element_type=F32)


def _dot_nt(a, b):
    return lax.dot_general(a.astype(BF16), b.astype(BF16), (((1,), (1,)), ((), ())), preferred_element_type=F32)


def _dot_tn(a, b):
    return lax.dot_general(a.astype(BF16), b.astype(BF16), (((0,), (0,)), ((), ())), preferred_element_type=F32)


def _dot_split_rhs(x, e):
    hi = x.astype(BF16)
    lo = (x - hi.astype(F32)).astype(BF16)
    f = lambda p: jnp.dot(p, e, preferred_element_type=F32)
    return f(hi) + f(lo)


def _dot_split_lhs(e, x):
    hi = x.astype(BF16)
    lo = (x - hi.astype(F32)).astype(BF16)
    f = lambda p: jnp.dot(e, p, preferred_element_type=F32)
    return f(hi) + f(lo)


def _sigmoid(x):
    return 1.0 / (1.0 + jnp.exp(-x))


def _silu(x):
    return x * _sigmoid(x)


def _softplus(x):
    return jnp.maximum(x, 0.0) + jnp.log(1.0 + jnp.exp(-jnp.abs(x)))


def _gelu_tanh(x):
    return 0.5 * x * (1.0 + jnp.tanh(math.sqrt(2.0 / math.pi) * (x + 0.044715 * (x * x * x))))


def _rmsnorm(x, g):
    return (x * lax.rsqrt(jnp.mean(x * x, axis=-1, keepdims=True) + RMS_EPS)) * g


def _layernorm(x, g, b):
    mu = jnp.mean(x, axis=-1, keepdims=True)
    xc = x - mu
    var = jnp.mean(xc * xc, axis=-1, keepdims=True)
    return (xc * lax.rsqrt(var + LN_EPS)) * g + b


MOD_ROWS = 32
MOD_COLS = 512


def _mod_body(c_ref, w_ref, b_ref, o_ref, s0_ref, s1_ref):
    d, tn = w_ref.shape[1], w_ref.shape[2]

    @pl.when((pl.program_id(0) == 0) & (pl.program_id(1) == 0))
    def _():
        s = _silu(c_ref[...])
        s0_ref[...] = jnp.broadcast_to(s[:, 0:1], (d, LANES))
        s1_ref[...] = jnp.broadcast_to(s[:, 1:2], (d, LANES))

    for c0 in range(0, tn, MOD_COLS):
        def step(k, acc):
            r0 = pl.multiple_of(k * MOD_ROWS, MOD_ROWS)
            w = w_ref[0, pl.ds(r0, MOD_ROWS), c0:c0 + MOD_COLS]
            s0 = jnp.concatenate([s0_ref[pl.ds(r0, MOD_ROWS), :]] * (MOD_COLS // LANES), axis=1)
            s1 = jnp.concatenate([s1_ref[pl.ds(r0, MOD_ROWS), :]] * (MOD_COLS // LANES), axis=1)
            return acc[0] + w * s0, acc[1] + w * s1

        z = jnp.zeros((MOD_ROWS, MOD_COLS), F32)
        a0, a1 = lax.fori_loop(0, d // MOD_ROWS, step, (z, z), unroll=2)
        bias = b_ref[0, :, c0:c0 + MOD_COLS]
        o_ref[0, :, c0:c0 + MOD_COLS] = jnp.concatenate(
            [jnp.sum(a0, axis=0, keepdims=True) + bias, jnp.sum(a1, axis=0, keepdims=True) + bias,
             jnp.zeros((6, MOD_COLS), F32)], axis=0)


def modulation(c_cols, mod_w, mod_b):
    depth, d, n = mod_w.shape
    tn = 1024
    return pl.pallas_call(
        _mod_body,
        grid=(depth, n // tn),
        in_specs=[pl.BlockSpec((d, 8), lambda l, j: (0, 0)),
                  pl.BlockSpec((1, d, tn), lambda l, j: (l, 0, j)),
                  pl.BlockSpec((1, 1, tn), lambda l, j: (l, 0, j))],
        out_specs=pl.BlockSpec((1, 8, tn), lambda l, j: (l, 0, j)),
        out_shape=jax.ShapeDtypeStruct((depth, 8, n), F32),
        scratch_shapes=[pltpu.VMEM((d, LANES), F32), pltpu.VMEM((d, LANES), F32)],
        compiler_params=_cparams(2, 2 * d * tn * 4 + 5 * d * LANES * 4),
        name="modulation",
    )(c_cols, mod_w, mod_b)


def _normmod_body(x_ref, g_ref, sh_ref, sc_ref, h_ref):
    h = _rmsnorm(x_ref[...], g_ref[...]) * (1.0 + sc_ref[...]) + sh_ref[...]
    h_ref[...] = h.astype(h_ref.dtype)


def normmod(x, g, shift, scale):
    t, d = x.shape
    tm = 256
    row = pl.BlockSpec((1, d), lambda i: (0, 0))
    return pl.pallas_call(
        _normmod_body,
        grid=(t // tm,),
        in_specs=[pl.BlockSpec((tm, d), lambda i: (i, 0)), row, row, row],
        out_specs=pl.BlockSpec((tm, d), lambda i: (i, 0)),
        out_shape=jax.ShapeDtypeStruct((t, d), BF16),
        compiler_params=_cparams(1, 2 * tm * d * 6),
        name="normmod",
    )(x, g, shift, scale)


def _resnorm_body(x_ref, y_ref, gate_ref, gpost_ref, gpre_ref, sh_ref, sc_ref, xo_ref, h_ref):
    xn = x_ref[...] + gate_ref[...] * _rmsnorm(y_ref[...].astype(F32), gpost_ref[...])
    xo_ref[...] = xn
    h = _rmsnorm(xn, gpre_ref[...]) * (1.0 + sc_ref[...]) + sh_ref[...]
    h_ref[...] = h.astype(h_ref.dtype)


def _res_body(x_ref, y_ref, gate_ref, gpost_ref, xo_ref):
    xo_ref[...] = x_ref[...] + gate_ref[...] * _rmsnorm(y_ref[...].astype(F32), gpost_ref[...])


def resnorm(x, y, gate, g_post, nxt=None):
    t, d = x.shape
    tm = 256
    row = pl.BlockSpec((1, d), lambda i: (0, 0))
    tile = pl.BlockSpec((tm, d), lambda i: (i, 0))
    if nxt is None:
        return pl.pallas_call(
            _res_body, grid=(t // tm,), in_specs=[tile, tile, row, row], out_specs=tile,
            out_shape=jax.ShapeDtypeStruct((t, d), F32),
            compiler_params=_cparams(1, 2 * tm * d * 12), name="residual",
        )(x, y, gate, g_post)
    return pl.pallas_call(
        _resnorm_body, grid=(t // tm,), in_specs=[tile, tile, row, row, row, row, row],
        out_specs=[tile, tile],
        out_shape=[jax.ShapeDtypeStruct((t, d), F32), jax.ShapeDtypeStruct((t, d), BF16)],
        compiler_params=_cparams(1, 2 * tm * d * 14), name="residual_norm",
    )(x, y, gate, g_post, *nxt)


def _mm_body(x_ref, w_ref, *rest, n_side, cast, w_t):
    side_in, o_ref, side_out = rest[:n_side], rest[n_side], rest[n_side + 1:2 * n_side + 1]
    for s_in, s_out in zip(side_in, side_out):
        s_out[...] = s_in[...].astype(BF16)
    if cast:
        wb_ref = rest[-1]

        @pl.when(pl.program_id(1) == 0)
        def _():
            wb_ref[...] = w_ref[...].astype(BF16)

        w = wb_ref[...]
    else:
        w = w_ref[...]
    if w_t:
        acc = lax.dot_general(x_ref[...], w, (((1,), (1,)), ((), ())), preferred_element_type=F32)
    else:
        acc = jnp.dot(x_ref[...], w, preferred_element_type=F32)
    o_ref[...] = acc.astype(o_ref.dtype)


def _wspec(w, layer, rows, tn, col_block, w_t=False):
    if w_t:
        return pl.BlockSpec((None, tn, rows), lambda j, i: (layer, col_block(j), 0))
    if w.ndim == 3:
        return pl.BlockSpec((None, rows, tn), lambda j, i: (layer, 0, col_block(j)))
    return pl.BlockSpec((rows, tn), lambda j, i: (0, col_block(j)))


def _side_specs(sides, nj, ni):
    side_in, side_out, side_shape, side_args, vm = [], [], [], [], 0
    for arr, lyr, row0, nrows in sides:
        rps, cols = nrows // (nj * ni), arr.shape[2]
        assert rps * nj * ni == nrows and rps % 16 == 0 and row0 % rps == 0
        side_in.append(pl.BlockSpec((None, rps, cols),
                                    functools.partial(lambda j, i, lyr, b0: (lyr, b0 + j * ni + i, 0), lyr=lyr, b0=row0 // rps)))
        side_out.append(pl.BlockSpec((rps, cols), lambda j, i: (j * ni + i, 0)))
        side_shape.append(jax.ShapeDtypeStruct((nrows, cols), BF16))
        side_args.append(arr)
        vm += 2 * rps * cols * 6
    return side_in, side_out, side_shape, side_args, vm


def matmul(x, w, n, col0, tn, out_dtype, name, layer=None, sides=(), w_t=False, tm_max=None):
    m, k = x.shape
    tm = min(m, tm_max or (1024 if k <= 4096 else 512))
    j0 = col0 // tn
    assert col0 % tn == 0 and n % tn == 0 and m % tm == 0
    nj, ni = n // tn, m // tm
    cast = w.dtype == F32
    assert not (cast and w_t)
    vm = 2 * (tm * k * 2 + k * tn * w.dtype.itemsize + tm * tn * 4) + (k * tn * 2 if cast else 0)
    side_in, side_out, side_shape, side_args, side_vm = _side_specs(sides, nj, ni)
    vm += side_vm
    out = pl.pallas_call(
        functools.partial(_mm_body, n_side=len(sides), cast=cast, w_t=w_t),
        grid=(nj, ni),
        in_specs=[pl.BlockSpec((tm, k), lambda j, i: (i, 0)),
                  _wspec(w, layer, k, tn, lambda j: j + j0, w_t)] + side_in,
        out_specs=[pl.BlockSpec((tm, tn), lambda j, i: (i, j))] + side_out,
        out_shape=[jax.ShapeDtypeStruct((m, n), out_dtype)] + side_shape,
        scratch_shapes=[pltpu.VMEM((k, tn), BF16)] if cast else [],
        compiler_params=_cparams(2, vm),
        name=name,
    )(x, w, *side_args)
    return (out[0], out[1:]) if sides else out[0]


def _round_rows_body(w_ref, o_ref):
    o_ref[...] = w_ref[...].astype(BF16)


def round_rows(w, row0, nrows, blk):
    depth, _, cols = w.shape
    assert row0 % blk == 0 and nrows % blk == 0 and blk % 16 == 0
    b0 = row0 // blk
    return pl.pallas_call(
        _round_rows_body,
        grid=(depth, nrows // blk),
        in_specs=[pl.BlockSpec((None, blk, cols), lambda l, i: (l, b0 + i, 0))],
        out_specs=pl.BlockSpec((None, blk, cols), lambda l, i: (l, i, 0)),
        out_shape=jax.ShapeDtypeStruct((depth, nrows, cols), BF16),
        compiler_params=_cparams(2, 2 * blk * cols * 6),
        name="round_rows",
    )(w)


def _ffn_up_body(x_ref, w1_ref, w3_ref, *rest, n_side):
    side_in, o_ref, side_out = rest[:n_side], rest[n_side], rest[n_side + 1:]
    for s_in, s_out in zip(side_in, side_out):
        s_out[...] = s_in[...].astype(BF16)
    x = x_ref[...]
    a = jnp.dot(x, w1_ref[...], preferred_element_type=F32)
    b = jnp.dot(x, w3_ref[...], preferred_element_type=F32)
    o_ref[...] = (_silu(a) * b).astype(o_ref.dtype)


def ffn_up(h, w1, w3, sides=()):
    m, k = h.shape
    n = w1.shape[-1]
    tm, tn = min(m, FFN_UP_TM), 256
    nj, ni = n // tn, m // tm
    vm = 2 * (tm * k * 2 + 2 * k * tn * 2 + tm * tn * 2) + 5 * tm * tn * 4
    side_in, side_out, side_shape, side_args, side_vm = _side_specs(sides, nj, ni)
    out = pl.pallas_call(
        functools.partial(_ffn_up_body, n_side=len(sides)),
        grid=(nj, ni),
        in_specs=[pl.BlockSpec((tm, k), lambda j, i: (i, 0)),
                  pl.BlockSpec((k, tn), lambda j, i: (0, j)),
                  pl.BlockSpec((k, tn), lambda j, i: (0, j))] + side_in,
        out_specs=[pl.BlockSpec((tm, tn), lambda j, i: (i, j))] + side_out,
        out_shape=[jax.ShapeDtypeStruct((m, n), BF16)] + side_shape,
        compiler_params=_cparams(2, vm + side_vm),
        name="ffn_up",
    )(h, w1, w3, *side_args)
    return (out[0], out[1:]) if sides else out[0]


def _merge_body(h_ref, *refs, n_side):
    z, g_top, g_bot, bias, w_out = refs[0:4], refs[4:8], refs[8:12], refs[12:16], refs[16:20]
    side_in, o_ref, side_out = refs[20:20 + n_side], refs[20 + n_side], refs[21 + n_side:]
    for s_in, s_out in zip(side_in, side_out):
        s_out[...] = s_in[...].astype(BF16)
    half = g_top[0].shape[0]
    h_top, h_bot = h_ref[:, :half], h_ref[:, half:]
    acc = None
    for br in range(4):
        logits = (jnp.dot(h_top, g_top[br][...], preferred_element_type=F32)
                  + jnp.dot(h_bot, g_bot[br][...], preferred_element_type=F32) + bias[br][...])
        y = jnp.dot(z[br][...], w_out[br][...], preferred_element_type=F32)
        acc = _sigmoid(logits) * y if acc is None else acc + _sigmoid(logits) * y
    o_ref[...] = acc.astype(o_ref.dtype)


def merge(h, zs, gate_halves, gate_b, w_outs, sides=()):
    m, d = h.shape
    bw = zs[0].shape[1]
    half = gate_halves[0].shape[0]
    tm, tn = min(m, MERGE_TM), 256
    nj = d // tn
    hspec = pl.BlockSpec((tm, d), lambda j, i: (i, 0))
    zspec = pl.BlockSpec((tm, bw), lambda j, i: (i, 0))
    gspecs = [pl.BlockSpec((half, tn), functools.partial(lambda j, i, br: (0, br * nj + j), br=br)) for br in range(4)]
    bspecs = [pl.BlockSpec((1, tn), functools.partial(lambda j, i, br: (0, br * nj + j), br=br)) for br in range(4)]
    wspec = pl.BlockSpec((bw, tn), lambda j, i: (0, j))
    vm = 2 * (tm * d * 2 + 4 * tm * bw * 2 + tm * tn * 2 + 4 * (d + bw) * tn * 2) + 3 * tm * tn * 4
    side_in, side_out, side_shape, side_args, side_vm = _side_specs(sides, nj, m // tm)
    out = pl.pallas_call(
        functools.partial(_merge_body, n_side=len(sides)),
        grid=(nj, m // tm),
        in_specs=[hspec] + [zspec] * 4 + gspecs + gspecs + bspecs + [wspec] * 4 + side_in,
        out_specs=[pl.BlockSpec((tm, tn), lambda j, i: (i, j))] + side_out,
        out_shape=[jax.ShapeDtypeStruct((m, d), BF16)] + side_shape,
        compiler_params=_cparams(2, vm + side_vm),
        name="merge",
    )(h, *zs, *([gate_halves[0]] * 4), *([gate_halves[1]] * 4), gate_b, gate_b, gate_b, gate_b, *w_outs, *side_args)
    return (out[0], out[1:]) if sides else out[0]


def _token_shift(x, mu, row_len):
    tm = x.shape[0]
    pos = lax.broadcasted_iota(jnp.int32, x.shape, 0) & (row_len - 1)
    prev = jnp.where(pos == 0, 0.0, pltpu.roll(x, 1, 0))
    nxt = jnp.where(pos == row_len - 1, 0.0, pltpu.roll(x, tm - 1, 0))
    return x + mu * (0.5 * (prev + nxt) - x)


def _aprep_body(r_ref, k_ref, v_ref, wa_ref, g1_ref, g2_ref,
                mur_ref, muk_ref, muv_ref, muwa_ref, mug1_ref, mug2_ref,
                w0_ref, wup_ref, a0_ref, aup_ref, gup_ref, kk_ref, ka_ref, rk_ref, e_ref,
                ro_ref, vo_ref, kko_ref, lw_ref, kd_ref, b_ref, bonus_ref, g_ref, *, row_len):
    e = e_ref[...]
    xr = _token_shift(r_ref[...], mur_ref[...], row_len)
    xk = _token_shift(k_ref[...], muk_ref[...], row_len)
    xv = _token_shift(v_ref[...], muv_ref[...], row_len)
    xwa = _token_shift(wa_ref[...], muwa_ref[...], row_len)
    xg1 = _token_shift(g1_ref[...], mug1_ref[...], row_len)
    xg2 = _token_shift(g2_ref[...], mug2_ref[...], row_len)
    kk = xk * kk_ref[...]
    kk = kk * lax.rsqrt(jnp.maximum(_dot_split_rhs(kk * kk, e), 1e-12))
    tw = jnp.tanh(xwa)
    bonus = None
    for d in range(2):
        w = -_softplus(-(w0_ref[d] + _dot(tw, wup_ref[d]))) - 0.5
        a = _sigmoid(a0_ref[d] + _dot(xwa, aup_ref[d]))
        kd = xk * (1.0 + (a - 1.0) * ka_ref[...])
        lw_ref[d] = -jnp.exp(w)
        kd_ref[d] = kd
        b_ref[d] = kk * a
        bn = _dot(xr * kd * rk_ref[...], e) * xv
        bonus = bn if bonus is None else bonus + bn
    ro_ref[...] = xr
    vo_ref[...] = xv
    kko_ref[...] = kk
    bonus_ref[...] = bonus
    g_ref[...] = _dot(_sigmoid(xg1), gup_ref[0]) + _dot(_sigmoid(xg2), gup_ref[1])


N_PREP_IN = 21


def _rwkv_maps_body(*refs, row_len, nchunk):
    prep_in, (tri_ref, msk_ref) = refs[:N_PREP_IN], refs[N_PREP_IN:N_PREP_IN + 2]
    rhat_ref, y0_ref, m_ref, n_ref, bonus_ref, g_ref = refs[N_PREP_IN + 2:N_PREP_IN + 8]
    r_s, v_s, kk_s, lw_s, kd_s, b_s = refs[N_PREP_IN + 8:]
    _aprep_body(*prep_in, r_s, v_s, kk_s, lw_s, kd_s, b_s, bonus_ref, g_ref, row_len=row_len)
    _scan_pre_body(r_s, v_s, kk_s, lw_s, kd_s, b_s, tri_ref, msk_ref, rhat_ref, y0_ref, m_ref, n_ref, nchunk=nchunk)


def rwkv_maps(pa, ap, row_len, tri, msk):
    t = pa.shape[0]
    nchunk = 4
    tm = nchunk * CHUNK
    nc = t // CHUNK
    nb = BRANCH_W // LANES
    col = lambda c: pl.BlockSpec((tm, LANES), functools.partial(lambda i, p, c: (i, c + p), c=c))
    fix = lambda c: pl.BlockSpec((tm, LANES), functools.partial(lambda i, p, c: (i, c), c=c))
    mcol = lambda c: pl.BlockSpec((1, LANES), functools.partial(lambda i, p, c: (0, c + p), c=c))
    mfix = lambda c: pl.BlockSpec((1, LANES), functools.partial(lambda i, p, c: (0, c), c=c))
    prow = pl.BlockSpec((1, LANES), lambda i, p: (0, p))
    p2 = pl.BlockSpec((2, 1, LANES), lambda i, p: (0, 0, p))
    up2 = pl.BlockSpec((2, LANES, LANES), lambda i, p: (0, 0, p))
    out1 = pl.BlockSpec((tm, LANES), lambda i, p: (i, p))
    out2 = pl.BlockSpec((2, tm, LANES), lambda i, p: (0, i, p))
    mspec = pl.BlockSpec((2, nchunk, 1, HEAD, LANES), lambda i, p: (0, i, p, 0, 0))
    s1 = jax.ShapeDtypeStruct((t, BRANCH_W), F32)
    s2 = jax.ShapeDtypeStruct((2, t, BRANCH_W), BF16)
    sm = jax.ShapeDtypeStruct((2, nc, PAIRS, HEAD, LANES), F32)
    lora, gc = A_LORA // LANES, A_GC // LANES
    tile = lambda lead=(): pltpu.VMEM(lead + (tm, LANES), F32)
    return pl.pallas_call(
        functools.partial(_rwkv_maps_body, row_len=row_len, nchunk=nchunk),
        grid=(t // tm, nb),
        in_specs=[col(0), col(nb), col(2 * nb), fix(lora), fix(gc), fix(gc + 1),
                  mcol(0), mcol(nb), mcol(2 * nb), mfix(lora), mfix(gc), mfix(gc + 1),
                  p2, up2, p2, up2, up2, prow, prow, prow,
                  pl.BlockSpec((LANES, LANES), lambda i, p: (0, 0)),
                  pl.BlockSpec((2, CHUNK, CHUNK), lambda i, p: (0, 0, 0)),
                  pl.BlockSpec(msk.shape, lambda i, p: (0, 0, 0, 0))],
        out_specs=[out2, out2, mspec, mspec, out1, out1],
        out_shape=[s2, s2, sm, sm, s1, s1],
        scratch_shapes=[tile(), tile(), tile(), tile((2,)), tile((2,)), tile((2,))],
        compiler_params=_cparams(2, 24 << 20),
        name="rwkv_chunk_maps",
    )(pa, pa, pa, pa, pa, pa, ap["mu"], ap["mu"], ap["mu"], ap["mu"], ap["mu"], ap["mu"],
      ap["w0"], ap["w_up"], ap["a0"], ap["a_up"], ap["g_up"], ap["k_k"], ap["k_a"], ap["r_k"], ap["e2"], tri, msk)


def _pair_rows(x):
    lane = lax.broadcasted_iota(jnp.int32, x.shape, 1)
    return jnp.concatenate([jnp.where(lane < HEAD, x, 0.0), jnp.where(lane >= HEAD, x, 0.0)], axis=0)


MSK_STRICT, MSK_INCL, MSK_EYE, MSK_DIAG8, MSK_OFF8, MSK_OFF16, MSK_OFF32 = range(7)


def _chunk_mask_tables():
    n = 2 * CHUNK
    r = np.arange(n)[:, None]
    c = np.arange(n)[None, :]
    same = (r // CHUNK) == (c // CHUNK)
    out = np.zeros((2, 7, n, n), np.float32)
    for d in range(2):
        before = (c > r) if d else (c < r)
        out[d, MSK_STRICT] = same & before
        out[d, MSK_INCL] = same & (before | (r == c))
        out[d, MSK_EYE] = r == c
        out[d, MSK_DIAG8] = ((r // 8) == (c // 8)) & before
        for idx, s in ((MSK_OFF8, 8), (MSK_OFF16, 16), (MSK_OFF32, 32)):
            blk = (r // (2 * s)) == (c // (2 * s))
            rh, ch = (r // s) % 2, (c // s) % 2
            out[d, idx] = blk & ((rh == 0) & (ch == 1) if d else (rh == 1) & (ch == 0))
    return jnp.asarray(out)


def _scan_pre_body(r_ref, v_ref, kk_ref, lw_ref, kd_ref, b_ref, tri_ref, msk_ref, rhat_ref, y0_ref, m_ref, n_ref, *, nchunk):
    c, n = CHUNK, 2 * CHUNK
    chains = [(d, ci) for d in range(2) for ci in range(nchunk)]
    each = lambda f, *cols: [f(*xs) for xs in zip(*cols)]
    msk = lambda d, k: msk_ref[d, k]
    rows = lambda ci: slice(ci * c, (ci + 1) * c)

    lw = [lw_ref[d, rows(ci), :] for d, ci in chains]
    cum = [_dot_split_lhs(tri_ref[d], x) for (d, _), x in zip(chains, lw)]
    tot = [x[0:1] if d else x[c - 1:c] for (d, _), x in zip(chains, cum)]
    g_inv = each(lambda x: jnp.exp(-x), cum)
    g_tail = each(lambda t, x: jnp.exp(t - x), tot, cum)
    atp = [_pair_rows(-kk_ref[rows(ci), :] * jnp.exp(x - l)) for (_, ci), x, l in zip(chains, cum, lw)]
    rtp = [_pair_rows(r_ref[rows(ci), :] * jnp.exp(x)) for (_, ci), x in zip(chains, cum)]
    btp = [_pair_rows(b_ref[d, rows(ci), :] * g) for (d, ci), g in zip(chains, g_inv)]
    ktp = [_pair_rows(kd_ref[d, rows(ci), :] * g) for (d, ci), g in zip(chains, g_inv)]
    bhp = [_pair_rows(b_ref[d, rows(ci), :] * g) for (d, ci), g in zip(chains, g_tail)]
    khp = [_pair_rows(kd_ref[d, rows(ci), :] * g) for (d, ci), g in zip(chains, g_tail)]
    vp = [_pair_rows(v_ref[rows(ci), :]) for _, ci in chains]

    big = each(lambda a, r, b, k: _dot_nt(jnp.concatenate([a, r], axis=0), jnp.concatenate([b, k], axis=0)),
               atp, rtp, btp, ktp)
    a_ab = [jnp.where(msk(d, MSK_STRICT) > 0.0, x[:n, :n], 0.0) for (d, _), x in zip(chains, big)]
    a_ak = [jnp.where(msk(d, MSK_STRICT) > 0.0, x[:n, n:], 0.0) for (d, _), x in zip(chains, big)]
    a_rb = [jnp.where(msk(d, MSK_INCL) > 0.0, x[n:, :n], 0.0) for (d, _), x in zip(chains, big)]
    a_rk = [jnp.where(msk(d, MSK_INCL) > 0.0, x[n:, n:], 0.0) for (d, _), x in zip(chains, big)]

    n8 = [x * msk(d, MSK_DIAG8) for (d, _), x in zip(chains, a_ab)]
    t = [msk(d, MSK_EYE) + x for (d, _), x in zip(chains, n8)]
    n2 = each(_dot, n8, n8)
    t = each(lambda x, p: x + _dot(p, x), t, n2)
    n4 = each(_dot, n2, n2)
    t = each(lambda x, p: x + _dot(p, x), t, n4)
    for idx in (MSK_OFF8, MSK_OFF16, MSK_OFF32):
        off = [x * msk(d, idx) for (d, _), x in zip(chains, a_ab)]
        tn = each(_dot, t, off)
        t = each(lambda x, p: x + _dot(p, x), t, tn)

    akv = each(_dot, a_ak, vp)
    w_u = each(lambda ti, a, x: _dot(ti, jnp.concatenate([a, x], axis=1)), t, atp, akv)
    ry = each(_dot, a_rb, w_u)
    rkv = each(_dot, a_rk, vp)
    mn = each(_dot_tn, bhp, w_u)
    kv = each(_dot_tn, khp, vp)
    for i, (d, ci) in enumerate(chains):
        rhat_p = rtp[i] + ry[i][:, :n]
        y0_p = ry[i][:, n:] + rkv[i]
        rhat_ref[d, rows(ci), :] = (rhat_p[:c] + rhat_p[c:]).astype(rhat_ref.dtype)
        y0_ref[d, rows(ci), :] = (y0_p[:c] + y0_p[c:]).astype(y0_ref.dtype)
        m_i = mn[i][:, :n] + msk(d, MSK_EYE) * jnp.exp(tot[i])
        n_i = mn[i][:, n:] + kv[i]
        m_ref[d, ci, 0] = m_i[:c] + m_i[c:]
        n_ref[d, ci, 0] = n_i[:c] + n_i[c:]


def _scan_seq_body(s0_ref, mf_ref, mb_ref, nf_ref, nb_ref, rf_ref, rb_ref, yf0_ref, yb0_ref,
                   yf_ref, yb_ref, sfin_ref, s_ref):
    c = pl.program_id(0)

    @pl.when(c == 0)
    def _():
        s_ref[...] = s0_ref[...]

    for sub in range(CHAIN_STEP):
        for d, (m_ref, n_ref, rh_ref, y0_ref, y_ref) in enumerate(((mf_ref, nf_ref, rf_ref, yf0_ref, yf_ref),
                                                                     (mb_ref, nb_ref, rb_ref, yb0_ref, yb_ref))):
            lc = sub if d == 0 else CHAIN_STEP - 1 - sub
            rows = slice(lc * CHUNK, (lc + 1) * CHUNK)
            for p in range(PAIRS):
                ln = slice(p * LANES, (p + 1) * LANES)
                s = s_ref[d, p]
                y_ref[rows, ln] = _dot(rh_ref[0, rows, ln], s) + y0_ref[0, rows, ln].astype(F32)
                s_ref[d, p] = _dot(_pair_rows(m_ref[0, lc, p]), s) + _pair_rows(n_ref[0, lc, p])

    @pl.when(c == pl.num_programs(0) - 1)
    def _():
        sfin_ref[...] = s_ref[...]


def scan_seq(s0, rhat, y0, m, n):
    t = rhat.shape[1]
    nc = t // (CHUNK * CHAIN_STEP)
    fwd5 = lambda c: (0, c, 0, 0, 0)
    bwd5 = lambda c: (1, nc - 1 - c, 0, 0, 0)
    mblk = (1, CHAIN_STEP, PAIRS, HEAD, LANES)
    rblk = (1, CHAIN_STEP * CHUNK, BRANCH_W)
    sblk = pl.BlockSpec((2, PAIRS, LANES, LANES), lambda c: (0, 0, 0, 0))
    ys = jax.ShapeDtypeStruct((t, BRANCH_W), F32)
    return pl.pallas_call(
        _scan_seq_body,
        grid=(nc,),
        in_specs=[sblk,
                  pl.BlockSpec(mblk, fwd5), pl.BlockSpec(mblk, bwd5),
                  pl.BlockSpec(mblk, fwd5), pl.BlockSpec(mblk, bwd5),
                  pl.BlockSpec(rblk, lambda c: (0, c, 0)), pl.BlockSpec(rblk, lambda c: (1, nc - 1 - c, 0)),
                  pl.BlockSpec(rblk, lambda c: (0, c, 0)), pl.BlockSpec(rblk, lambda c: (1, nc - 1 - c, 0))],
        out_specs=[pl.BlockSpec((CHAIN_STEP * CHUNK, BRANCH_W), lambda c: (c, 0)),
                   pl.BlockSpec((CHAIN_STEP * CHUNK, BRANCH_W), lambda c: (nc - 1 - c, 0)),
                   sblk],
        out_shape=[ys, ys, jax.ShapeDtypeStruct((2, PAIRS, LANES, LANES), F32)],
        scratch_shapes=[pltpu.VMEM((2, PAIRS, LANES, LANES), F32)],
        compiler_params=_cparams(1, 16 << 20),
        name="rwkv_chain",
    )(s0, m, m, n, n, rhat, rhat, y0, y0)


def _areadout_body(yf_ref, yb_ref, bonus_ref, g_ref, lng_ref, lnb_ref, e_ref, o_ref):
    e = e_ref[...]
    for p in range(BRANCH_W // LANES):
        ln = slice(p * LANES, (p + 1) * LANES)
        y = yf_ref[:, ln] + yb_ref[:, ln]
        mu = _dot_split_rhs(y, e) * (1.0 / HEAD)
        yc = y - mu
        var = _dot_split_rhs(yc * yc, e) * (1.0 / HEAD)
        yn = (yc * lax.rsqrt(var + A_GN_EPS)) * lng_ref[:, ln] + lnb_ref[:, ln]
        o_ref[:, ln] = ((yn + bonus_ref[:, ln]) * g_ref[:, ln]).astype(o_ref.dtype)


def a_readout(yf, yb, bonus, g, ap):
    t = yf.shape[0]
    tm = 256
    blk = pl.BlockSpec((tm, BRANCH_W), lambda i: (i, 0))
    prow = pl.BlockSpec((1, BRANCH_W), lambda i: (0, 0))
    return pl.pallas_call(
        _areadout_body,
        grid=(t // tm,),
        in_specs=[blk, blk, blk, blk, prow, prow, pl.BlockSpec((LANES, LANES), lambda i: (0, 0))],
        out_specs=blk,
        out_shape=jax.ShapeDtypeStruct((t, BRANCH_W), BF16),
        compiler_params=_cparams(1, 2 * 5 * tm * BRANCH_W * 4),
        name="rwkv_readout",
    )(yf, yb, bonus, g, ap["ln_g"], ap["ln_b"], ap["e2"])


def _bmix_body(u_ref, v_ref, lng_ref, lnb_ref, ws_ref, bs_ref, o_ref):
    u = _gelu_tanh(u_ref[...].astype(F32))
    v = _layernorm(_gelu_tanh(v_ref[...].astype(F32)), lng_ref[...], lnb_ref[...]).astype(BF16)
    for ci in range(u.shape[0] // B_CHUNK):
        rows = slice(ci * B_CHUNK, (ci + 1) * B_CHUNK)
        for g in range(B_GROUPS):
            ln = slice(g * LANES, (g + 1) * LANES)
            s = jnp.dot(ws_ref[g], v[rows, ln], preferred_element_type=F32) + bs_ref[:, ln]
            o_ref[rows, ln] = (u[rows, ln] * s).astype(o_ref.dtype)


def b_mix(pb, bp):
    t = pb.shape[0]
    tm = 256
    row = pl.BlockSpec((1, BRANCH_W), lambda i: (0, 0))
    return pl.pallas_call(
        _bmix_body,
        grid=(t // tm,),
        in_specs=[pl.BlockSpec((tm, BRANCH_W), lambda i: (i, 0)), pl.BlockSpec((tm, BRANCH_W), lambda i: (i, 1)),
                  row, row, pl.BlockSpec((B_GROUPS, B_CHUNK, B_CHUNK), lambda i: (0, 0, 0)),
                  pl.BlockSpec((B_CHUNK, BRANCH_W), lambda i: (0, 0))],
        out_specs=pl.BlockSpec((tm, BRANCH_W), lambda i: (i, 0)),
        out_shape=jax.ShapeDtypeStruct((t, BRANCH_W), BF16),
        compiler_params=_cparams(1, 12 * tm * BRANCH_W * 4),
        name="gmlp_mix",
    )(pb, pb, bp["ln_g"], bp["ln_b"], bp["ws"], bp["bs"])


def _conv_body(ac_ref, gc_ref, ap_ref, gp_ref, an_ref, gn_ref, dw_ref, dwb_ref, lng_ref, lnb_ref,
               o_ref, zs_ref, cs_ref, *, tm):
    i = pl.program_id(0)
    last = pl.num_programs(0) - 1
    glu = lambda a_ref, g_ref: a_ref[...].astype(F32) * _sigmoid(g_ref[...].astype(F32))
    zs_ref[0:CONV_HALO, :] = jnp.where(i == 0, 0.0, glu(ap_ref, gp_ref))
    zs_ref[CONV_HALO:CONV_HALO + tm, :] = glu(ac_ref, gc_ref)
    zs_ref[CONV_HALO + tm:, :] = jnp.where(i == last, 0.0, glu(an_ref, gn_ref))
    rb, lb = 32, 256
    win = rb + 2 * CONV_HALO
    first = CONV_HALO - C_KERNEL // 2
    for r0 in range(0, tm, rb):
        for l0 in range(0, BRANCH_W, lb):
            w = zs_ref[r0:r0 + win, l0:l0 + lb]
            acc = jnp.zeros((rb, lb), F32) + dwb_ref[:, l0:l0 + lb]
            for s in range(SUBLANES):
                ws = w if s == 0 else pltpu.roll(w, win - s, 0)
                for j in range(C_KERNEL):
                    if (first + j) % SUBLANES == s:
                        a = (first + j) - s
                        acc = acc + ws[a:a + rb] * dw_ref[j:j + 1, l0:l0 + lb]
            cs_ref[r0:r0 + rb, l0:l0 + lb] = acc
    o_ref[...] = _silu(_layernorm(cs_ref[...], lng_ref[...], lnb_ref[...])).astype(o_ref.dtype)


def conv_mix(pc, cp):
    t = pc.shape[0]
    tm = 256
    hb = tm // CONV_HALO
    nh = t // CONV_HALO
    cur = lambda c: pl.BlockSpec((tm, BRANCH_W), functools.partial(lambda i, c: (i, c), c=c))
    prev = lambda c: pl.BlockSpec((CONV_HALO, BRANCH_W), functools.partial(lambda i, c: (jnp.maximum(i * hb - 1, 0), c), c=c))
    nxt = lambda c: pl.BlockSpec((CONV_HALO, BRANCH_W), functools.partial(lambda i, c: (jnp.minimum((i + 1) * hb, nh - 1), c), c=c))
    row = pl.BlockSpec((1, BRANCH_W), lambda i: (0, 0))
    return pl.pallas_call(
        functools.partial(_conv_body, tm=tm),
        grid=(t // tm,),
        in_specs=[cur(0), cur(1), prev(0), prev(1), nxt(0), nxt(1),
                  pl.BlockSpec((C_KERNEL + 1, BRANCH_W), lambda i: (0, 0)), row, row, row],
        out_specs=pl.BlockSpec((tm, BRANCH_W), lambda i: (i, 0)),
        out_shape=jax.ShapeDtypeStruct((t, BRANCH_W), BF16),
        scratch_shapes=[pltpu.VMEM((tm + 2 * CONV_HALO, BRANCH_W), F32), pltpu.VMEM((tm, BRANCH_W), F32)],
        compiler_params=_cparams(1, 16 * tm * BRANCH_W * 4),
        name="conv_mix",
    )(pc, pc, pc, pc, pc, pc, cp["dw"], cp["dw_b"], cp["ln_g"], cp["ln_b"])


FFT_N1, FFT_N2 = 64, 128


def _dft_tables(t):
    two_pi = 2.0 * np.pi
    cidx = np.arange(D_GROUP_CH)
    ph = two_pi * np.outer(cidx, cidx) / D_GROUP_CH
    chan = np.concatenate([np.cos(ph), np.sin(ph)], axis=0)
    if t <= 256:
        n = np.arange(t)
        th = two_pi * np.outer(n, n) / t
        m2 = np.concatenate([np.cos(th), -np.sin(th)], axis=0)
        return None, m2.astype(np.float32), chan.astype(np.float32)
    n1, n2 = FFT_N1, FFT_N2
    assert t == n1 * n2
    k1 = np.arange(n1)
    tok = (n2 * np.arange(n1))[None, None, :] + np.arange(n2)[:, None, None]
    th = two_pi * (k1[None, :, None] * tok) / t
    g1 = np.concatenate([np.cos(th), -np.sin(th)], axis=1)
    q = np.arange(n2)
    th2 = two_pi * np.outer(q, q) / n2
    c2, s2 = np.cos(th2), np.sin(th2)
    m2 = np.block([[c2, s2], [-s2, c2]])
    return g1.astype(np.float32), m2.astype(np.float32), chan.astype(np.float32)


FFT_STEP = 4


def _fft1_body(x_ref, g_ref, o_ref):
    for q in range(FFT_STEP):
        ln = slice(q * BRANCH_W, (q + 1) * BRANCH_W)
        o_ref[:, ln] = _dot(g_ref[q], x_ref[:, ln])


def _fft2_body(z_ref, m2_ref, ch_ref, o_ref, *, n_out, scale, stacked):
    z = jnp.concatenate([z_ref[0], z_ref[1]], axis=0) if stacked else z_ref[...]
    x = _dot(m2_ref[...], z)
    xr, xi = x[:n_out], x[n_out:]
    for g in range(D_GROUPS):
        ln = slice(g * D_GROUP_CH, (g + 1) * D_GROUP_CH)
        f = _dot(xr[:, ln], ch_ref[0:D_GROUP_CH]) + _dot(xi[:, ln], ch_ref[D_GROUP_CH:])
        o_ref[:, ln] = (f * scale).astype(o_ref.dtype)


def fourier_mix(pd):
    t = pd.shape[0]
    g1, m2, chan = (None if a is None else jnp.asarray(a) for a in _dft_tables(t))
    scale = 1.0 / math.sqrt(t * D_GROUP_CH)
    chspec = pl.BlockSpec((2 * D_GROUP_CH, D_GROUP_CH), lambda i: (0, 0))
    if g1 is None:
        return pl.pallas_call(
            functools.partial(_fft2_body, n_out=t, scale=scale, stacked=False),
            grid=(1,),
            in_specs=[pl.BlockSpec((t, BRANCH_W), lambda i: (0, 0)), pl.BlockSpec((2 * t, t), lambda i: (0, 0)), chspec],
            out_specs=pl.BlockSpec((t, BRANCH_W), lambda i: (0, 0)),
            out_shape=jax.ShapeDtypeStruct((t, BRANCH_W), BF16),
            compiler_params=_cparams(1, 16 << 20),
            name="fourier_small",
        )(pd, m2, chan)
    n1, n2 = FFT_N1, FFT_N2
    z = pl.pallas_call(
        _fft1_body,
        grid=(n2 // FFT_STEP,),
        in_specs=[pl.BlockSpec((n1, FFT_STEP * BRANCH_W), lambda q: (0, q)),
                  pl.BlockSpec((FFT_STEP, 2 * n1, n1), lambda q: (q, 0, 0))],
        out_specs=pl.BlockSpec((2 * n1, FFT_STEP * BRANCH_W), lambda q: (0, q)),
        out_shape=jax.ShapeDtypeStruct((2 * n1, n2 * BRANCH_W), F32),
        compiler_params=_cparams(1, 8 << 20),
        name="fourier_stage1",
    )(pd.reshape(n1, n2 * BRANCH_W), g1)
    f = pl.pallas_call(
        functools.partial(_fft2_body, n_out=n2, scale=scale, stacked=True),
        grid=(n1,),
        in_specs=[pl.BlockSpec((2, None, n2, BRANCH_W), lambda k: (0, k, 0, 0)),
                  pl.BlockSpec((2 * n2, 2 * n2), lambda k: (0, 0)), chspec],
        out_specs=pl.BlockSpec((n2, BRANCH_W), lambda k: (0, k)),
        out_shape=jax.ShapeDtypeStruct((n2, n1 * BRANCH_W), BF16),
        compiler_params=_cparams(1, 16 << 20),
        name="fourier_stage2",
    )(z.reshape(2, n1, n2, BRANCH_W), m2, chan)
    return f.reshape(t, BRANCH_W)


def _align_a(x):
    return jnp.pad(x, [(0, 0)] * (x.ndim - 1) + [(0, A_PAD - x.shape[-1])])


def _pad_rows(x, before, rows):
    return jnp.pad(x, [(0, 0)] * (x.ndim - 2) + [(before, rows - before - x.shape[-2]), (0, 0)])


def _layer_params(l, w_in, a_mu, a_w0, a_w_up, a_a0, a_a_up, a_g_up, a_k_k, a_k_a, a_r_k, a_ln, a_w_out,
                  b_ln, b_ws, b_bs, b_w_out, c_dw, c_dw_b, c_ln, c_w_out, d_w_out, gate_w, gate_b, w_o,
                  ffn_w1, ffn_w3, ffn_w2):
    row = lambda v: v.reshape(1, -1)
    hid = np.arange(LANES) // HEAD
    e2 = jnp.asarray((hid[:, None] == hid[None, :]).astype(np.float32)).astype(BF16)
    g_up = jnp.stack([a_g_up[l][:LANES], _pad_rows(a_g_up[l][LANES:], 0, LANES)])
    ap = dict(mu=_align_a(row(a_mu[l])), w0=a_w0[l][:, None, :], w_up=_pad_rows(a_w_up[l], 0, LANES),
              a0=a_a0[l][:, None, :], a_up=_pad_rows(a_a_up[l], LORA_W, LANES), g_up=g_up,
              k_k=row(a_k_k[l]), k_a=row(a_k_a[l]), r_k=row(a_r_k[l]), ln_g=row(a_ln[l][0]), ln_b=row(a_ln[l][1]), e2=e2)
    bs_exp = jnp.repeat(jnp.swapaxes(b_bs[l], 0, 1), LANES, axis=1)
    return dict(
        layer=l, w_a=w_in[0], w_bcd=w_in[1], ap=ap,
        bp=dict(ln_g=row(b_ln[l][0]), ln_b=row(b_ln[l][1]), ws=b_ws[l].astype(BF16), bs=bs_exp),
        cp=dict(dw=_pad_rows(c_dw[l], 0, C_KERNEL + 1), dw_b=row(c_dw_b[l]), ln_g=row(c_ln[l][0]), ln_b=row(c_ln[l][1])),
        w_outs=[a_w_out, b_w_out, c_w_out, d_w_out],
        gate_w=gate_w, gate_b=row(gate_b[l]), w_o=w_o,
        w1=ffn_w1, w3=ffn_w3, w2=ffn_w2)


def _tri_tables():
    i = np.arange(CHUNK)
    lower = (i[None, :] <= i[:, None]).astype(np.float32)
    return jnp.asarray(np.stack([lower, lower.T])).astype(BF16)


def _rwkv_scan(pa, s0, ap, row_len):
    rhat, y0, m, n, bonus, g = rwkv_maps(pa, ap, row_len, _tri_tables(), _chunk_mask_tables())
    yf, yb, s_fin = scan_seq(s0, rhat, y0, m, n)
    return yf, yb, bonus, g, s_fin


def _in_proj_a(h, lp):
    return matmul(h, lp["w_a"], A_PAD, 0, 512, F32, "in_proj_a", layer=lp["layer"], w_t=True, tm_max=2048)


def _in_proj_bcd(h, lp, cast_merge_weights=False):
    l = lp["layer"]
    proj = functools.partial(matmul, h, lp["w_bcd"], tn=512, out_dtype=BF16, layer=l, w_t=True)
    if not cast_merge_weights:
        return (proj(n=2 * BRANCH_W, col0=0, name="in_proj_b"),
                proj(n=2 * BRANCH_W, col0=2 * BRANCH_W, name="in_proj_c"),
                proj(n=BRANCH_W, col0=4 * BRANCH_W, name="in_proj_d")), None
    half = D_MODEL // 2
    pb, (g_top,) = proj(n=2 * BRANCH_W, col0=0, name="in_proj_b", sides=[(lp["gate_w"], l, 0, half)])
    pc, (g_bot,) = proj(n=2 * BRANCH_W, col0=2 * BRANCH_W, name="in_proj_c", sides=[(lp["gate_w"], l, half, half)])
    pd, w_outs = proj(n=BRANCH_W, col0=4 * BRANCH_W, name="in_proj_d", sides=[(wo, l, 0, BRANCH_W) for wo in lp["w_outs"]])
    return (pb, pc, pd), ((g_top, g_bot), list(w_outs))


def _token_mix(h, s0, row_len, lp, pbcd, merge_w, cast_ffn_weights=False):
    pa = _in_proj_a(h, lp)
    yf, yb, bonus, g, s_fin = _rwkv_scan(pa, s0, lp["ap"], row_len)
    za = a_readout(yf, yb, bonus, g, lp["ap"])
    zb = b_mix(pbcd[0], lp["bp"])
    zc = conv_mix(pbcd[1], lp["cp"])
    zd = fourier_mix(pbcd[2])
    zs, w13 = [za, zb, zc, zd], None
    if cast_ffn_weights:
        sides = [(w, lp["layer"], 0, w.shape[1]) for w in (lp["w1"], lp["w3"])]
        merged, w13 = merge(h, zs, merge_w[0], lp["gate_b"], merge_w[1], sides=sides)
    else:
        merged = merge(h, zs, merge_w[0], lp["gate_b"], merge_w[1])
    mix = matmul(merged, lp["w_o"], D_MODEL, 0, 512, BF16, "out_proj", layer=lp["layer"])
    return mix, s_fin, w13


def _ffn(h2, lp, w13, w2_bf16=None):
    if w2_bf16 is None:
        u, (w2_bf16,) = ffn_up(h2, w13[0], w13[1], sides=[(lp["w2"], lp["layer"], 0, lp["w2"].shape[1])])
    else:
        u = ffn_up(h2, w13[0], w13[1])
    return matmul(u, w2_bf16, D_MODEL, 0, 512, BF16, "ffn_down"), w2_bf16


def kernel(x, c, ctx, c_ctx, mod_w, mod_b, norm_g, w_in, a_mu, a_w0, a_w_up, a_a0, a_a_up, a_g_up, a_k_k, a_k_a,
           a_r_k, a_ln, a_w_out, b_ln, b_ws, b_bs, b_w_out, c_dw, c_dw_b, c_ln, c_w_out, d_w_out, gate_w, gate_b,
           w_o, ffn_w1, ffn_w3, ffn_w2):
    depth = mod_w.shape[0]
    d = D_MODEL
    c_cols = jnp.concatenate([c.reshape(d, 1), c_ctx.reshape(d, 1), jnp.zeros((d, 6), F32)], axis=1)
    mods = modulation(c_cols, mod_w, mod_b.reshape(depth, 1, 6 * d))
    x_lat, x_ctx = x[0], ctx[0]
    zero_state = jnp.zeros((2, PAIRS, LANES, LANES), F32)
    w_in_t = jnp.swapaxes(w_in, 1, 2)
    w_in_packed = (round_rows(w_in_t, 0, A_PAD, 256), round_rows(w_in_t, A_PROJ, w_in.shape[2] - A_PROJ, 160))
    weights = (w_in_packed, a_mu, a_w0, a_w_up, a_a0, a_a_up, a_g_up, a_k_k, a_k_a, a_r_k, a_ln, a_w_out, b_ln, b_ws, b_bs,
               b_w_out, c_dw, c_dw_b, c_ln, c_w_out, d_w_out, gate_w, gate_b, w_o, ffn_w1, ffn_w3, ffn_w2)
    h_lat = h_ctx = None
    for l in range(depth):
        last = l == depth - 1
        lp = _layer_params(l, *weights)
        ng = [norm_g[l, i].reshape(1, d) for i in range(4)]
        ml = [mods[l, 0:1, i * d:(i + 1) * d] for i in range(6)]
        mc = [mods[l, 1:2, i * d:(i + 1) * d] for i in range(6)]
        if l == 0:
            h_lat = normmod(x_lat, ng[0], ml[0], ml[1])
            h_ctx = normmod(x_ctx, ng[0], mc[0], mc[1])

        pbcd_lat, merge_w = _in_proj_bcd(h_lat, lp, cast_merge_weights=True)

        if last:
            ctx_states = _rwkv_scan(_in_proj_a(h_ctx, lp), zero_state, lp["ap"], x_ctx.shape[0])[4]
        else:
            pbcd_ctx, _ = _in_proj_bcd(h_ctx, lp)
            mix_ctx, ctx_states, _ = _token_mix(h_ctx, zero_state, x_ctx.shape[0], lp, pbcd_ctx, merge_w)

        mix_lat, _, w13 = _token_mix(h_lat, ctx_states, GRID_W, lp, pbcd_lat, merge_w, cast_ffn_weights=True)
        x_lat, h2 = resnorm(x_lat, mix_lat, ml[2], ng[1], (ng[2], ml[3], ml[4]))
        y, w2_bf16 = _ffn(h2, lp, w13)
        if last:
            x_lat = resnorm(x_lat, y, ml[5], ng[3])
        else:
            ngn = norm_g[l + 1, 0].reshape(1, d)
            mln = [mods[l + 1, 0:1, i * d:(i + 1) * d] for i in range(2)]
            mcn = [mods[l + 1, 1:2, i * d:(i + 1) * d] for i in range(2)]
            x_lat, h_lat = resnorm(x_lat, y, ml[5], ng[3], (ngn, mln[0], mln[1]))
            x_ctx, h2c = resnorm(x_ctx, mix_ctx, mc[2], ng[1], (ng[2], mc[3], mc[4]))
            yc, _ = _ffn(h2c, lp, w13, w2_bf16)
            x_ctx, h_ctx = resnorm(x_ctx, yc, mc[5], ng[3], (ngn, mcn[0], mcn[1]))
    return x_lat[None]
```

```python
import functools
import math

import numpy as np
import jax
import jax.numpy as jnp
from jax import lax
from jax.experimental import pallas as pl
from jax.experimental.pallas import tpu as pltpu

F32, BF16 = jnp.float32, jnp.bfloat16

D_MODEL = 4096
DEPTH = 2
GRID_W = 64
BRANCH_W = 1024
HEAD = 64
HEADS = BRANCH_W // HEAD
PAIRS = HEADS // 2
LORA_W, LORA_A, LORA_G = 64, 64, 160
A_PROJ = 3 * BRANCH_W + LORA_W + LORA_A + LORA_G
A_GN_EPS = 64e-5
B_CHUNK = 128
B_GROUPS = 8
C_KERNEL = 31
D_GROUPS = 4
D_GROUP_CH = BRANCH_W // D_GROUPS
FFN_HIDDEN = 11008
RMS_EPS = 1e-6
LN_EPS = 1e-5

LANES = 128
SUBLANES = 8
VMEM_BUDGET = 60 * 1024 * 1024

A_LORA = 3 * BRANCH_W
A_GC = A_LORA + LANES
A_PAD = A_GC + 3 * LANES

MERGE_TM = 512
FFN_UP_TM = 2048
CHUNK = 64
CHAIN_STEP = 4
CONV_HALO = 16


def _cparams(n_axes, vmem_bytes):
    limit = int(min(max(vmem_bytes + (8 << 20), 32 << 20), VMEM_BUDGET))
    return pltpu.CompilerParams(dimension_semantics=("arbitrary",) * n_axes, vmem_limit_bytes=limit)


def _dot(a, b):
    return jnp.dot(a.astype(BF16), b.astype(BF16), preferred_element_type=F32)


def _dot_nt(a, b):
    return lax.dot_general(a.astype(BF16), b.astype(BF16), (((1,), (1,)), ((), ())), preferred_element_type=F32)


def _dot_tn(a, b):
    return lax.dot_general(a.astype(BF16), b.astype(BF16), (((0,), (0,)), ((), ())), preferred_element_type=F32)


def _dot_split_rhs(x, e):
    hi = x.astype(BF16)
    lo = (x - hi.astype(F32)).astype(BF16)
    f = lambda p: jnp.dot(p, e, preferred_element_type=F32)
    return f(hi) + f(lo)


def _dot_split_lhs(e, x):
    hi = x.astype(BF16)
    lo = (x - hi.astype(F32)).astype(BF16)
    f = lambda p: jnp.dot(e, p, preferred_element_type=F32)
    return f(hi) + f(lo)


def _sigmoid(x):
    return 1.0 / (1.0 + jnp.exp(-x))


def _silu(x):
    return x * _sigmoid(x)


def _softplus(x):
    return jnp.maximum(x, 0.0) + jnp.log(1.0 + jnp.exp(-jnp.abs(x)))


def _gelu_tanh(x):
    return 0.5 * x * (1.0 + jnp.tanh(math.sqrt(2.0 / math.pi) * (x + 0.044715 * (x * x * x))))


def _rmsnorm(x, g):
    return (x * lax.rsqrt(jnp.mean(x * x, axis=-1, keepdims=True) + RMS_EPS)) * g


def _layernorm(x, g, b):
    mu = jnp.mean(x, axis=-1, keepdims=True)
    xc = x - mu
    var = jnp.mean(xc * xc, axis=-1, keepdims=True)
    return (xc * lax.rsqrt(var + LN_EPS)) * g + b


MOD_ROWS = 32
MOD_COLS = 512


def _mod_body(c_ref, w_ref, b_ref, o_ref, s0_ref, s1_ref):
    d, tn = w_ref.shape[1], w_ref.shape[2]

    @pl.when((pl.program_id(0) == 0) & (pl.program_id(1) == 0))
    def _():
        s = _silu(c_ref[...])
        s0_ref[...] = jnp.broadcast_to(s[:, 0:1], (d, LANES))
        s1_ref[...] = jnp.broadcast_to(s[:, 1:2], (d, LANES))

    for c0 in range(0, tn, MOD_COLS):
        def step(k, acc):
            r0 = pl.multiple_of(k * MOD_ROWS, MOD_ROWS)
            w = w_ref[0, pl.ds(r0, MOD_ROWS), c0:c0 + MOD_COLS]
            s0 = jnp.concatenate([s0_ref[pl.ds(r0, MOD_ROWS), :]] * (MOD_COLS // LANES), axis=1)
            s1 = jnp.concatenate([s1_ref[pl.ds(r0, MOD_ROWS), :]] * (MOD_COLS // LANES), axis=1)
            return acc[0] + w * s0, acc[1] + w * s1

        z = jnp.zeros((MOD_ROWS, MOD_COLS), F32)
        a0, a1 = lax.fori_loop(0, d // MOD_ROWS, step, (z, z), unroll=2)
        bias = b_ref[0, :, c0:c0 + MOD_COLS]
        o_ref[0, :, c0:c0 + MOD_COLS] = jnp.concatenate(
            [jnp.sum(a0, axis=0, keepdims=True) + bias, jnp.sum(a1, axis=0, keepdims=True) + bias,
             jnp.zeros((6, MOD_COLS), F32)], axis=0)


def modulation(c_cols, mod_w, mod_b):
    depth, d, n = mod_w.shape
    tn = 1024
    return pl.pallas_call(
        _mod_body,
        grid=(depth, n // tn),
        in_specs=[pl.BlockSpec((d, 8), lambda l, j: (0, 0)),
                  pl.BlockSpec((1, d, tn), lambda l, j: (l, 0, j)),
                  pl.BlockSpec((1, 1, tn), lambda l, j: (l, 0, j))],
        out_specs=pl.BlockSpec((1, 8, tn), lambda l, j: (l, 0, j)),
        out_shape=jax.ShapeDtypeStruct((depth, 8, n), F32),
        scratch_shapes=[pltpu.VMEM((d, LANES), F32), pltpu.VMEM((d, LANES), F32)],
        compiler_params=_cparams(2, 2 * d * tn * 4 + 5 * d * LANES * 4),
        name="modulation",
    )(c_cols, mod_w, mod_b)


def _normmod_body(x_ref, g_ref, sh_ref, sc_ref, h_ref):
    h = _rmsnorm(x_ref[...], g_ref[...]) * (1.0 + sc_ref[...]) + sh_ref[...]
    h_ref[...] = h.astype(h_ref.dtype)


def normmod(x, g, shift, scale):
    t, d = x.shape
    tm = 256
    row = pl.BlockSpec((1, d), lambda i: (0, 0))
    return pl.pallas_call(
        _normmod_body,
        grid=(t // tm,),
        in_specs=[pl.BlockSpec((tm, d), lambda i: (i, 0)), row, row, row],
        out_specs=pl.BlockSpec((tm, d), lambda i: (i, 0)),
        out_shape=jax.ShapeDtypeStruct((t, d), BF16),
        compiler_params=_cparams(1, 2 * tm * d * 6),
        name="normmod",
    )(x, g, shift, scale)


def _resnorm_body(x_ref, y_ref, gate_ref, gpost_ref, gpre_ref, sh_ref, sc_ref, xo_ref, h_ref):
    xn = x_ref[...].astype(F32) + gate_ref[...] * _rmsnorm(y_ref[...].astype(F32), gpost_ref[...])
    xo_ref[...] = xn.astype(xo_ref.dtype)
    h = _rmsnorm(xn, gpre_ref[...]) * (1.0 + sc_ref[...]) + sh_ref[...]
    h_ref[...] = h.astype(h_ref.dtype)


def _res_body(x_ref, y_ref, gate_ref, gpost_ref, xo_ref):
    xo_ref[...] = x_ref[...].astype(F32) + gate_ref[...] * _rmsnorm(y_ref[...].astype(F32), gpost_ref[...])


def resnorm(x, y, gate, g_post, nxt=None):
    t, d = x.shape
    tm = 256
    row = pl.BlockSpec((1, d), lambda i: (0, 0))
    tile = pl.BlockSpec((tm, d), lambda i: (i, 0))
    if nxt is None:
        return pl.pallas_call(
            _res_body, grid=(t // tm,), in_specs=[tile, tile, row, row], out_specs=tile,
            out_shape=jax.ShapeDtypeStruct((t, d), F32),
            compiler_params=_cparams(1, 2 * tm * d * 12), name="residual",
        )(x, y, gate, g_post)
    return pl.pallas_call(
        _resnorm_body, grid=(t // tm,), in_specs=[tile, tile, row, row, row, row, row],
        out_specs=[tile, tile],
        out_shape=[jax.ShapeDtypeStruct((t, d), BF16), jax.ShapeDtypeStruct((t, d), BF16)],
        compiler_params=_cparams(1, 2 * tm * d * 14), name="residual_norm",
    )(x, y, gate, g_post, *nxt)


def _mm_body(x_ref, w_ref, *rest, n_side, cast, w_t):
    side_in, o_ref, side_out = rest[:n_side], rest[n_side], rest[n_side + 1:2 * n_side + 1]
    for s_in, s_out in zip(side_in, side_out):
        s_out[...] = s_in[...].astype(BF16)
    if cast:
        wb_ref = rest[-1]

        @pl.when(pl.program_id(1) == 0)
        def _():
            wb_ref[...] = w_ref[...].astype(BF16)

        w = wb_ref[...]
    else:
        w = w_ref[...]
    if w_t:
        acc = lax.dot_general(x_ref[...], w, (((1,), (1,)), ((), ())), preferred_element_type=F32)
    else:
        acc = jnp.dot(x_ref[...], w, preferred_element_type=F32)
    o_ref[...] = acc.astype(o_ref.dtype)


def _wspec(w, layer, rows, tn, col_block, w_t=False):
    if w_t:
        return pl.BlockSpec((None, tn, rows), lambda j, i: (layer, col_block(j), 0))
    if w.ndim == 3:
        return pl.BlockSpec((None, rows, tn), lambda j, i: (layer, 0, col_block(j)))
    return pl.BlockSpec((rows, tn), lambda j, i: (0, col_block(j)))


def _side_specs(sides, nj, ni):
    side_in, side_out, side_shape, side_args, vm = [], [], [], [], 0
    for arr, lyr, row0, nrows in sides:
        rps, cols = nrows // (nj * ni), arr.shape[2]
        assert rps * nj * ni == nrows and rps % 16 == 0 and row0 % rps == 0
        side_in.append(pl.BlockSpec((None, rps, cols),
                                    functools.partial(lambda j, i, lyr, b0: (lyr, b0 + j * ni + i, 0), lyr=lyr, b0=row0 // rps)))
        side_out.append(pl.BlockSpec((rps, cols), lambda j, i: (j * ni + i, 0)))
        side_shape.append(jax.ShapeDtypeStruct((nrows, cols), BF16))
        side_args.append(arr)
        vm += 2 * rps * cols * 6
    return side_in, side_out, side_shape, side_args, vm


def matmul(x, w, n, col0, tn, out_dtype, name, layer=None, sides=(), w_t=False, tm_max=None):
    m, k = x.shape
    tm = min(m, tm_max or (1024 if k <= 4096 else 512))
    j0 = col0 // tn
    assert col0 % tn == 0 and n % tn == 0 and m % tm == 0
    nj, ni = n // tn, m // tm
    cast = w.dtype == F32
    assert not (cast and w_t)
    vm = 2 * (tm * k * 2 + k * tn * w.dtype.itemsize + tm * tn * 4) + (k * tn * 2 if cast else 0)
    side_in, side_out, side_shape, side_args, side_vm = _side_specs(sides, nj, ni)
    vm += side_vm
    out = pl.pallas_call(
        functools.partial(_mm_body, n_side=len(sides), cast=cast, w_t=w_t),
        grid=(nj, ni),
        in_specs=[pl.BlockSpec((tm, k), lambda j, i: (i, 0)),
                  _wspec(w, layer, k, tn, lambda j: j + j0, w_t)] + side_in,
        out_specs=[pl.BlockSpec((tm, tn), lambda j, i: (i, j))] + side_out,
        out_shape=[jax.ShapeDtypeStruct((m, n), out_dtype)] + side_shape,
        scratch_shapes=[pltpu.VMEM((k, tn), BF16)] if cast else [],
        compiler_params=_cparams(2, vm),
        name=name,
    )(x, w, *side_args)
    return (out[0], out[1:]) if sides else out[0]


def _round_rows_body(w_ref, o_ref):
    o_ref[...] = w_ref[...].astype(BF16)


def round_rows(w, row0, nrows, blk):
    depth, _, cols = w.shape
    assert row0 % blk == 0 and nrows % blk == 0 and blk % 16 == 0
    b0 = row0 // blk
    return pl.pallas_call(
        _round_rows_body,
        grid=(depth, nrows // blk),
        in_specs=[pl.BlockSpec((None, blk, cols), lambda l, i: (l, b0 + i, 0))],
        out_specs=pl.BlockSpec((None, blk, cols), lambda l, i: (l, i, 0)),
        out_shape=jax.ShapeDtypeStruct((depth, nrows, cols), BF16),
        compiler_params=_cparams(2, 2 * blk * cols * 6),
        name="round_rows",
    )(w)


def _ffn_up_body(x_ref, w1_ref, w3_ref, *rest, n_side):
    side_in, o_ref, side_out = rest[:n_side], rest[n_side], rest[n_side + 1:]
    for s_in, s_out in zip(side_in, side_out):
        s_out[...] = s_in[...].astype(BF16)
    x = x_ref[...]
    a = jnp.dot(x, w1_ref[...], preferred_element_type=F32)
    b = jnp.dot(x, w3_ref[...], preferred_element_type=F32)
    o_ref[...] = (_silu(a) * b).astype(o_ref.dtype)


def ffn_up(h, w1, w3, sides=()):
    m, k = h.shape
    n = w1.shape[-1]
    tm, tn = min(m, FFN_UP_TM), 256
    nj, ni = n // tn, m // tm
    vm = 2 * (tm * k * 2 + 2 * k * tn * 2 + tm * tn * 2) + 5 * tm * tn * 4
    side_in, side_out, side_shape, side_args, side_vm = _side_specs(sides, nj, ni)
    out = pl.pallas_call(
        functools.partial(_ffn_up_body, n_side=len(sides)),
        grid=(nj, ni),
        in_specs=[pl.BlockSpec((tm, k), lambda j, i: (i, 0)),
                  pl.BlockSpec((k, tn), lambda j, i: (0, j)),
                  pl.BlockSpec((k, tn), lambda j, i: (0, j))] + side_in,
        out_specs=[pl.BlockSpec((tm, tn), lambda j, i: (i, j))] + side_out,
        out_shape=[jax.ShapeDtypeStruct((m, n), BF16)] + side_shape,
        compiler_params=_cparams(2, vm + side_vm),
        name="ffn_up",
    )(h, w1, w3, *side_args)
    return (out[0], out[1:]) if sides else out[0]


def _merge_body(h_ref, *refs, n_side):
    z, g_top, g_bot, bias, w_out = refs[0:4], refs[4:8], refs[8:12], refs[12:16], refs[16:20]
    side_in, o_ref, side_out = refs[20:20 + n_side], refs[20 + n_side], refs[21 + n_side:]
    for s_in, s_out in zip(side_in, side_out):
        s_out[...] = s_in[...].astype(BF16)
    half = g_top[0].shape[0]
    h_top, h_bot = h_ref[:, :half], h_ref[:, half:]
    acc = None
    for br in range(4):
        logits = (jnp.dot(h_top, g_top[br][...], preferred_element_type=F32)
                  + jnp.dot(h_bot, g_bot[br][...], preferred_element_type=F32) + bias[br][...])
        y = jnp.dot(z[br][...], w_out[br][...], preferred_element_type=F32)
        acc = _sigmoid(logits) * y if acc is None else acc + _sigmoid(logits) * y
    o_ref[...] = acc.astype(o_ref.dtype)


def merge(h, zs, gate_halves, gate_b, w_outs, sides=()):
    m, d = h.shape
    bw = zs[0].shape[1]
    half = gate_halves[0].shape[0]
    tm, tn = min(m, MERGE_TM), 256
    nj = d // tn
    hspec = pl.BlockSpec((tm, d), lambda j, i: (i, 0))
    zspec = pl.BlockSpec((tm, bw), lambda j, i: (i, 0))
    gspecs = [pl.BlockSpec((half, tn), functools.partial(lambda j, i, br: (0, br * nj + j), br=br)) for br in range(4)]
    bspecs = [pl.BlockSpec((1, tn), functools.partial(lambda j, i, br: (0, br * nj + j), br=br)) for br in range(4)]
    wspec = pl.BlockSpec((bw, tn), lambda j, i: (0, j))
    vm = 2 * (tm * d * 2 + 4 * tm * bw * 2 + tm * tn * 2 + 4 * (d + bw) * tn * 2) + 3 * tm * tn * 4
    side_in, side_out, side_shape, side_args, side_vm = _side_specs(sides, nj, m // tm)
    out = pl.pallas_call(
        functools.partial(_merge_body, n_side=len(sides)),
        grid=(nj, m // tm),
        in_specs=[hspec] + [zspec] * 4 + gspecs + gspecs + bspecs + [wspec] * 4 + side_in,
        out_specs=[pl.BlockSpec((tm, tn), lambda j, i: (i, j))] + side_out,
        out_shape=[jax.ShapeDtypeStruct((m, d), BF16)] + side_shape,
        compiler_params=_cparams(2, vm + side_vm),
        name="merge",
    )(h, *zs, *([gate_halves[0]] * 4), *([gate_halves[1]] * 4), gate_b, gate_b, gate_b, gate_b, *w_outs, *side_args)
    return (out[0], out[1:]) if sides else out[0]


def _token_shift(x, mu, row_len):
    tm = x.shape[0]
    pos = lax.broadcasted_iota(jnp.int32, x.shape, 0) & (row_len - 1)
    prev = jnp.where(pos == 0, 0.0, pltpu.roll(x, 1, 0))
    nxt = jnp.where(pos == row_len - 1, 0.0, pltpu.roll(x, tm - 1, 0))
    return x + mu * (0.5 * (prev + nxt) - x)


def _aprep_body(r_ref, k_ref, v_ref, wa_ref, g1_ref, g2_ref,
                mur_ref, muk_ref, muv_ref, muwa_ref, mug1_ref, mug2_ref,
                w0_ref, wup_ref, a0_ref, aup_ref, gup_ref, kk_ref, ka_ref, rk_ref, e_ref,
                ro_ref, vo_ref, kko_ref, lw_ref, kd_ref, b_ref, bonus_ref, g_ref, *, row_len):
    e = e_ref[...]
    xr = _token_shift(r_ref[...], mur_ref[...], row_len)
    xk = _token_shift(k_ref[...], muk_ref[...], row_len)
    xv = _token_shift(v_ref[...], muv_ref[...], row_len)
    xwa = _token_shift(wa_ref[...], muwa_ref[...], row_len)
    xg1 = _token_shift(g1_ref[...], mug1_ref[...], row_len)
    xg2 = _token_shift(g2_ref[...], mug2_ref[...], row_len)
    kk = xk * kk_ref[...]
    kk = kk * lax.rsqrt(jnp.maximum(_dot_split_rhs(kk * kk, e), 1e-12))
    tw = jnp.tanh(xwa)
    bonus = None
    for d in range(2):
        w = -_softplus(-(w0_ref[d] + _dot(tw, wup_ref[d]))) - 0.5
        a = _sigmoid(a0_ref[d] + _dot(xwa, aup_ref[d]))
        kd = xk * (1.0 + (a - 1.0) * ka_ref[...])
        lw_ref[d] = -jnp.exp(w)
        kd_ref[d] = kd
        b_ref[d] = kk * a
        bn = _dot(xr * kd * rk_ref[...], e) * xv
        bonus = bn if bonus is None else bonus + bn
    ro_ref[...] = xr
    vo_ref[...] = xv
    kko_ref[...] = kk
    bonus_ref[...] = bonus
    g_ref[...] = _dot(_sigmoid(xg1), gup_ref[0]) + _dot(_sigmoid(xg2), gup_ref[1])


N_PREP_IN = 21


def _rwkv_maps_body(*refs, row_len, nchunk):
    prep_in, (tri_ref, msk_ref) = refs[:N_PREP_IN], refs[N_PREP_IN:N_PREP_IN + 2]
    rhat_ref, y0_ref, m_ref, n_ref, bonus_ref, g_ref = refs[N_PREP_IN + 2:N_PREP_IN + 8]
    r_s, v_s, kk_s, lw_s, kd_s, b_s = refs[N_PREP_IN + 8:]
    _aprep_body(*prep_in, r_s, v_s, kk_s, lw_s, kd_s, b_s, bonus_ref, g_ref, row_len=row_len)
    _scan_pre_body(r_s, v_s, kk_s, lw_s, kd_s, b_s, tri_ref, msk_ref, rhat_ref, y0_ref, m_ref, n_ref, nchunk=nchunk)


def rwkv_maps(pa, ap, row_len, tri, msk):
    t = pa.shape[0]
    nchunk = 4
    tm = nchunk * CHUNK
    nc = t // CHUNK
    nb = BRANCH_W // LANES
    col = lambda c: pl.BlockSpec((tm, LANES), functools.partial(lambda i, p, c: (i, c + p), c=c))
    fix = lambda c: pl.BlockSpec((tm, LANES), functools.partial(lambda i, p, c: (i, c), c=c))
    mcol = lambda c: pl.BlockSpec((1, LANES), functools.partial(lambda i, p, c: (0, c + p), c=c))
    mfix = lambda c: pl.BlockSpec((1, LANES), functools.partial(lambda i, p, c: (0, c), c=c))
    prow = pl.BlockSpec((1, LANES), lambda i, p: (0, p))
    p2 = pl.BlockSpec((2, 1, LANES), lambda i, p: (0, 0, p))
    up2 = pl.BlockSpec((2, LANES, LANES), lambda i, p: (0, 0, p))
    out1 = pl.BlockSpec((tm, LANES), lambda i, p: (i, p))
    out2 = pl.BlockSpec((2, tm, LANES), lambda i, p: (0, i, p))
    mspec = pl.BlockSpec((2, nchunk, 1, HEAD, LANES), lambda i, p: (0, i, p, 0, 0))
    s1 = jax.ShapeDtypeStruct((t, BRANCH_W), F32)
    s2 = jax.ShapeDtypeStruct((2, t, BRANCH_W), BF16)
    sm = jax.ShapeDtypeStruct((2, nc, PAIRS, HEAD, LANES), F32)
    lora, gc = A_LORA // LANES, A_GC // LANES
    tile = lambda lead=(): pltpu.VMEM(lead + (tm, LANES), F32)
    return pl.pallas_call(
        functools.partial(_rwkv_maps_body, row_len=row_len, nchunk=nchunk),
        grid=(t // tm, nb),
        in_specs=[col(0), col(nb), col(2 * nb), fix(lora), fix(gc), fix(gc + 1),
                  mcol(0), mcol(nb), mcol(2 * nb), mfix(lora), mfix(gc), mfix(gc + 1),
                  p2, up2, p2, up2, up2, prow, prow, prow,
                  pl.BlockSpec((LANES, LANES), lambda i, p: (0, 0)),
                  pl.BlockSpec((2, CHUNK, CHUNK), lambda i, p: (0, 0, 0)),
                  pl.BlockSpec(msk.shape, lambda i, p: (0, 0, 0, 0))],
        out_specs=[out2, out2, mspec, mspec, out1, out1],
        out_shape=[s2, s2, sm, sm, s1, s1],
        scratch_shapes=[tile(), tile(), tile(), tile((2,)), tile((2,)), tile((2,))],
        compiler_params=_cparams(2, 24 << 20),
        name="rwkv_chunk_maps",
    )(pa, pa, pa, pa, pa, pa, ap["mu"], ap["mu"], ap["mu"], ap["mu"], ap["mu"], ap["mu"],
      ap["w0"], ap["w_up"], ap["a0"], ap["a_up"], ap["g_up"], ap["k_k"], ap["k_a"], ap["r_k"], ap["e2"], tri, msk)


def _pair_rows(x):
    lane = lax.broadcasted_iota(jnp.int32, x.shape, 1)
    return jnp.concatenate([jnp.where(lane < HEAD, x, 0.0), jnp.where(lane >= HEAD, x, 0.0)], axis=0)


MSK_STRICT, MSK_INCL, MSK_EYE, MSK_DIAG8, MSK_OFF8, MSK_OFF16, MSK_OFF32 = range(7)


def _chunk_mask_tables():
    n = 2 * CHUNK
    r = np.arange(n)[:, None]
    c = np.arange(n)[None, :]
    same = (r // CHUNK) == (c // CHUNK)
    out = np.zeros((2, 7, n, n), np.float32)
    for d in range(2):
        before = (c > r) if d else (c < r)
        out[d, MSK_STRICT] = same & before
        out[d, MSK_INCL] = same & (before | (r == c))
        out[d, MSK_EYE] = r == c
        out[d, MSK_DIAG8] = ((r // 8) == (c // 8)) & before
        for idx, s in ((MSK_OFF8, 8), (MSK_OFF16, 16), (MSK_OFF32, 32)):
            blk = (r // (2 * s)) == (c // (2 * s))
            rh, ch = (r // s) % 2, (c // s) % 2
            out[d, idx] = blk & ((rh == 0) & (ch == 1) if d else (rh == 1) & (ch == 0))
    return jnp.asarray(out)


def _scan_pre_body(r_ref, v_ref, kk_ref, lw_ref, kd_ref, b_ref, tri_ref, msk_ref, rhat_ref, y0_ref, m_ref, n_ref, *, nchunk):
    c, n = CHUNK, 2 * CHUNK
    chains = [(d, ci) for d in range(2) for ci in range(nchunk)]
    each = lambda f, *cols: [f(*xs) for xs in zip(*cols)]
    msk = lambda d, k: msk_ref[d, k]
    rows = lambda ci: slice(ci * c, (ci + 1) * c)

    lw = [lw_ref[d, rows(ci), :] for d, ci in chains]
    cum = [_dot_split_lhs(tri_ref[d], x) for (d, _), x in zip(chains, lw)]
    tot = [x[0:1] if d else x[c - 1:c] for (d, _), x in zip(chains, cum)]
    g_inv = each(lambda x: jnp.exp(-x), cum)
    g_tail = each(lambda t, x: jnp.exp(t - x), tot, cum)
    atp = [_pair_rows(-kk_ref[rows(ci), :] * jnp.exp(x - l)) for (_, ci), x, l in zip(chains, cum, lw)]
    rtp = [_pair_rows(r_ref[rows(ci), :] * jnp.exp(x)) for (_, ci), x in zip(chains, cum)]
    btp = [_pair_rows(b_ref[d, rows(ci), :] * g) for (d, ci), g in zip(chains, g_inv)]
    ktp = [_pair_rows(kd_ref[d, rows(ci), :] * g) for (d, ci), g in zip(chains, g_inv)]
    bhp = [_pair_rows(b_ref[d, rows(ci), :] * g) for (d, ci), g in zip(chains, g_tail)]
    khp = [_pair_rows(kd_ref[d, rows(ci), :] * g) for (d, ci), g in zip(chains, g_tail)]
    vp = [_pair_rows(v_ref[rows(ci), :]) for _, ci in chains]

    big = each(lambda a, r, b, k: _dot_nt(jnp.concatenate([a, r], axis=0), jnp.concatenate([b, k], axis=0)),
               atp, rtp, btp, ktp)
    a_ab = [jnp.where(msk(d, MSK_STRICT) > 0.0, x[:n, :n], 0.0) for (d, _), x in zip(chains, big)]
    a_ak = [jnp.where(msk(d, MSK_STRICT) > 0.0, x[:n, n:], 0.0) for (d, _), x in zip(chains, big)]
    a_rb = [jnp.where(msk(d, MSK_INCL) > 0.0, x[n:, :n], 0.0) for (d, _), x in zip(chains, big)]
    a_rk = [jnp.where(msk(d, MSK_INCL) > 0.0, x[n:, n:], 0.0) for (d, _), x in zip(chains, big)]

    n8 = [x * msk(d, MSK_DIAG8) for (d, _), x in zip(chains, a_ab)]
    t = [msk(d, MSK_EYE) + x for (d, _), x in zip(chains, n8)]
    n2 = each(_dot, n8, n8)
    t = each(lambda x, p: x + _dot(p, x), t, n2)
    n4 = each(_dot, n2, n2)
    t = each(lambda x, p: x + _dot(p, x), t, n4)
    for idx in (MSK_OFF8, MSK_OFF16, MSK_OFF32):
        off = [x * msk(d, idx) for (d, _), x in zip(chains, a_ab)]
        tn = each(_dot, t, off)
        t = each(lambda x, p: x + _dot(p, x), t, tn)

    akv = each(_dot, a_ak, vp)
    w_u = each(lambda ti, a, x: _dot(ti, jnp.concatenate([a, x], axis=1)), t, atp, akv)
    ry = each(_dot, a_rb, w_u)
    rkv = each(_dot, a_rk, vp)
    mn = each(_dot_tn, bhp, w_u)
    kv = each(_dot_tn, khp, vp)
    for i, (d, ci) in enumerate(chains):
        rhat_p = rtp[i] + ry[i][:, :n]
        y0_p = ry[i][:, n:] + rkv[i]
        rhat_ref[d, rows(ci), :] = (rhat_p[:c] + rhat_p[c:]).astype(rhat_ref.dtype)
        y0_ref[d, rows(ci), :] = (y0_p[:c] + y0_p[c:]).astype(y0_ref.dtype)
        m_i = mn[i][:, :n] + msk(d, MSK_EYE) * jnp.exp(tot[i])
        n_i = mn[i][:, n:] + kv[i]
        m_ref[d, ci, 0] = m_i[:c] + m_i[c:]
        n_ref[d, ci, 0] = n_i[:c] + n_i[c:]


def _scan_seq_body(s0_ref, mf_ref, mb_ref, nf_ref, nb_ref, rf_ref, rb_ref, yf0_ref, yb0_ref,
                   yf_ref, yb_ref, sfin_ref, s_ref):
    c = pl.program_id(0)

    @pl.when(c == 0)
    def _():
        s_ref[...] = s0_ref[...]

    for sub in range(CHAIN_STEP):
        for d, (m_ref, n_ref, rh_ref, y0_ref, y_ref) in enumerate(((mf_ref, nf_ref, rf_ref, yf0_ref, yf_ref),
                                                                     (mb_ref, nb_ref, rb_ref, yb0_ref, yb_ref))):
            lc = sub if d == 0 else CHAIN_STEP - 1 - sub
            rows = slice(lc * CHUNK, (lc + 1) * CHUNK)
            for p in range(PAIRS):
                ln = slice(p * LANES, (p + 1) * LANES)
                s = s_ref[d, p]
                y_ref[rows, ln] = _dot(rh_ref[0, rows, ln], s) + y0_ref[0, rows, ln].astype(F32)
                s_ref[d, p] = _dot(_pair_rows(m_ref[0, lc, p]), s) + _pair_rows(n_ref[0, lc, p])

    @pl.when(c == pl.num_programs(0) - 1)
    def _():
        sfin_ref[...] = s_ref[...]


def scan_seq(s0, rhat, y0, m, n):
    t = rhat.shape[1]
    nc = t // (CHUNK * CHAIN_STEP)
    fwd5 = lambda c: (0, c, 0, 0, 0)
    bwd5 = lambda c: (1, nc - 1 - c, 0, 0, 0)
    mblk = (1, CHAIN_STEP, PAIRS, HEAD, LANES)
    rblk = (1, CHAIN_STEP * CHUNK, BRANCH_W)
    sblk = pl.BlockSpec((2, PAIRS, LANES, LANES), lambda c: (0, 0, 0, 0))
    ys = jax.ShapeDtypeStruct((t, BRANCH_W), F32)
    return pl.pallas_call(
        _scan_seq_body,
        grid=(nc,),
        in_specs=[sblk,
                  pl.BlockSpec(mblk, fwd5), pl.BlockSpec(mblk, bwd5),
                  pl.BlockSpec(mblk, fwd5), pl.BlockSpec(mblk, bwd5),
                  pl.BlockSpec(rblk, lambda c: (0, c, 0)), pl.BlockSpec(rblk, lambda c: (1, nc - 1 - c, 0)),
                  pl.BlockSpec(rblk, lambda c: (0, c, 0)), pl.BlockSpec(rblk, lambda c: (1, nc - 1 - c, 0))],
        out_specs=[pl.BlockSpec((CHAIN_STEP * CHUNK, BRANCH_W), lambda c: (c, 0)),
                   pl.BlockSpec((CHAIN_STEP * CHUNK, BRANCH_W), lambda c: (nc - 1 - c, 0)),
                   sblk],
        out_shape=[ys, ys, jax.ShapeDtypeStruct((2, PAIRS, LANES, LANES), F32)],
        scratch_shapes=[pltpu.VMEM((2, PAIRS, LANES, LANES), F32)],
        compiler_params=_cparams(1, 16 << 20),
        name="rwkv_chain",
    )(s0, m, m, n, n, rhat, rhat, y0, y0)


def _areadout_body(yf_ref, yb_ref, bonus_ref, g_ref, lng_ref, lnb_ref, e_ref, o_ref):
    e = e_ref[...]
    for p in range(BRANCH_W // LANES):
        ln = slice(p * LANES, (p + 1) * LANES)
        y = yf_ref[:, ln] + yb_ref[:, ln]
        mu = _dot_split_rhs(y, e) * (1.0 / HEAD)
        yc = y - mu
        var = _dot_split_rhs(yc * yc, e) * (1.0 / HEAD)
        yn = (yc * lax.rsqrt(var + A_GN_EPS)) * lng_ref[:, ln] + lnb_ref[:, ln]
        o_ref[:, ln] = ((yn + bonus_ref[:, ln]) * g_ref[:, ln]).astype(o_ref.dtype)


def a_readout(yf, yb, bonus, g, ap):
    t = yf.shape[0]
    tm = 256
    blk = pl.BlockSpec((tm, BRANCH_W), lambda i: (i, 0))
    prow = pl.BlockSpec((1, BRANCH_W), lambda i: (0, 0))
    return pl.pallas_call(
        _areadout_body,
        grid=(t // tm,),
        in_specs=[blk, blk, blk, blk, prow, prow, pl.BlockSpec((LANES, LANES), lambda i: (0, 0))],
        out_specs=blk,
        out_shape=jax.ShapeDtypeStruct((t, BRANCH_W), BF16),
        compiler_params=_cparams(1, 2 * 5 * tm * BRANCH_W * 4),
        name="rwkv_readout",
    )(yf, yb, bonus, g, ap["ln_g"], ap["ln_b"], ap["e2"])


def _bmix_body(u_ref, v_ref, lng_ref, lnb_ref, ws_ref, bs_ref, o_ref):
    u = _gelu_tanh(u_ref[...].astype(F32))
    v = _layernorm(_gelu_tanh(v_ref[...].astype(F32)), lng_ref[...], lnb_ref[...]).astype(BF16)
    for ci in range(u.shape[0] // B_CHUNK):
        rows = slice(ci * B_CHUNK, (ci + 1) * B_CHUNK)
        for g in range(B_GROUPS):
            ln = slice(g * LANES, (g + 1) * LANES)
            s = jnp.dot(ws_ref[g], v[rows, ln], preferred_element_type=F32) + bs_ref[:, ln]
            o_ref[rows, ln] = (u[rows, ln] * s).astype(o_ref.dtype)


def b_mix(pb, bp):
    t = pb.shape[0]
    tm = 256
    row = pl.BlockSpec((1, BRANCH_W), lambda i: (0, 0))
    return pl.pallas_call(
        _bmix_body,
        grid=(t // tm,),
        in_specs=[pl.BlockSpec((tm, BRANCH_W), lambda i: (i, 0)), pl.BlockSpec((tm, BRANCH_W), lambda i: (i, 1)),
                  row, row, pl.BlockSpec((B_GROUPS, B_CHUNK, B_CHUNK), lambda i: (0, 0, 0)),
                  pl.BlockSpec((B_CHUNK, BRANCH_W), lambda i: (0, 0))],
        out_specs=pl.BlockSpec((tm, BRANCH_W), lambda i: (i, 0)),
        out_shape=jax.ShapeDtypeStruct((t, BRANCH_W), BF16),
        compiler_params=_cparams(1, 12 * tm * BRANCH_W * 4),
        name="gmlp_mix",
    )(pb, pb, bp["ln_g"], bp["ln_b"], bp["ws"], bp["bs"])


def _conv_body(ac_ref, gc_ref, ap_ref, gp_ref, an_ref, gn_ref, dw_ref, dwb_ref, lng_ref, lnb_ref,
               o_ref, zs_ref, cs_ref, *, tm):
    i = pl.program_id(0)
    last = pl.num_programs(0) - 1
    glu = lambda a_ref, g_ref: a_ref[...].astype(F32) * _sigmoid(g_ref[...].astype(F32))
    zs_ref[0:CONV_HALO, :] = jnp.where(i == 0, 0.0, glu(ap_ref, gp_ref))
    zs_ref[CONV_HALO:CONV_HALO + tm, :] = glu(ac_ref, gc_ref)
    zs_ref[CONV_HALO + tm:, :] = jnp.where(i == last, 0.0, glu(an_ref, gn_ref))
    rb, lb = 32, 256
    win = rb + 2 * CONV_HALO
    first = CONV_HALO - C_KERNEL // 2
    for r0 in range(0, tm, rb):
        for l0 in range(0, BRANCH_W, lb):
            w = zs_ref[r0:r0 + win, l0:l0 + lb]
            acc = jnp.zeros((rb, lb), F32) + dwb_ref[:, l0:l0 + lb]
            for s in range(SUBLANES):
                ws = w if s == 0 else pltpu.roll(w, win - s, 0)
                for j in range(C_KERNEL):
                    if (first + j) % SUBLANES == s:
                        a = (first + j) - s
                        acc = acc + ws[a:a + rb] * dw_ref[j:j + 1, l0:l0 + lb]
            cs_ref[r0:r0 + rb, l0:l0 + lb] = acc
    o_ref[...] = _silu(_layernorm(cs_ref[...], lng_ref[...], lnb_ref[...])).astype(o_ref.dtype)


def conv_mix(pc, cp):
    t = pc.shape[0]
    tm = 256
    hb = tm // CONV_HALO
    nh = t // CONV_HALO
    cur = lambda c: pl.BlockSpec((tm, BRANCH_W), functools.partial(lambda i, c: (i, c), c=c))
    prev = lambda c: pl.BlockSpec((CONV_HALO, BRANCH_W), functools.partial(lambda i, c: (jnp.maximum(i * hb - 1, 0), c), c=c))
    nxt = lambda c: pl.BlockSpec((CONV_HALO, BRANCH_W), functools.partial(lambda i, c: (jnp.minimum((i + 1) * hb, nh - 1), c), c=c))
    row = pl.BlockSpec((1, BRANCH_W), lambda i: (0, 0))
    return pl.pallas_call(
        functools.partial(_conv_body, tm=tm),
        grid=(t // tm,),
        in_specs=[cur(0), cur(1), prev(0), prev(1), nxt(0), nxt(1),
                  pl.BlockSpec((C_KERNEL + 1, BRANCH_W), lambda i: (0, 0)), row, row, row],
        out_specs=pl.BlockSpec((tm, BRANCH_W), lambda i: (i, 0)),
        out_shape=jax.ShapeDtypeStruct((t, BRANCH_W), BF16),
        scratch_shapes=[pltpu.VMEM((tm + 2 * CONV_HALO, BRANCH_W), F32), pltpu.VMEM((tm, BRANCH_W), F32)],
        compiler_params=_cparams(1, 16 * tm * BRANCH_W * 4),
        name="conv_mix",
    )(pc, pc, pc, pc, pc, pc, cp["dw"], cp["dw_b"], cp["ln_g"], cp["ln_b"])


FFT_N1, FFT_N2 = 64, 128


def _dft_tables(t):
    two_pi = 2.0 * np.pi
    cidx = np.arange(D_GROUP_CH)
    ph = two_pi * np.outer(cidx, cidx) / D_GROUP_CH
    chan = np.concatenate([np.cos(ph), np.sin(ph)], axis=0)
    if t <= 256:
        n = np.arange(t)
        th = two_pi * np.outer(n, n) / t
        m2 = np.concatenate([np.cos(th), -np.sin(th)], axis=0)
        return None, m2.astype(np.float32), chan.astype(np.float32)
    n1, n2 = FFT_N1, FFT_N2
    assert t == n1 * n2
    k1 = np.arange(n1)
    tok = (n2 * np.arange(n1))[None, None, :] + np.arange(n2)[:, None, None]
    th = two_pi * (k1[None, :, None] * tok) / t
    g1 = np.concatenate([np.cos(th), -np.sin(th)], axis=1)
    q = np.arange(n2)
    th2 = two_pi * np.outer(q, q) / n2
    c2, s2 = np.cos(th2), np.sin(th2)
    m2 = np.block([[c2, s2], [-s2, c2]])
    return g1.astype(np.float32), m2.astype(np.float32), chan.astype(np.float32)


FFT_STEP = 4


def _fft1_body(x_ref, g_ref, o_ref):
    for q in range(FFT_STEP):
        ln = slice(q * BRANCH_W, (q + 1) * BRANCH_W)
        o_ref[:, ln] = _dot(g_ref[q], x_ref[:, ln])


def _fft2_body(z_ref, m2_ref, ch_ref, o_ref, *, n_out, scale, stacked):
    z = jnp.concatenate([z_ref[0], z_ref[1]], axis=0) if stacked else z_ref[...]
    x = _dot(m2_ref[...], z)
    xr, xi = x[:n_out], x[n_out:]
    for g in range(D_GROUPS):
        ln = slice(g * D_GROUP_CH, (g + 1) * D_GROUP_CH)
        f = _dot(xr[:, ln], ch_ref[0:D_GROUP_CH]) + _dot(xi[:, ln], ch_ref[D_GROUP_CH:])
        o_ref[:, ln] = (f * scale).astype(o_ref.dtype)


def fourier_mix(pd):
    t = pd.shape[0]
    g1, m2, chan = (None if a is None else jnp.asarray(a) for a in _dft_tables(t))
    scale = 1.0 / math.sqrt(t * D_GROUP_CH)
    chspec = pl.BlockSpec((2 * D_GROUP_CH, D_GROUP_CH), lambda i: (0, 0))
    if g1 is None:
        return pl.pallas_call(
            functools.partial(_fft2_body, n_out=t, scale=scale, stacked=False),
            grid=(1,),
            in_specs=[pl.BlockSpec((t, BRANCH_W), lambda i: (0, 0)), pl.BlockSpec((2 * t, t), lambda i: (0, 0)), chspec],
            out_specs=pl.BlockSpec((t, BRANCH_W), lambda i: (0, 0)),
            out_shape=jax.ShapeDtypeStruct((t, BRANCH_W), BF16),
            compiler_params=_cparams(1, 16 << 20),
            name="fourier_small",
        )(pd, m2, chan)
    n1, n2 = FFT_N1, FFT_N2
    z = pl.pallas_call(
        _fft1_body,
        grid=(n2 // FFT_STEP,),
        in_specs=[pl.BlockSpec((n1, FFT_STEP * BRANCH_W), lambda q: (0, q)),
                  pl.BlockSpec((FFT_STEP, 2 * n1, n1), lambda q: (q, 0, 0))],
        out_specs=pl.BlockSpec((2 * n1, FFT_STEP * BRANCH_W), lambda q: (0, q)),
        out_shape=jax.ShapeDtypeStruct((2 * n1, n2 * BRANCH_W), F32),
        compiler_params=_cparams(1, 8 << 20),
        name="fourier_stage1",
    )(pd.reshape(n1, n2 * BRANCH_W), g1)
    f = pl.pallas_call(
        functools.partial(_fft2_body, n_out=n2, scale=scale, stacked=True),
        grid=(n1,),
        in_specs=[pl.BlockSpec((2, None, n2, BRANCH_W), lambda k: (0, k, 0, 0)),
                  pl.BlockSpec((2 * n2, 2 * n2), lambda k: (0, 0)), chspec],
        out_specs=pl.BlockSpec((n2, BRANCH_W), lambda k: (0, k)),
        out_shape=jax.ShapeDtypeStruct((n2, n1 * BRANCH_W), BF16),
        compiler_params=_cparams(1, 16 << 20),
        name="fourier_stage2",
    )(z.reshape(2, n1, n2, BRANCH_W), m2, chan)
    return f.reshape(t, BRANCH_W)


def _align_a(x):
    return jnp.pad(x, [(0, 0)] * (x.ndim - 1) + [(0, A_PAD - x.shape[-1])])


def _pad_rows(x, before, rows):
    return jnp.pad(x, [(0, 0)] * (x.ndim - 2) + [(before, rows - before - x.shape[-2]), (0, 0)])


def _layer_params(l, w_in, a_mu, a_w0, a_w_up, a_a0, a_a_up, a_g_up, a_k_k, a_k_a, a_r_k, a_ln, a_w_out,
                  b_ln, b_ws, b_bs, b_w_out, c_dw, c_dw_b, c_ln, c_w_out, d_w_out, gate_w, gate_b, w_o,
                  ffn_w1, ffn_w3, ffn_w2):
    row = lambda v: v.reshape(1, -1)
    hid = np.arange(LANES) // HEAD
    e2 = jnp.asarray((hid[:, None] == hid[None, :]).astype(np.float32)).astype(BF16)
    g_up = jnp.stack([a_g_up[l][:LANES], _pad_rows(a_g_up[l][LANES:], 0, LANES)])
    ap = dict(mu=_align_a(row(a_mu[l])), w0=a_w0[l][:, None, :], w_up=_pad_rows(a_w_up[l], 0, LANES),
              a0=a_a0[l][:, None, :], a_up=_pad_rows(a_a_up[l], LORA_W, LANES), g_up=g_up,
              k_k=row(a_k_k[l]), k_a=row(a_k_a[l]), r_k=row(a_r_k[l]), ln_g=row(a_ln[l][0]), ln_b=row(a_ln[l][1]), e2=e2)
    bs_exp = jnp.repeat(jnp.swapaxes(b_bs[l], 0, 1), LANES, axis=1)
    return dict(
        layer=l, w_a=w_in[0], w_bcd=w_in[1], ap=ap,
        bp=dict(ln_g=row(b_ln[l][0]), ln_b=row(b_ln[l][1]), ws=b_ws[l].astype(BF16), bs=bs_exp),
        cp=dict(dw=_pad_rows(c_dw[l], 0, C_KERNEL + 1), dw_b=row(c_dw_b[l]), ln_g=row(c_ln[l][0]), ln_b=row(c_ln[l][1])),
        w_outs=[a_w_out, b_w_out, c_w_out, d_w_out],
        gate_w=gate_w, gate_b=row(gate_b[l]), w_o=w_o,
        w1=ffn_w1, w3=ffn_w3, w2=ffn_w2)


def _tri_tables():
    i = np.arange(CHUNK)
    lower = (i[None, :] <= i[:, None]).astype(np.float32)
    return jnp.asarray(np.stack([lower, lower.T])).astype(BF16)


def _rwkv_scan(pa, s0, ap, row_len):
    rhat, y0, m, n, bonus, g = rwkv_maps(pa, ap, row_len, _tri_tables(), _chunk_mask_tables())
    yf, yb, s_fin = scan_seq(s0, rhat, y0, m, n)
    return yf, yb, bonus, g, s_fin


def _in_proj_a(h, lp):
    return matmul(h, lp["w_a"], A_PAD, 0, 512, F32, "in_proj_a", layer=lp["layer"], w_t=True, tm_max=2048)


def _in_proj_bcd(h, lp, cast_merge_weights=False):
    l = lp["layer"]
    proj = functools.partial(matmul, h, lp["w_bcd"], tn=512, out_dtype=BF16, layer=l, w_t=True)
    if not cast_merge_weights:
        return (proj(n=2 * BRANCH_W, col0=0, name="in_proj_b"),
                proj(n=2 * BRANCH_W, col0=2 * BRANCH_W, name="in_proj_c"),
                proj(n=BRANCH_W, col0=4 * BRANCH_W, name="in_proj_d")), None
    half = D_MODEL // 2
    pb, (g_top,) = proj(n=2 * BRANCH_W, col0=0, name="in_proj_b", sides=[(lp["gate_w"], l, 0, half)])
    pc, (g_bot,) = proj(n=2 * BRANCH_W, col0=2 * BRANCH_W, name="in_proj_c", sides=[(lp["gate_w"], l, half, half)])
    pd, w_outs = proj(n=BRANCH_W, col0=4 * BRANCH_W, name="in_proj_d", sides=[(wo, l, 0, BRANCH_W) for wo in lp["w_outs"]])
    return (pb, pc, pd), ((g_top, g_bot), list(w_outs))


def _token_mix(h, s0, row_len, lp, pbcd, merge_w, cast_ffn_weights=False):
    pa = _in_proj_a(h, lp)
    yf, yb, bonus, g, s_fin = _rwkv_scan(pa, s0, lp["ap"], row_len)
    za = a_readout(yf, yb, bonus, g, lp["ap"])
    zb = b_mix(pbcd[0], lp["bp"])
    zc = conv_mix(pbcd[1], lp["cp"])
    zd = fourier_mix(pbcd[2])
    zs, w13 = [za, zb, zc, zd], None
    if cast_ffn_weights:
        sides = [(w, lp["layer"], 0, w.shape[1]) for w in (lp["w1"], lp["w3"])]
        merged, w13 = merge(h, zs, merge_w[0], lp["gate_b"], merge_w[1], sides=sides)
    else:
        merged = merge(h, zs, merge_w[0], lp["gate_b"], merge_w[1])
    mix = matmul(merged, lp["w_o"], D_MODEL, 0, 512, BF16, "out_proj", layer=lp["layer"])
    return mix, s_fin, w13


def _ffn(h2, lp, w13, w2_bf16=None):
    if w2_bf16 is None:
        u, (w2_bf16,) = ffn_up(h2, w13[0], w13[1], sides=[(lp["w2"], lp["layer"], 0, lp["w2"].shape[1])])
    else:
        u = ffn_up(h2, w13[0], w13[1])
    return matmul(u, w2_bf16, D_MODEL, 0, 512, BF16, "ffn_down"), w2_bf16


def kernel(x, c, ctx, c_ctx, mod_w, mod_b, norm_g, w_in, a_mu, a_w0, a_w_up, a_a0, a_a_up, a_g_up, a_k_k, a_k_a,
           a_r_k, a_ln, a_w_out, b_ln, b_ws, b_bs, b_w_out, c_dw, c_dw_b, c_ln, c_w_out, d_w_out, gate_w, gate_b,
           w_o, ffn_w1, ffn_w3, ffn_w2):
    depth = mod_w.shape[0]
    d = D_MODEL
    c_cols = jnp.concatenate([c.reshape(d, 1), c_ctx.reshape(d, 1), jnp.zeros((d, 6), F32)], axis=1)
    mods = modulation(c_cols, mod_w, mod_b.reshape(depth, 1, 6 * d))
    x_lat, x_ctx = x[0], ctx[0]
    zero_state = jnp.zeros((2, PAIRS, LANES, LANES), F32)
    w_in_t = jnp.swapaxes(w_in, 1, 2)
    w_in_packed = (round_rows(w_in_t, 0, A_PAD, 256), round_rows(w_in_t, A_PROJ, w_in.shape[2] - A_PROJ, 160))
    weights = (w_in_packed, a_mu, a_w0, a_w_up, a_a0, a_a_up, a_g_up, a_k_k, a_k_a, a_r_k, a_ln, a_w_out, b_ln, b_ws, b_bs,
               b_w_out, c_dw, c_dw_b, c_ln, c_w_out, d_w_out, gate_w, gate_b, w_o, ffn_w1, ffn_w3, ffn_w2)
    h_lat = h_ctx = None
    for l in range(depth):
        last = l == depth - 1
        lp = _layer_params(l, *weights)
        ng = [norm_g[l, i].reshape(1, d) for i in range(4)]
        ml = [mods[l, 0:1, i * d:(i + 1) * d] for i in range(6)]
        mc = [mods[l, 1:2, i * d:(i + 1) * d] for i in range(6)]
        if l == 0:
            h_lat = normmod(x_lat, ng[0], ml[0], ml[1])
            h_ctx = normmod(x_ctx, ng[0], mc[0], mc[1])

        pbcd_lat, merge_w = _in_proj_bcd(h_lat, lp, cast_merge_weights=True)

        if last:
            ctx_states = _rwkv_scan(_in_proj_a(h_ctx, lp), zero_state, lp["ap"], x_ctx.shape[0])[4]
        else:
            pbcd_ctx, _ = _in_proj_bcd(h_ctx, lp)
            mix_ctx, ctx_states, _ = _token_mix(h_ctx, zero_state, x_ctx.shape[0], lp, pbcd_ctx, merge_w)

        mix_lat, _, w13 = _token_mix(h_lat, ctx_states, GRID_W, lp, pbcd_lat, merge_w, cast_ffn_weights=True)
        x_lat, h2 = resnorm(x_lat, mix_lat, ml[2], ng[1], (ng[2], ml[3], ml[4]))
        y, w2_bf16 = _ffn(h2, lp, w13)
        if last:
            x_lat = resnorm(x_lat, y, ml[5], ng[3])
        else:
            ngn = norm_g[l + 1, 0].reshape(1, d)
            mln = [mods[l + 1, 0:1, i * d:(i + 1) * d] for i in range(2)]
            mcn = [mods[l + 1, 1:2, i * d:(i + 1) * d] for i in range(2)]
            x_lat, h_lat = resnorm(x_lat, y, ml[5], ng[3], (ngn, mln[0], mln[1]))
            x_ctx, h2c = resnorm(x_ctx, mix_ctx, mc[2], ng[1], (ng[2], mc[3], mc[4]))
            yc, _ = _ffn(h2c, lp, w13, w2_bf16)
            x_ctx, h_ctx = resnorm(x_ctx, yc, mc[5], ng[3], (ngn, mcn[0], mcn[1]))
    return x_lat[None]
```

```python
import functools
import math

import numpy as np
import jax
import jax.numpy as jnp
from jax import lax
from jax.experimental import pallas as pl
from jax.experimental.pallas import tpu as pltpu

F32, BF16 = jnp.float32, jnp.bfloat16

D_MODEL = 4096
DEPTH = 2
GRID_W = 64
BRANCH_W = 1024
HEAD = 64
HEADS = BRANCH_W // HEAD
PAIRS = HEADS // 2
LORA_W, LORA_A, LORA_G = 64, 64, 160
A_PROJ = 3 * BRANCH_W + LORA_W + LORA_A + LORA_G
A_GN_EPS = 64e-5
B_CHUNK = 128
B_GROUPS = 8
C_KERNEL = 31
D_GROUPS = 4
D_GROUP_CH = BRANCH_W // D_GROUPS
FFN_HIDDEN = 11008
RMS_EPS = 1e-6
LN_EPS = 1e-5

LANES = 128
SUBLANES = 8
VMEM_BUDGET = 60 * 1024 * 1024

A_LORA = 3 * BRANCH_W
A_GC = A_LORA + LANES
A_PAD = A_GC + 3 * LANES

MERGE_TM = 512
FFN_UP_TM = 2048
CHUNK = 64
CHAIN_STEP = 4
CONV_HALO = 16


def _cparams(n_axes, vmem_bytes):
    limit = int(min(max(vmem_bytes + (8 << 20), 32 << 20), VMEM_BUDGET))
    return pltpu.CompilerParams(dimension_semantics=("arbitrary",) * n_axes, vmem_limit_bytes=limit)


def _dot(a, b):
    return jnp.dot(a.astype(BF16), b.astype(BF16), preferred_element_type=F32)


def _dot_nt(a, b):
    return lax.dot_general(a.astype(BF16), b.astype(BF16), (((1,), (1,)), ((), ())), preferred_element_type=F32)


def _dot_tn(a, b):
    return lax.dot_general(a.astype(BF16), b.astype(BF16), (((0,), (0,)), ((), ())), preferred_element_type=F32)


def _dot_split_rhs(x, e):
    hi = x.astype(BF16)
    lo = (x - hi.astype(F32)).astype(BF16)
    f = lambda p: jnp.dot(p, e, preferred_element_type=F32)
    return f(hi) + f(lo)


def _dot_split_lhs(e, x):
    hi = x.astype(BF16)
    lo = (x - hi.astype(F32)).astype(BF16)
    f = lambda p: jnp.dot(e, p, preferred_element_type=F32)
    return f(hi) + f(lo)


def _sigmoid(x):
    return 1.0 / (1.0 + jnp.exp(-x))


def _silu(x):
    return x * _sigmoid(x)


def _softplus(x):
    return jnp.maximum(x, 0.0) + jnp.log(1.0 + jnp.exp(-jnp.abs(x)))


def _gelu_tanh(x):
    return 0.5 * x * (1.0 + jnp.tanh(math.sqrt(2.0 / math.pi) * (x + 0.044715 * (x * x * x))))


def _rmsnorm(x, g):
    return (x * lax.rsqrt(jnp.mean(x * x, axis=-1, keepdims=True) + RMS_EPS)) * g


def _layernorm(x, g, b):
    mu = jnp.mean(x, axis=-1, keepdims=True)
    xc = x - mu
    var = jnp.mean(xc * xc, axis=-1, keepdims=True)
    return (xc * lax.rsqrt(var + LN_EPS)) * g + b


MOD_ROWS = 32
MOD_COLS = 512


def _mod_body(c_ref, w_ref, b_ref, o_ref, s0_ref, s1_ref):
    d, tn = w_ref.shape[1], w_ref.shape[2]

    @pl.when((pl.program_id(0) == 0) & (pl.program_id(1) == 0))
    def _():
        s = _silu(c_ref[...])
        s0_ref[...] = jnp.broadcast_to(s[:, 0:1], (d, LANES))
        s1_ref[...] = jnp.broadcast_to(s[:, 1:2], (d, LANES))

    for c0 in range(0, tn, MOD_COLS):
        def step(k, acc):
            r0 = pl.multiple_of(k * MOD_ROWS, MOD_ROWS)
            w = w_ref[0, pl.ds(r0, MOD_ROWS), c0:c0 + MOD_COLS]
            s0 = jnp.concatenate([s0_ref[pl.ds(r0, MOD_ROWS), :]] * (MOD_COLS // LANES), axis=1)
            s1 = jnp.concatenate([s1_ref[pl.ds(r0, MOD_ROWS), :]] * (MOD_COLS // LANES), axis=1)
            return acc[0] + w * s0, acc[1] + w * s1

        z = jnp.zeros((MOD_ROWS, MOD_COLS), F32)
        a0, a1 = lax.fori_loop(0, d // MOD_ROWS, step, (z, z), unroll=2)
        bias = b_ref[0, :, c0:c0 + MOD_COLS]
        o_ref[0, :, c0:c0 + MOD_COLS] = jnp.concatenate(
            [jnp.sum(a0, axis=0, keepdims=True) + bias, jnp.sum(a1, axis=0, keepdims=True) + bias,
             jnp.zeros((6, MOD_COLS), F32)], axis=0)


def modulation(c_cols, mod_w, mod_b):
    depth, d, n = mod_w.shape
    tn = 1024
    return pl.pallas_call(
        _mod_body,
        grid=(depth, n // tn),
        in_specs=[pl.BlockSpec((d, 8), lambda l, j: (0, 0)),
                  pl.BlockSpec((1, d, tn), lambda l, j: (l, 0, j)),
                  pl.BlockSpec((1, 1, tn), lambda l, j: (l, 0, j))],
        out_specs=pl.BlockSpec((1, 8, tn), lambda l, j: (l, 0, j)),
        out_shape=jax.ShapeDtypeStruct((depth, 8, n), F32),
        scratch_shapes=[pltpu.VMEM((d, LANES), F32), pltpu.VMEM((d, LANES), F32)],
        compiler_params=_cparams(2, 2 * d * tn * 4 + 5 * d * LANES * 4),
        name="modulation",
    )(c_cols, mod_w, mod_b)


def _normmod_body(x_ref, g_ref, sh_ref, sc_ref, h_ref):
    h = _rmsnorm(x_ref[...], g_ref[...]) * (1.0 + sc_ref[...]) + sh_ref[...]
    h_ref[...] = h.astype(h_ref.dtype)


def normmod(x, g, shift, scale):
    t, d = x.shape
    tm = 256
    row = pl.BlockSpec((1, d), lambda i: (0, 0))
    return pl.pallas_call(
        _normmod_body,
        grid=(t // tm,),
        in_specs=[pl.BlockSpec((tm, d), lambda i: (i, 0)), row, row, row],
        out_specs=pl.BlockSpec((tm, d), lambda i: (i, 0)),
        out_shape=jax.ShapeDtypeStruct((t, d), BF16),
        compiler_params=_cparams(1, 2 * tm * d * 6),
        name="normmod",
    )(x, g, shift, scale)


def _resnorm_body(x_ref, y_ref, gate_ref, gpost_ref, gpre_ref, sh_ref, sc_ref, xo_ref, h_ref):
    xn = x_ref[...].astype(F32) + gate_ref[...] * _rmsnorm(y_ref[...].astype(F32), gpost_ref[...])
    xo_ref[...] = xn.astype(xo_ref.dtype)
    h = _rmsnorm(xn, gpre_ref[...]) * (1.0 + sc_ref[...]) + sh_ref[...]
    h_ref[...] = h.astype(h_ref.dtype)


def _res_body(x_ref, y_ref, gate_ref, gpost_ref, xo_ref):
    xo_ref[...] = x_ref[...].astype(F32) + gate_ref[...] * _rmsnorm(y_ref[...].astype(F32), gpost_ref[...])


def resnorm(x, y, gate, g_post, nxt=None):
    t, d = x.shape
    tm = 256
    row = pl.BlockSpec((1, d), lambda i: (0, 0))
    tile = pl.BlockSpec((tm, d), lambda i: (i, 0))
    if nxt is None:
        return pl.pallas_call(
            _res_body, grid=(t // tm,), in_specs=[tile, tile, row, row], out_specs=tile,
            out_shape=jax.ShapeDtypeStruct((t, d), F32),
            compiler_params=_cparams(1, 2 * tm * d * 12), name="residual",
        )(x, y, gate, g_post)
    return pl.pallas_call(
        _resnorm_body, grid=(t // tm,), in_specs=[tile, tile, row, row, row, row, row],
        out_specs=[tile, tile],
        out_shape=[jax.ShapeDtypeStruct((t, d), BF16), jax.ShapeDtypeStruct((t, d), BF16)],
        compiler_params=_cparams(1, 2 * tm * d * 14), name="residual_norm",
    )(x, y, gate, g_post, *nxt)


def _mm_body(x_ref, w_ref, *rest, n_side, cast, w_t):
    side_in, o_ref, side_out = rest[:n_side], rest[n_side], rest[n_side + 1:2 * n_side + 1]
    for s_in, s_out in zip(side_in, side_out):
        s_out[...] = s_in[...].astype(BF16)
    if cast:
        wb_ref = rest[-1]

        @pl.when(pl.program_id(1) == 0)
        def _():
            wb_ref[...] = w_ref[...].astype(BF16)

        w = wb_ref[...]
    else:
        w = w_ref[...]
    if w_t:
        acc = lax.dot_general(x_ref[...], w, (((1,), (1,)), ((), ())), preferred_element_type=F32)
    else:
        acc = jnp.dot(x_ref[...], w, preferred_element_type=F32)
    o_ref[...] = acc.astype(o_ref.dtype)


def _wspec(w, layer, rows, tn, col_block, w_t=False):
    if w_t:
        return pl.BlockSpec((None, tn, rows), lambda j, i: (layer, col_block(j), 0))
    if w.ndim == 3:
        return pl.BlockSpec((None, rows, tn), lambda j, i: (layer, 0, col_block(j)))
    return pl.BlockSpec((rows, tn), lambda j, i: (0, col_block(j)))


def _side_specs(sides, nj, ni):
    side_in, side_out, side_shape, side_args, vm = [], [], [], [], 0
    for arr, lyr, row0, nrows in sides:
        rps, cols = nrows // (nj * ni), arr.shape[2]
        assert rps * nj * ni == nrows and rps % 16 == 0 and row0 % rps == 0
        side_in.append(pl.BlockSpec((None, rps, cols),
                                    functools.partial(lambda j, i, lyr, b0: (lyr, b0 + j * ni + i, 0), lyr=lyr, b0=row0 // rps)))
        side_out.append(pl.BlockSpec((rps, cols), lambda j, i: (j * ni + i, 0)))
        side_shape.append(jax.ShapeDtypeStruct((nrows, cols), BF16))
        side_args.append(arr)
        vm += 2 * rps * cols * 6
    return side_in, side_out, side_shape, side_args, vm


def matmul(x, w, n, col0, tn, out_dtype, name, layer=None, sides=(), w_t=False, tm_max=None):
    m, k = x.shape
    tm = min(m, tm_max or (1024 if k <= 4096 else 512))
    j0 = col0 // tn
    assert col0 % tn == 0 and n % tn == 0 and m % tm == 0
    nj, ni = n // tn, m // tm
    cast = w.dtype == F32
    assert not (cast and w_t)
    vm = 2 * (tm * k * 2 + k * tn * w.dtype.itemsize + tm * tn * 4) + (k * tn * 2 if cast else 0)
    side_in, side_out, side_shape, side_args, side_vm = _side_specs(sides, nj, ni)
    vm += side_vm
    out = pl.pallas_call(
        functools.partial(_mm_body, n_side=len(sides), cast=cast, w_t=w_t),
        grid=(nj, ni),
        in_specs=[pl.BlockSpec((tm, k), lambda j, i: (i, 0)),
                  _wspec(w, layer, k, tn, lambda j: j + j0, w_t)] + side_in,
        out_specs=[pl.BlockSpec((tm, tn), lambda j, i: (i, j))] + side_out,
        out_shape=[jax.ShapeDtypeStruct((m, n), out_dtype)] + side_shape,
        scratch_shapes=[pltpu.VMEM((k, tn), BF16)] if cast else [],
        compiler_params=_cparams(2, vm),
        name=name,
    )(x, w, *side_args)
    return (out[0], out[1:]) if sides else out[0]


def _round_rows_body(w_ref, o_ref):
    o_ref[...] = w_ref[...].astype(BF16)


def round_rows(w, row0, nrows, blk):
    depth, _, cols = w.shape
    assert row0 % blk == 0 and nrows % blk == 0 and blk % 16 == 0
    b0 = row0 // blk
    return pl.pallas_call(
        _round_rows_body,
        grid=(depth, nrows // blk),
        in_specs=[pl.BlockSpec((None, blk, cols), lambda l, i: (l, b0 + i, 0))],
        out_specs=pl.BlockSpec((None, blk, cols), lambda l, i: (l, i, 0)),
        out_shape=jax.ShapeDtypeStruct((depth, nrows, cols), BF16),
        compiler_params=_cparams(2, 2 * blk * cols * 6),
        name="round_rows",
    )(w)


def _ffn_up_body(x_ref, w1_ref, w3_ref, *rest, n_side):
    side_in, o_ref, side_out = rest[:n_side], rest[n_side], rest[n_side + 1:]
    for s_in, s_out in zip(side_in, side_out):
        s_out[...] = s_in[...].astype(BF16)
    x = x_ref[...]
    a = jnp.dot(x, w1_ref[...], preferred_element_type=F32)
    b = jnp.dot(x, w3_ref[...], preferred_element_type=F32)
    o_ref[...] = (_silu(a) * b).astype(o_ref.dtype)


def ffn_up(h, w1, w3, sides=()):
    m, k = h.shape
    n = w1.shape[-1]
    tm, tn = min(m, FFN_UP_TM), 256
    nj, ni = n // tn, m // tm
    vm = 2 * (tm * k * 2 + 2 * k * tn * 2 + tm * tn * 2) + 5 * tm * tn * 4
    side_in, side_out, side_shape, side_args, side_vm = _side_specs(sides, nj, ni)
    out = pl.pallas_call(
        functools.partial(_ffn_up_body, n_side=len(sides)),
        grid=(nj, ni),
        in_specs=[pl.BlockSpec((tm, k), lambda j, i: (i, 0)),
                  pl.BlockSpec((k, tn), lambda j, i: (0, j)),
                  pl.BlockSpec((k, tn), lambda j, i: (0, j))] + side_in,
        out_specs=[pl.BlockSpec((tm, tn), lambda j, i: (i, j))] + side_out,
        out_shape=[jax.ShapeDtypeStruct((m, n), BF16)] + side_shape,
        compiler_params=_cparams(2, vm + side_vm),
        name="ffn_up",
    )(h, w1, w3, *side_args)
    return (out[0], out[1:]) if sides else out[0]


def _merge_body(h_ref, *refs, n_side):
    z, g_top, g_bot, bias, w_out = refs[0:4], refs[4:8], refs[8:12], refs[12:16], refs[16:20]
    side_in, o_ref, side_out = refs[20:20 + n_side], refs[20 + n_side], refs[21 + n_side:]
    for s_in, s_out in zip(side_in, side_out):
        s_out[...] = s_in[...].astype(BF16)
    half = g_top[0].shape[0]
    h_top, h_bot = h_ref[:, :half], h_ref[:, half:]
    acc = None
    for br in range(4):
        logits = (jnp.dot(h_top, g_top[br][...], preferred_element_type=F32)
                  + jnp.dot(h_bot, g_bot[br][...], preferred_element_type=F32) + bias[br][...])
        y = jnp.dot(z[br][...], w_out[br][...], preferred_element_type=F32)
        acc = _sigmoid(logits) * y if acc is None else acc + _sigmoid(logits) * y
    o_ref[...] = acc.astype(o_ref.dtype)


def merge(h, zs, gate_halves, gate_b, w_outs, sides=()):
    m, d = h.shape
    bw = zs[0].shape[1]
    half = gate_halves[0].shape[0]
    tm, tn = min(m, MERGE_TM), 256
    nj = d // tn
    hspec = pl.BlockSpec((tm, d), lambda j, i: (i, 0))
    zspec = pl.BlockSpec((tm, bw), lambda j, i: (i, 0))
    gspecs = [pl.BlockSpec((half, tn), functools.partial(lambda j, i, br: (0, br * nj + j), br=br)) for br in range(4)]
    bspecs = [pl.BlockSpec((1, tn), functools.partial(lambda j, i, br: (0, br * nj + j), br=br)) for br in range(4)]
    wspec = pl.BlockSpec((bw, tn), lambda j, i: (0, j))
    vm = 2 * (tm * d * 2 + 4 * tm * bw * 2 + tm * tn * 2 + 4 * (d + bw) * tn * 2) + 3 * tm * tn * 4
    side_in, side_out, side_shape, side_args, side_vm = _side_specs(sides, nj, m // tm)
    out = pl.pallas_call(
        functools.partial(_merge_body, n_side=len(sides)),
        grid=(nj, m // tm),
        in_specs=[hspec] + [zspec] * 4 + gspecs + gspecs + bspecs + [wspec] * 4 + side_in,
        out_specs=[pl.BlockSpec((tm, tn), lambda j, i: (i, j))] + side_out,
        out_shape=[jax.ShapeDtypeStruct((m, d), BF16)] + side_shape,
        compiler_params=_cparams(2, vm + side_vm),
        name="merge",
    )(h, *zs, *([gate_halves[0]] * 4), *([gate_halves[1]] * 4), gate_b, gate_b, gate_b, gate_b, *w_outs, *side_args)
    return (out[0], out[1:]) if sides else out[0]


def _token_shift(x, mu, row_len):
    tm = x.shape[0]
    pos = lax.broadcasted_iota(jnp.int32, x.shape, 0) & (row_len - 1)
    prev = jnp.where(pos == 0, 0.0, pltpu.roll(x, 1, 0))
    nxt = jnp.where(pos == row_len - 1, 0.0, pltpu.roll(x, tm - 1, 0))
    return x + mu * (0.5 * (prev + nxt) - x)


def _aprep_body(r_ref, k_ref, v_ref, wa_ref, g1_ref, g2_ref,
                mur_ref, muk_ref, muv_ref, muwa_ref, mug1_ref, mug2_ref,
                w0_ref, wup_ref, a0_ref, aup_ref, gup_ref, kk_ref, ka_ref, rk_ref, e_ref,
                ro_ref, vo_ref, kko_ref, lw_ref, kd_ref, b_ref, bonus_ref, g_ref, *, row_len):
    e = e_ref[...]
    shift = lambda p_ref, mu_ref: _token_shift(p_ref[...].astype(F32), mu_ref[...], row_len)
    xr, xk, xv = shift(r_ref, mur_ref), shift(k_ref, muk_ref), shift(v_ref, muv_ref)
    xwa = shift(wa_ref, muwa_ref)
    xg1, xg2 = shift(g1_ref, mug1_ref), shift(g2_ref, mug2_ref)
    kk = xk * kk_ref[...]
    kk = kk * lax.rsqrt(jnp.maximum(_dot_split_rhs(kk * kk, e), 1e-12))
    tw = jnp.tanh(xwa)
    bonus = None
    for d in range(2):
        w = -_softplus(-(w0_ref[d] + _dot(tw, wup_ref[d]))) - 0.5
        a = _sigmoid(a0_ref[d] + _dot(xwa, aup_ref[d]))
        kd = xk * (1.0 + (a - 1.0) * ka_ref[...])
        lw_ref[d] = -jnp.exp(w)
        kd_ref[d] = kd
        b_ref[d] = kk * a
        bn = _dot(xr * kd * rk_ref[...], e) * xv
        bonus = bn if bonus is None else bonus + bn
    ro_ref[...] = xr
    vo_ref[...] = xv
    kko_ref[...] = kk
    bonus_ref[...] = bonus
    g_ref[...] = _dot(_sigmoid(xg1), gup_ref[0]) + _dot(_sigmoid(xg2), gup_ref[1])


N_PREP_IN = 21


def _rwkv_maps_body(*refs, row_len, nchunk):
    prep_in, (tri_ref, msk_ref) = refs[:N_PREP_IN], refs[N_PREP_IN:N_PREP_IN + 2]
    rhat_ref, y0_ref, m_ref, n_ref, bonus_ref, g_ref = refs[N_PREP_IN + 2:N_PREP_IN + 8]
    r_s, v_s, kk_s, lw_s, kd_s, b_s = refs[N_PREP_IN + 8:]
    _aprep_body(*prep_in, r_s, v_s, kk_s, lw_s, kd_s, b_s, bonus_ref, g_ref, row_len=row_len)
    _scan_pre_body(r_s, v_s, kk_s, lw_s, kd_s, b_s, tri_ref, msk_ref, rhat_ref, y0_ref, m_ref, n_ref, nchunk=nchunk)


def rwkv_maps(pa, ap, row_len, tri, msk):
    t = pa.shape[0]
    nchunk = 4
    tm = nchunk * CHUNK
    nc = t // CHUNK
    nb = BRANCH_W // LANES
    col = lambda c: pl.BlockSpec((tm, LANES), functools.partial(lambda i, p, c: (i, c + p), c=c))
    fix = lambda c: pl.BlockSpec((tm, LANES), functools.partial(lambda i, p, c: (i, c), c=c))
    mcol = lambda c: pl.BlockSpec((1, LANES), functools.partial(lambda i, p, c: (0, c + p), c=c))
    mfix = lambda c: pl.BlockSpec((1, LANES), functools.partial(lambda i, p, c: (0, c), c=c))
    prow = pl.BlockSpec((1, LANES), lambda i, p: (0, p))
    p2 = pl.BlockSpec((2, 1, LANES), lambda i, p: (0, 0, p))
    up2 = pl.BlockSpec((2, LANES, LANES), lambda i, p: (0, 0, p))
    out1 = pl.BlockSpec((tm, LANES), lambda i, p: (i, p))
    out2 = pl.BlockSpec((2, tm, LANES), lambda i, p: (0, i, p))
    mspec = pl.BlockSpec((2, nchunk, 1, HEAD, LANES), lambda i, p: (0, i, p, 0, 0))
    s1 = jax.ShapeDtypeStruct((t, BRANCH_W), F32)
    s2 = jax.ShapeDtypeStruct((2, t, BRANCH_W), BF16)
    sm = jax.ShapeDtypeStruct((2, nc, PAIRS, HEAD, LANES), F32)
    lora, gc = A_LORA // LANES, A_GC // LANES
    tile = lambda lead=(): pltpu.VMEM(lead + (tm, LANES), F32)
    return pl.pallas_call(
        functools.partial(_rwkv_maps_body, row_len=row_len, nchunk=nchunk),
        grid=(t // tm, nb),
        in_specs=[col(0), col(nb), col(2 * nb), fix(lora), fix(gc), fix(gc + 1),
                  mcol(0), mcol(nb), mcol(2 * nb), mfix(lora), mfix(gc), mfix(gc + 1),
                  p2, up2, p2, up2, up2, prow, prow, prow,
                  pl.BlockSpec((LANES, LANES), lambda i, p: (0, 0)),
                  pl.BlockSpec((2, CHUNK, CHUNK), lambda i, p: (0, 0, 0)),
                  pl.BlockSpec(msk.shape, lambda i, p: (0, 0, 0, 0))],
        out_specs=[out2, out2, mspec, mspec, out1, out1],
        out_shape=[s2, s2, sm, sm, s1, s1],
        scratch_shapes=[tile(), tile(), tile(), tile((2,)), tile((2,)), tile((2,))],
        compiler_params=_cparams(2, 24 << 20),
        name="rwkv_chunk_maps",
    )(pa, pa, pa, pa, pa, pa, ap["mu"], ap["mu"], ap["mu"], ap["mu"], ap["mu"], ap["mu"],
      ap["w0"], ap["w_up"], ap["a0"], ap["a_up"], ap["g_up"], ap["k_k"], ap["k_a"], ap["r_k"], ap["e2"], tri, msk)


def _pair_rows(x):
    lane = lax.broadcasted_iota(jnp.int32, x.shape, 1)
    return jnp.concatenate([jnp.where(lane < HEAD, x, 0.0), jnp.where(lane >= HEAD, x, 0.0)], axis=0)


MSK_STRICT, MSK_INCL, MSK_EYE, MSK_DIAG8, MSK_OFF8, MSK_OFF16, MSK_OFF32 = range(7)


def _chunk_mask_tables():
    n = 2 * CHUNK
    r = np.arange(n)[:, None]
    c = np.arange(n)[None, :]
    same = (r // CHUNK) == (c // CHUNK)
    out = np.zeros((2, 7, n, n), np.float32)
    for d in range(2):
        before = (c > r) if d else (c < r)
        out[d, MSK_STRICT] = same & before
        out[d, MSK_INCL] = same & (before | (r == c))
        out[d, MSK_EYE] = r == c
        out[d, MSK_DIAG8] = ((r // 8) == (c // 8)) & before
        for idx, s in ((MSK_OFF8, 8), (MSK_OFF16, 16), (MSK_OFF32, 32)):
            blk = (r // (2 * s)) == (c // (2 * s))
            rh, ch = (r // s) % 2, (c // s) % 2
            out[d, idx] = blk & ((rh == 0) & (ch == 1) if d else (rh == 1) & (ch == 0))
    return jnp.asarray(out)


def _scan_pre_body(r_ref, v_ref, kk_ref, lw_ref, kd_ref, b_ref, tri_ref, msk_ref, rhat_ref, y0_ref, m_ref, n_ref, *, nchunk):
    c, n = CHUNK, 2 * CHUNK
    chains = [(d, ci) for d in range(2) for ci in range(nchunk)]
    each = lambda f, *cols: [f(*xs) for xs in zip(*cols)]
    msk = lambda d, k: msk_ref[d, k]
    rows = lambda ci: slice(ci * c, (ci + 1) * c)

    lw = [lw_ref[d, rows(ci), :] for d, ci in chains]
    cum = [_dot_split_lhs(tri_ref[d], x) for (d, _), x in zip(chains, lw)]
    tot = [x[0:1] if d else x[c - 1:c] for (d, _), x in zip(chains, cum)]
    g_inv = each(lambda x: jnp.exp(-x), cum)
    g_tail = each(lambda t, x: jnp.exp(t - x), tot, cum)
    atp = [_pair_rows(-kk_ref[rows(ci), :] * jnp.exp(x - l)) for (_, ci), x, l in zip(chains, cum, lw)]
    rtp = [_pair_rows(r_ref[rows(ci), :] * jnp.exp(x)) for (_, ci), x in zip(chains, cum)]
    btp = [_pair_rows(b_ref[d, rows(ci), :] * g) for (d, ci), g in zip(chains, g_inv)]
    ktp = [_pair_rows(kd_ref[d, rows(ci), :] * g) for (d, ci), g in zip(chains, g_inv)]
    bhp = [_pair_rows(b_ref[d, rows(ci), :] * g) for (d, ci), g in zip(chains, g_tail)]
    khp = [_pair_rows(kd_ref[d, rows(ci), :] * g) for (d, ci), g in zip(chains, g_tail)]
    vp = [_pair_rows(v_ref[rows(ci), :]) for _, ci in chains]

    big = each(lambda a, r, b, k: _dot_nt(jnp.concatenate([a, r], axis=0), jnp.concatenate([b, k], axis=0)),
               atp, rtp, btp, ktp)
    a_ab = [jnp.where(msk(d, MSK_STRICT) > 0.0, x[:n, :n], 0.0) for (d, _), x in zip(chains, big)]
    a_ak = [jnp.where(msk(d, MSK_STRICT) > 0.0, x[:n, n:], 0.0) for (d, _), x in zip(chains, big)]
    a_rb = [jnp.where(msk(d, MSK_INCL) > 0.0, x[n:, :n], 0.0) for (d, _), x in zip(chains, big)]
    a_rk = [jnp.where(msk(d, MSK_INCL) > 0.0, x[n:, n:], 0.0) for (d, _), x in zip(chains, big)]

    n8 = [x * msk(d, MSK_DIAG8) for (d, _), x in zip(chains, a_ab)]
    t = [msk(d, MSK_EYE) + x for (d, _), x in zip(chains, n8)]
    n2 = each(_dot, n8, n8)
    t = each(lambda x, p: x + _dot(p, x), t, n2)
    n4 = each(_dot, n2, n2)
    t = each(lambda x, p: x + _dot(p, x), t, n4)
    for idx in (MSK_OFF8, MSK_OFF16, MSK_OFF32):
        off = [x * msk(d, idx) for (d, _), x in zip(chains, a_ab)]
        tn = each(_dot, t, off)
        t = each(lambda x, p: x + _dot(p, x), t, tn)

    akv = each(_dot, a_ak, vp)
    w_u = each(lambda ti, a, x: _dot(ti, jnp.concatenate([a, x], axis=1)), t, atp, akv)
    ry = each(_dot, a_rb, w_u)
    rkv = each(_dot, a_rk, vp)
    mn = each(_dot_tn, bhp, w_u)
    kv = each(_dot_tn, khp, vp)
    for i, (d, ci) in enumerate(chains):
        rhat_p = rtp[i] + ry[i][:, :n]
        y0_p = ry[i][:, n:] + rkv[i]
        rhat_ref[d, rows(ci), :] = (rhat_p[:c] + rhat_p[c:]).astype(rhat_ref.dtype)
        y0_ref[d, rows(ci), :] = (y0_p[:c] + y0_p[c:]).astype(y0_ref.dtype)
        m_i = mn[i][:, :n] + msk(d, MSK_EYE) * jnp.exp(tot[i])
        n_i = mn[i][:, n:] + kv[i]
        m_ref[d, ci, 0] = m_i[:c] + m_i[c:]
        n_ref[d, ci, 0] = n_i[:c] + n_i[c:]


def _scan_seq_body(s0_ref, mf_ref, mb_ref, nf_ref, nb_ref, rf_ref, rb_ref, yf0_ref, yb0_ref,
                   yf_ref, yb_ref, sfin_ref, s_ref):
    c = pl.program_id(0)

    @pl.when(c == 0)
    def _():
        s_ref[...] = s0_ref[...]

    for sub in range(CHAIN_STEP):
        for d, (m_ref, n_ref, rh_ref, y0_ref, y_ref) in enumerate(((mf_ref, nf_ref, rf_ref, yf0_ref, yf_ref),
                                                                     (mb_ref, nb_ref, rb_ref, yb0_ref, yb_ref))):
            lc = sub if d == 0 else CHAIN_STEP - 1 - sub
            rows = slice(lc * CHUNK, (lc + 1) * CHUNK)
            for p in range(PAIRS):
                ln = slice(p * LANES, (p + 1) * LANES)
                s = s_ref[d, p]
                y_ref[rows, ln] = _dot(rh_ref[0, rows, ln], s) + y0_ref[0, rows, ln].astype(F32)
                s_ref[d, p] = _dot(_pair_rows(m_ref[0, lc, p]), s) + _pair_rows(n_ref[0, lc, p])

    @pl.when(c == pl.num_programs(0) - 1)
    def _():
        sfin_ref[...] = s_ref[...]


def scan_seq(s0, rhat, y0, m, n):
    t = rhat.shape[1]
    nc = t // (CHUNK * CHAIN_STEP)
    fwd5 = lambda c: (0, c, 0, 0, 0)
    bwd5 = lambda c: (1, nc - 1 - c, 0, 0, 0)
    mblk = (1, CHAIN_STEP, PAIRS, HEAD, LANES)
    rblk = (1, CHAIN_STEP * CHUNK, BRANCH_W)
    sblk = pl.BlockSpec((2, PAIRS, LANES, LANES), lambda c: (0, 0, 0, 0))
    ys = jax.ShapeDtypeStruct((t, BRANCH_W), F32)
    return pl.pallas_call(
        _scan_seq_body,
        grid=(nc,),
        in_specs=[sblk,
                  pl.BlockSpec(mblk, fwd5), pl.BlockSpec(mblk, bwd5),
                  pl.BlockSpec(mblk, fwd5), pl.BlockSpec(mblk, bwd5),
                  pl.BlockSpec(rblk, lambda c: (0, c, 0)), pl.BlockSpec(rblk, lambda c: (1, nc - 1 - c, 0)),
                  pl.BlockSpec(rblk, lambda c: (0, c, 0)), pl.BlockSpec(rblk, lambda c: (1, nc - 1 - c, 0))],
        out_specs=[pl.BlockSpec((CHAIN_STEP * CHUNK, BRANCH_W), lambda c: (c, 0)),
                   pl.BlockSpec((CHAIN_STEP * CHUNK, BRANCH_W), lambda c: (nc - 1 - c, 0)),
                   sblk],
        out_shape=[ys, ys, jax.ShapeDtypeStruct((2, PAIRS, LANES, LANES), F32)],
        scratch_shapes=[pltpu.VMEM((2, PAIRS, LANES, LANES), F32)],
        compiler_params=_cparams(1, 16 << 20),
        name="rwkv_chain",
    )(s0, m, m, n, n, rhat, rhat, y0, y0)


def _areadout_body(yf_ref, yb_ref, bonus_ref, g_ref, lng_ref, lnb_ref, e_ref, o_ref):
    e = e_ref[...]
    for p in range(BRANCH_W // LANES):
        ln = slice(p * LANES, (p + 1) * LANES)
        y = yf_ref[:, ln] + yb_ref[:, ln]
        mu = _dot_split_rhs(y, e) * (1.0 / HEAD)
        yc = y - mu
        var = _dot_split_rhs(yc * yc, e) * (1.0 / HEAD)
        yn = (yc * lax.rsqrt(var + A_GN_EPS)) * lng_ref[:, ln] + lnb_ref[:, ln]
        o_ref[:, ln] = ((yn + bonus_ref[:, ln]) * g_ref[:, ln]).astype(o_ref.dtype)


def a_readout(yf, yb, bonus, g, ap):
    t = yf.shape[0]
    tm = 256
    blk = pl.BlockSpec((tm, BRANCH_W), lambda i: (i, 0))
    prow = pl.BlockSpec((1, BRANCH_W), lambda i: (0, 0))
    return pl.pallas_call(
        _areadout_body,
        grid=(t // tm,),
        in_specs=[blk, blk, blk, blk, prow, prow, pl.BlockSpec((LANES, LANES), lambda i: (0, 0))],
        out_specs=blk,
        out_shape=jax.ShapeDtypeStruct((t, BRANCH_W), BF16),
        compiler_params=_cparams(1, 2 * 5 * tm * BRANCH_W * 4),
        name="rwkv_readout",
    )(yf, yb, bonus, g, ap["ln_g"], ap["ln_b"], ap["e2"])


def _bmix_body(u_ref, v_ref, lng_ref, lnb_ref, ws_ref, bs_ref, o_ref):
    u = _gelu_tanh(u_ref[...].astype(F32))
    v = _layernorm(_gelu_tanh(v_ref[...].astype(F32)), lng_ref[...], lnb_ref[...]).astype(BF16)
    for ci in range(u.shape[0] // B_CHUNK):
        rows = slice(ci * B_CHUNK, (ci + 1) * B_CHUNK)
        for g in range(B_GROUPS):
            ln = slice(g * LANES, (g + 1) * LANES)
            s = jnp.dot(ws_ref[g], v[rows, ln], preferred_element_type=F32) + bs_ref[:, ln]
            o_ref[rows, ln] = (u[rows, ln] * s).astype(o_ref.dtype)


def b_mix(pb, bp):
    t = pb.shape[0]
    tm = 256
    row = pl.BlockSpec((1, BRANCH_W), lambda i: (0, 0))
    return pl.pallas_call(
        _bmix_body,
        grid=(t // tm,),
        in_specs=[pl.BlockSpec((tm, BRANCH_W), lambda i: (i, 0)), pl.BlockSpec((tm, BRANCH_W), lambda i: (i, 1)),
                  row, row, pl.BlockSpec((B_GROUPS, B_CHUNK, B_CHUNK), lambda i: (0, 0, 0)),
                  pl.BlockSpec((B_CHUNK, BRANCH_W), lambda i: (0, 0))],
        out_specs=pl.BlockSpec((tm, BRANCH_W), lambda i: (i, 0)),
        out_shape=jax.ShapeDtypeStruct((t, BRANCH_W), BF16),
        compiler_params=_cparams(1, 12 * tm * BRANCH_W * 4),
        name="gmlp_mix",
    )(pb, pb, bp["ln_g"], bp["ln_b"], bp["ws"], bp["bs"])


def _conv_body(ac_ref, gc_ref, ap_ref, gp_ref, an_ref, gn_ref, dw_ref, dwb_ref, lng_ref, lnb_ref,
               o_ref, zs_ref, cs_ref, *, tm):
    i = pl.program_id(0)
    last = pl.num_programs(0) - 1
    glu = lambda a_ref, g_ref: a_ref[...].astype(F32) * _sigmoid(g_ref[...].astype(F32))
    zs_ref[0:CONV_HALO, :] = jnp.where(i == 0, 0.0, glu(ap_ref, gp_ref))
    zs_ref[CONV_HALO:CONV_HALO + tm, :] = glu(ac_ref, gc_ref)
    zs_ref[CONV_HALO + tm:, :] = jnp.where(i == last, 0.0, glu(an_ref, gn_ref))
    rb, lb = 32, 256
    win = rb + 2 * CONV_HALO
    first = CONV_HALO - C_KERNEL // 2
    for r0 in range(0, tm, rb):
        for l0 in range(0, BRANCH_W, lb):
            w = zs_ref[r0:r0 + win, l0:l0 + lb]
            acc = jnp.zeros((rb, lb), F32) + dwb_ref[:, l0:l0 + lb]
            for s in range(SUBLANES):
                ws = w if s == 0 else pltpu.roll(w, win - s, 0)
                for j in range(C_KERNEL):
                    if (first + j) % SUBLANES == s:
                        a = (first + j) - s
                        acc = acc + ws[a:a + rb] * dw_ref[j:j + 1, l0:l0 + lb]
            cs_ref[r0:r0 + rb, l0:l0 + lb] = acc
    o_ref[...] = _silu(_layernorm(cs_ref[...], lng_ref[...], lnb_ref[...])).astype(o_ref.dtype)


def conv_mix(pc, cp):
    t = pc.shape[0]
    tm = 256
    hb = tm // CONV_HALO
    nh = t // CONV_HALO
    cur = lambda c: pl.BlockSpec((tm, BRANCH_W), functools.partial(lambda i, c: (i, c), c=c))
    prev = lambda c: pl.BlockSpec((CONV_HALO, BRANCH_W), functools.partial(lambda i, c: (jnp.maximum(i * hb - 1, 0), c), c=c))
    nxt = lambda c: pl.BlockSpec((CONV_HALO, BRANCH_W), functools.partial(lambda i, c: (jnp.minimum((i + 1) * hb, nh - 1), c), c=c))
    row = pl.BlockSpec((1, BRANCH_W), lambda i: (0, 0))
    return pl.pallas_call(
        functools.partial(_conv_body, tm=tm),
        grid=(t // tm,),
        in_specs=[cur(0), cur(1), prev(0), prev(1), nxt(0), nxt(1),
                  pl.BlockSpec((C_KERNEL + 1, BRANCH_W), lambda i: (0, 0)), row, row, row],
        out_specs=pl.BlockSpec((tm, BRANCH_W), lambda i: (i, 0)),
        out_shape=jax.ShapeDtypeStruct((t, BRANCH_W), BF16),
        scratch_shapes=[pltpu.VMEM((tm + 2 * CONV_HALO, BRANCH_W), F32), pltpu.VMEM((tm, BRANCH_W), F32)],
        compiler_params=_cparams(1, 16 * tm * BRANCH_W * 4),
        name="conv_mix",
    )(pc, pc, pc, pc, pc, pc, cp["dw"], cp["dw_b"], cp["ln_g"], cp["ln_b"])


FFT_N1, FFT_N2 = 64, 128


def _dft_tables(t):
    two_pi = 2.0 * np.pi
    cidx = np.arange(D_GROUP_CH)
    ph = two_pi * np.outer(cidx, cidx) / D_GROUP_CH
    chan = np.concatenate([np.cos(ph), np.sin(ph)], axis=0)
    if t <= 256:
        n = np.arange(t)
        th = two_pi * np.outer(n, n) / t
        m2 = np.concatenate([np.cos(th), -np.sin(th)], axis=0)
        return None, m2.astype(np.float32), chan.astype(np.float32)
    n1, n2 = FFT_N1, FFT_N2
    assert t == n1 * n2
    k1 = np.arange(n1)
    tok = (n2 * np.arange(n1))[None, None, :] + np.arange(n2)[:, None, None]
    th = two_pi * (k1[None, :, None] * tok) / t
    g1 = np.concatenate([np.cos(th), -np.sin(th)], axis=1)
    q = np.arange(n2)
    th2 = two_pi * np.outer(q, q) / n2
    c2, s2 = np.cos(th2), np.sin(th2)
    m2 = np.block([[c2, s2], [-s2, c2]])
    return g1.astype(np.float32), m2.astype(np.float32), chan.astype(np.float32)


FFT_STEP = 4


def _fft1_body(x_ref, g_ref, o_ref):
    for q in range(FFT_STEP):
        ln = slice(q * BRANCH_W, (q + 1) * BRANCH_W)
        o_ref[:, ln] = _dot(g_ref[q], x_ref[:, ln])


def _fft2_body(z_ref, m2_ref, ch_ref, o_ref, *, n_out, scale, stacked):
    z = jnp.concatenate([z_ref[0], z_ref[1]], axis=0) if stacked else z_ref[...]
    x = _dot(m2_ref[...], z)
    xr, xi = x[:n_out], x[n_out:]
    for g in range(D_GROUPS):
        ln = slice(g * D_GROUP_CH, (g + 1) * D_GROUP_CH)
        f = _dot(xr[:, ln], ch_ref[0:D_GROUP_CH]) + _dot(xi[:, ln], ch_ref[D_GROUP_CH:])
        o_ref[:, ln] = (f * scale).astype(o_ref.dtype)


def fourier_mix(pd):
    t = pd.shape[0]
    g1, m2, chan = (None if a is None else jnp.asarray(a) for a in _dft_tables(t))
    scale = 1.0 / math.sqrt(t * D_GROUP_CH)
    chspec = pl.BlockSpec((2 * D_GROUP_CH, D_GROUP_CH), lambda i: (0, 0))
    if g1 is None:
        return pl.pallas_call(
            functools.partial(_fft2_body, n_out=t, scale=scale, stacked=False),
            grid=(1,),
            in_specs=[pl.BlockSpec((t, BRANCH_W), lambda i: (0, 0)), pl.BlockSpec((2 * t, t), lambda i: (0, 0)), chspec],
            out_specs=pl.BlockSpec((t, BRANCH_W), lambda i: (0, 0)),
            out_shape=jax.ShapeDtypeStruct((t, BRANCH_W), BF16),
            compiler_params=_cparams(1, 16 << 20),
            name="fourier_small",
        )(pd, m2, chan)
    n1, n2 = FFT_N1, FFT_N2
    z = pl.pallas_call(
        _fft1_body,
        grid=(n2 // FFT_STEP,),
        in_specs=[pl.BlockSpec((n1, FFT_STEP * BRANCH_W), lambda q: (0, q)),
                  pl.BlockSpec((FFT_STEP, 2 * n1, n1), lambda q: (q, 0, 0))],
        out_specs=pl.BlockSpec((2 * n1, FFT_STEP * BRANCH_W), lambda q: (0, q)),
        out_shape=jax.ShapeDtypeStruct((2 * n1, n2 * BRANCH_W), F32),
        compiler_params=_cparams(1, 8 << 20),
        name="fourier_stage1",
    )(pd.reshape(n1, n2 * BRANCH_W), g1)
    f = pl.pallas_call(
        functools.partial(_fft2_body, n_out=n2, scale=scale, stacked=True),
        grid=(n1,),
        in_specs=[pl.BlockSpec((2, None, n2, BRANCH_W), lambda k: (0, k, 0, 0)),
                  pl.BlockSpec((2 * n2, 2 * n2), lambda k: (0, 0)), chspec],
        out_specs=pl.BlockSpec((n2, BRANCH_W), lambda k: (0, k)),
        out_shape=jax.ShapeDtypeStruct((n2, n1 * BRANCH_W), BF16),
        compiler_params=_cparams(1, 16 << 20),
        name="fourier_stage2",
    )(z.reshape(2, n1, n2, BRANCH_W), m2, chan)
    return f.reshape(t, BRANCH_W)


def _align_a(x):
    return jnp.pad(x, [(0, 0)] * (x.ndim - 1) + [(0, A_PAD - x.shape[-1])])


def _pad_rows(x, before, rows):
    return jnp.pad(x, [(0, 0)] * (x.ndim - 2) + [(before, rows - before - x.shape[-2]), (0, 0)])


def _layer_params(l, w_in, a_mu, a_w0, a_w_up, a_a0, a_a_up, a_g_up, a_k_k, a_k_a, a_r_k, a_ln, a_w_out,
                  b_ln, b_ws, b_bs, b_w_out, c_dw, c_dw_b, c_ln, c_w_out, d_w_out, gate_w, gate_b, w_o,
                  ffn_w1, ffn_w3, ffn_w2):
    row = lambda v: v.reshape(1, -1)
    hid = np.arange(LANES) // HEAD
    e2 = jnp.asarray((hid[:, None] == hid[None, :]).astype(np.float32)).astype(BF16)
    g_up = jnp.stack([a_g_up[l][:LANES], _pad_rows(a_g_up[l][LANES:], 0, LANES)])
    ap = dict(mu=_align_a(row(a_mu[l])), w0=a_w0[l][:, None, :], w_up=_pad_rows(a_w_up[l], 0, LANES),
              a0=a_a0[l][:, None, :], a_up=_pad_rows(a_a_up[l], LORA_W, LANES), g_up=g_up,
              k_k=row(a_k_k[l]), k_a=row(a_k_a[l]), r_k=row(a_r_k[l]), ln_g=row(a_ln[l][0]), ln_b=row(a_ln[l][1]), e2=e2)
    bs_exp = jnp.repeat(jnp.swapaxes(b_bs[l], 0, 1), LANES, axis=1)
    return dict(
        layer=l, w_a=w_in[0], w_bcd=w_in[1], ap=ap,
        bp=dict(ln_g=row(b_ln[l][0]), ln_b=row(b_ln[l][1]), ws=b_ws[l].astype(BF16), bs=bs_exp),
        cp=dict(dw=_pad_rows(c_dw[l], 0, C_KERNEL + 1), dw_b=row(c_dw_b[l]), ln_g=row(c_ln[l][0]), ln_b=row(c_ln[l][1])),
        w_outs=[a_w_out, b_w_out, c_w_out, d_w_out],
        gate_w=gate_w, gate_b=row(gate_b[l]), w_o=w_o,
        w1=ffn_w1, w3=ffn_w3, w2=ffn_w2)


def _tri_tables():
    i = np.arange(CHUNK)
    lower = (i[None, :] <= i[:, None]).astype(np.float32)
    return jnp.asarray(np.stack([lower, lower.T])).astype(BF16)


def _rwkv_scan(pa, s0, ap, row_len):
    rhat, y0, m, n, bonus, g = rwkv_maps(pa, ap, row_len, _tri_tables(), _chunk_mask_tables())
    yf, yb, s_fin = scan_seq(s0, rhat, y0, m, n)
    return yf, yb, bonus, g, s_fin


def _in_proj_a(h, lp):
    return matmul(h, lp["w_a"], A_PAD, 0, 512, BF16, "in_proj_a", layer=lp["layer"], w_t=True, tm_max=2048)


def _in_proj_bcd(h, lp, cast_merge_weights=False):
    l = lp["layer"]
    proj = functools.partial(matmul, h, lp["w_bcd"], tn=512, out_dtype=BF16, layer=l, w_t=True)
    if not cast_merge_weights:
        return (proj(n=2 * BRANCH_W, col0=0, name="in_proj_b"),
                proj(n=2 * BRANCH_W, col0=2 * BRANCH_W, name="in_proj_c"),
                proj(n=BRANCH_W, col0=4 * BRANCH_W, name="in_proj_d")), None
    half = D_MODEL // 2
    pb, (g_top,) = proj(n=2 * BRANCH_W, col0=0, name="in_proj_b", sides=[(lp["gate_w"], l, 0, half)])
    pc, (g_bot,) = proj(n=2 * BRANCH_W, col0=2 * BRANCH_W, name="in_proj_c", sides=[(lp["gate_w"], l, half, half)])
    pd, w_outs = proj(n=BRANCH_W, col0=4 * BRANCH_W, name="in_proj_d", sides=[(wo, l, 0, BRANCH_W) for wo in lp["w_outs"]])
    return (pb, pc, pd), ((g_top, g_bot), list(w_outs))


def _token_mix(h, s0, row_len, lp, pbcd, merge_w, cast_ffn_weights=False):
    pa = _in_proj_a(h, lp)
    yf, yb, bonus, g, s_fin = _rwkv_scan(pa, s0, lp["ap"], row_len)
    za = a_readout(yf, yb, bonus, g, lp["ap"])
    zb = b_mix(pbcd[0], lp["bp"])
    zc = conv_mix(pbcd[1], lp["cp"])
    zd = fourier_mix(pbcd[2])
    zs, w13 = [za, zb, zc, zd], None
    if cast_ffn_weights:
        sides = [(w, lp["layer"], 0, w.shape[1]) for w in (lp["w1"], lp["w3"])]
        merged, w13 = merge(h, zs, merge_w[0], lp["gate_b"], merge_w[1], sides=sides)
    else:
        merged = merge(h, zs, merge_w[0], lp["gate_b"], merge_w[1])
    mix = matmul(merged, lp["w_o"], D_MODEL, 0, 512, BF16, "out_proj", layer=lp["layer"])
    return mix, s_fin, w13


def _ffn(h2, lp, w13, w2_bf16=None):
    if w2_bf16 is None:
        u, (w2_bf16,) = ffn_up(h2, w13[0], w13[1], sides=[(lp["w2"], lp["layer"], 0, lp["w2"].shape[1])])
    else:
        u = ffn_up(h2, w13[0], w13[1])
    return matmul(u, w2_bf16, D_MODEL, 0, 512, BF16, "ffn_down"), w2_bf16


def kernel(x, c, ctx, c_ctx, mod_w, mod_b, norm_g, w_in, a_mu, a_w0, a_w_up, a_a0, a_a_up, a_g_up, a_k_k, a_k_a,
           a_r_k, a_ln, a_w_out, b_ln, b_ws, b_bs, b_w_out, c_dw, c_dw_b, c_ln, c_w_out, d_w_out, gate_w, gate_b,
           w_o, ffn_w1, ffn_w3, ffn_w2):
    depth = mod_w.shape[0]
    d = D_MODEL
    c_cols = jnp.concatenate([c.reshape(d, 1), c_ctx.reshape(d, 1), jnp.zeros((d, 6), F32)], axis=1)
    mods = modulation(c_cols, mod_w, mod_b.reshape(depth, 1, 6 * d))
    x_lat, x_ctx = x[0], ctx[0]
    zero_state = jnp.zeros((2, PAIRS, LANES, LANES), F32)
    w_in_t = jnp.swapaxes(w_in, 1, 2)
    w_in_packed = (round_rows(w_in_t, 0, A_PAD, 256), round_rows(w_in_t, A_PROJ, w_in.shape[2] - A_PROJ, 160))
    weights = (w_in_packed, a_mu, a_w0, a_w_up, a_a0, a_a_up, a_g_up, a_k_k, a_k_a, a_r_k, a_ln, a_w_out, b_ln, b_ws, b_bs,
               b_w_out, c_dw, c_dw_b, c_ln, c_w_out, d_w_out, gate_w, gate_b, w_o, ffn_w1, ffn_w3, ffn_w2)
    h_lat = h_ctx = None
    for l in range(depth):
        last = l == depth - 1
        lp = _layer_params(l, *weights)
        ng = [norm_g[l, i].reshape(1, d) for i in range(4)]
        ml = [mods[l, 0:1, i * d:(i + 1) * d] for i in range(6)]
        mc = [mods[l, 1:2, i * d:(i + 1) * d] for i in range(6)]
        if l == 0:
            h_lat = normmod(x_lat, ng[0], ml[0], ml[1])
            h_ctx = normmod(x_ctx, ng[0], mc[0], mc[1])

        pbcd_lat, merge_w = _in_proj_bcd(h_lat, lp, cast_merge_weights=True)

        if last:
            ctx_states = _rwkv_scan(_in_proj_a(h_ctx, lp), zero_state, lp["ap"], x_ctx.shape[0])[4]
        else:
            pbcd_ctx, _ = _in_proj_bcd(h_ctx, lp)
            mix_ctx, ctx_states, _ = _token_mix(h_ctx, zero_state, x_ctx.shape[0], lp, pbcd_ctx, merge_w)

        mix_lat, _, w13 = _token_mix(h_lat, ctx_states, GRID_W, lp, pbcd_lat, merge_w, cast_ffn_weights=True)
        x_lat, h2 = resnorm(x_lat, mix_lat, ml[2], ng[1], (ng[2], ml[3], ml[4]))
        y, w2_bf16 = _ffn(h2, lp, w13)
        if last:
            x_lat = resnorm(x_lat, y, ml[5], ng[3])
        else:
            ngn = norm_g[l + 1, 0].reshape(1, d)
            mln = [mods[l + 1, 0:1, i * d:(i + 1) * d] for i in range(2)]
            mcn = [mods[l + 1, 1:2, i * d:(i + 1) * d] for i in range(2)]
            x_lat, h_lat = resnorm(x_lat, y, ml[5], ng[3], (ngn, mln[0], mln[1]))
            x_ctx, h2c = resnorm(x_ctx, mix_ctx, mc[2], ng[1], (ng[2], mc[3], mc[4]))
            yc, _ = _ffn(h2c, lp, w13, w2_bf16)
            x_ctx, h_ctx = resnorm(x_ctx, yc, mc[5], ng[3], (ngn, mcn[0], mcn[1]))
    return x_lat[None]
```
